```python
import jax, jax.numpy as jnp
from jax import lax
import numpy as np

D_MODEL = 1024
BATCH = 2
SEQ = 8192
DEPTH = 2

HEAD_DIM = 64
MLA_HEADS = 4
MLA_Q_LORA = 256
MLA_KV_LORA = 128
MLA_NOPE = 64
MLA_ROPE = 32
MLA_V = 64
DIL_HEADS = 6
DIL_PATTERNS = ((128, 1), (512, 4), (2048, 16))
NSA_HEADS = 6
NSA_KV_GROUPS = 2
NSA_CMP_BLOCK = 32
NSA_CMP_STRIDE = 16
NSA_CMP_HIDDEN = 256
NSA_SEL_BLOCK = 64
NSA_TOP_N = 16
NSA_WINDOW = 512

Q_BLOCK = 128
MIX_WIDTH = (MLA_HEADS + DIL_HEADS + NSA_HEADS) * HEAD_DIM
D_FF = ((8 * D_MODEL // 3 + 255) // 256) * 256
ROPE_THETA = 10000.0
LN_EPS = 1e-5
RMS_EPS = 1e-6
NEG_INF = -1e30
FORCE_SCORE = 1e4

IN_SPLITS = (
    MLA_Q_LORA, MLA_KV_LORA, MLA_ROPE,
    DIL_HEADS * HEAD_DIM, DIL_HEADS * HEAD_DIM, DIL_HEADS * HEAD_DIM,
    NSA_HEADS * HEAD_DIM,
) + (NSA_KV_GROUPS * HEAD_DIM,) * 6 + (NSA_HEADS * 3,)
IN_WIDTH = sum(IN_SPLITS)

kernel_name = 'hybrid_mla_dilated_nsa_macaron_deepnorm'


def layer_norm(x, g, b):
    xf = x.astype(jnp.float32)
    mu = jnp.mean(xf, axis=-1, keepdims=True)
    var = jnp.mean(jnp.square(xf - mu), axis=-1, keepdims=True)
    return ((xf - mu) * lax.rsqrt(var + LN_EPS) * g + b).astype(x.dtype)


def rms_norm(x, g):
    xf = x.astype(jnp.float32)
    return (xf * lax.rsqrt(jnp.mean(xf * xf, axis=-1, keepdims=True) + RMS_EPS) * g).astype(x.dtype)


def swiglu(x, w_in, w_out):
    gate, up = jnp.split(x @ w_in, 2, axis=-1)
    return (jax.nn.silu(gate) * up) @ w_out


def rope_cos_sin(n_pos, dim):
    inv_freq = ROPE_THETA ** (-jnp.arange(0, dim, 2, dtype=jnp.float32) / dim)
    ang = jnp.arange(n_pos, dtype=jnp.float32)[:, None] * inv_freq[None, :]
    return jnp.cos(ang), jnp.sin(ang)


def apply_rope(x, cos, sin):
    half = x.shape[-1] // 2
    x1 = x[..., :half].astype(jnp.float32)
    x2 = x[..., half:].astype(jnp.float32)
    c, s = cos[:, None, :], sin[:, None, :]
    return jnp.concatenate([x1 * c - x2 * s, x1 * s + x2 * c], axis=-1).astype(x.dtype)


def masked_softmax(s, mask):
    s = jnp.where(mask, s.astype(jnp.float32), NEG_INF)
    m = jnp.max(s, axis=-1, keepdims=True)
    e = jnp.where(mask, jnp.exp(s - m), 0.0)
    l = jnp.maximum(jnp.sum(e, axis=-1, keepdims=True), 1e-30)
    return e / l, (m + jnp.log(l))[..., 0]


def split_cols(h, sizes):
    out, start = [], 0
    for n in sizes:
        out.append(h[..., start:start + n])
        start += n
    return out


def causal_block_attention(q, k, v):
    B, S, H, dq = q.shape
    nb = S // Q_BLOCK
    qb = q.reshape(B, nb, Q_BLOCK, H, dq).transpose(1, 0, 2, 3, 4)
    key_pos = jnp.arange(S)
    scale = dq ** -0.5

    def one_block(args):
        i, q_blk = args
        s = jnp.einsum('bqhd,bkhd->bhqk', q_blk, k).astype(jnp.float32) * scale
        qpos = i * Q_BLOCK + jnp.arange(Q_BLOCK)
        p, _ = masked_softmax(s, key_pos[None, :] <= qpos[:, None])
        return jnp.einsum('bhqk,bkhd->bqhd', p.astype(v.dtype), v)

    o = lax.map(one_block, (jnp.arange(nb), qb))
    return o.transpose(1, 0, 2, 3, 4).reshape(B, S, H, v.shape[-1])


def banded_attention(q, k, v, window):
    L, d = q.shape[-2], q.shape[-1]
    nb = L // Q_BLOCK
    n_prev = -(-window // Q_BLOCK)

    def blocks(t):
        return t.reshape(t.shape[:-2] + (nb, Q_BLOCK, t.shape[-1]))

    def with_history(tb):
        pad = [(0, 0)] * (tb.ndim - 3) + [(n_prev, 0), (0, 0), (0, 0)]
        tp = jnp.pad(tb, pad)
        return jnp.concatenate([tp[..., j:j + nb, :, :] for j in range(n_prev + 1)], axis=-2)

    qb = blocks(q)
    kh, vh = with_history(blocks(k)), with_history(blocks(v))
    s = jnp.matmul(qb, jnp.swapaxes(kh, -1, -2)).astype(jnp.float32) * d ** -0.5
    qi = jnp.arange(Q_BLOCK)[:, None] + n_prev * Q_BLOCK
    kj = jnp.arange((n_prev + 1) * Q_BLOCK)[None, :]
    dist = qi - kj
    key_pos = jnp.arange(nb)[:, None, None] * Q_BLOCK - n_prev * Q_BLOCK + kj[None]
    mask = (dist >= 0) & (dist <= window) & (key_pos >= 0)
    p, lse = masked_softmax(s, mask)
    o = jnp.matmul(p.astype(vh.dtype), vh)
    return o.reshape(o.shape[:-3] + (L, d)), lse.reshape(lse.shape[:-2] + (L,))


def dilated_attention(q, k, v):
    B, S, H, d = q.shape
    outs, lses = [], []
    for window, dil in DIL_PATTERNS:
        span = dil * Q_BLOCK
        Sp = -(-S // span) * span

        def to_sub(t):
            t = jnp.pad(t, ((0, 0), (0, Sp - S), (0, 0), (0, 0)))
            return t.reshape(B, Sp // dil, dil, H, d).transpose(0, 2, 3, 1, 4)

        o, lse = banded_attention(to_sub(q), to_sub(k), to_sub(v), window // dil)
        outs.append(o.transpose(0, 3, 1, 2, 4).reshape(B, Sp, H, d)[:, :S])
        lses.append(lse.transpose(0, 3, 1, 2).reshape(B, Sp, H)[:, :S])
    w = jax.nn.softmax(jnp.stack(lses, axis=-1), axis=-1)
    return jnp.einsum('bshp,pbshd->bshd', w.astype(q.dtype), jnp.stack(outs))


def nsa_attention(q, k_cmp, v_cmp, k_slc, v_slc, k_win, v_win, gates,
                  cmp_pos, cmp_w1, cmp_w2, cos, sin):
    B, S, H, d = q.shape
    G, Hg = NSA_KV_GROUPS, NSA_HEADS // NSA_KV_GROUPS
    nb = S // Q_BLOCK
    scale = d ** -0.5
    q_rot = apply_rope(q, cos, sin)

    n_cmp = (S - NSA_CMP_BLOCK) // NSA_CMP_STRIDE + 1
    cmp_start = jnp.arange(n_cmp) * NSA_CMP_STRIDE
    cmp_end = cmp_start + NSA_CMP_BLOCK - 1
    cmp_idx = cmp_start[:, None] + jnp.arange(NSA_CMP_BLOCK)[None, :]

    def compress(t, pos, w1, w2):
        blk = t[:, cmp_idx] + pos[:, None, :]
        flat = blk.transpose(0, 1, 3, 2, 4).reshape(B, n_cmp, G, NSA_CMP_BLOCK * d)
        return (jax.nn.silu(flat @ w1) @ w2).transpose(0, 2, 1, 3)

    kc = compress(k_cmp, cmp_pos[0], cmp_w1[0], cmp_w2[0])
    vc = compress(v_cmp, cmp_pos[1], cmp_w1[1], cmp_w2[1])

    n_sel = S // NSA_SEL_BLOCK
    n_top = min(NSA_TOP_N, n_sel)
    sel_start = jnp.arange(n_sel) * NSA_SEL_BLOCK
    overlap = ((cmp_start[:, None] < sel_start[None, :] + NSA_SEL_BLOCK) &
               (cmp_end[:, None] >= sel_start[None, :])).astype(jnp.float32)
    ks = k_slc.transpose(0, 2, 1, 3)
    vs = v_slc.transpose(0, 2, 1, 3)
    sel_offsets = jnp.arange(NSA_SEL_BLOCK)
    blk_ids = jnp.arange(n_sel)[None, :]

    def grouped(t):
        return t.reshape(B, nb, Q_BLOCK, G, Hg, d).transpose(1, 0, 3, 4, 2, 5)

    def one_block(args):
        i, q_c, q_s = args
        qpos = i * Q_BLOCK + jnp.arange(Q_BLOCK)
        s = jnp.einsum('bghqd,bgcd->bghqc', q_c, kc).astype(jnp.float32) * scale
        p, _ = masked_softmax(s, cmp_end[None, :] <= qpos[:, None])
        o_c = jnp.einsum('bghqc,bgcd->bghqd', p.astype(vc.dtype), vc)
        imp = jnp.einsum('bgqc,cj->bgqj', jnp.sum(p, axis=2), overlap)
        cur = (qpos // NSA_SEL_BLOCK)[:, None]
        forced = (blk_ids == 0) | (blk_ids == cur) | (blk_ids == cur - 1)
        score = jnp.where(forced, FORCE_SCORE, jnp.where(blk_ids <= cur, imp, -FORCE_SCORE))
        _, top = lax.top_k(score, n_top)
        tok = (top[..., None] * NSA_SEL_BLOCK + sel_offsets).reshape(B, G, -1)
        n_keys = n_top * NSA_SEL_BLOCK
        kg = jnp.take_along_axis(ks, tok[..., None], axis=2).reshape(B, G, Q_BLOCK, n_keys, d)
        vg = jnp.take_along_axis(vs, tok[..., None], axis=2).reshape(B, G, Q_BLOCK, n_keys, d)
        s = jnp.einsum('bghqd,bgqkd->bghqk', q_s, kg).astype(jnp.float32) * scale
        kmask = tok.reshape(B, G, 1, Q_BLOCK, n_keys) <= qpos[:, None]
        p, _ = masked_softmax(s, kmask)
        o_s = jnp.einsum('bghqk,bgqkd->bghqd', p.astype(vg.dtype), vg)
        return o_c, o_s

    o_cmp, o_slc = lax.map(one_block, (jnp.arange(nb), grouped(q), grouped(q_rot)))

    def ungroup(t):
        return t.transpose(1, 0, 4, 2, 3, 5).reshape(B, S, H, d)

    q_w = q_rot.reshape(B, S, G, Hg, d).transpose(0, 2, 3, 1, 4)
    k_w = k_win.transpose(0, 2, 1, 3)[:, :, None]
    v_w = v_win.transpose(0, 2, 1, 3)[:, :, None]
    o_win, _ = banded_attention(q_w, k_w, v_w, NSA_WINDOW)
    o_win = o_win.transpose(0, 3, 1, 2, 4).reshape(B, S, H, d)
    g = gates.astype(q.dtype)
    return g[..., 0:1] * ungroup(o_cmp) + g[..., 1:2] * ungroup(o_slc) + g[..., 2:3] * o_win


def token_mixing(h, w_in, w_out, q_norm, kv_norm, w_uq, w_ukv, cmp_pos, cmp_w1, cmp_w2,
                 cos, sin, cos_m, sin_m):
    B, S, _ = h.shape
    (c_q, c_kv, k_pe, dq, dk, dv, nq, kc, vc, ksl, vsl, kw, vw, gl) = split_cols(h @ w_in, IN_SPLITS)

    def heads(t):
        return t.reshape(B, S, -1, HEAD_DIM)

    q_a = (rms_norm(c_q, q_norm) @ w_uq).reshape(B, S, MLA_HEADS, MLA_NOPE + MLA_ROPE)
    q_a = jnp.concatenate([q_a[..., :MLA_NOPE], apply_rope(q_a[..., MLA_NOPE:], cos_m, sin_m)], axis=-1)
    kv_a = (rms_norm(c_kv, kv_norm) @ w_ukv).reshape(B, S, MLA_HEADS, MLA_NOPE + MLA_V)
    k_pe = apply_rope(k_pe.reshape(B, S, 1, MLA_ROPE), cos_m, sin_m)
    k_a = jnp.concatenate([kv_a[..., :MLA_NOPE],
                           jnp.broadcast_to(k_pe, (B, S, MLA_HEADS, MLA_ROPE))], axis=-1)
    o_a = causal_block_attention(q_a, k_a, kv_a[..., MLA_NOPE:])

    o_b = dilated_attention(apply_rope(heads(dq), cos, sin), apply_rope(heads(dk), cos, sin), heads(dv))

    o_c = nsa_attention(heads(nq), heads(kc), heads(vc),
                        apply_rope(heads(ksl), cos, sin), heads(vsl),
                        apply_rope(heads(kw), cos, sin), heads(vw),
                        jax.nn.sigmoid(gl.astype(jnp.float32)).reshape(B, S, NSA_HEADS, 3),
                        cmp_pos, cmp_w1, cmp_w2, cos, sin)

    o = jnp.concatenate([o_a, o_b, o_c], axis=2).reshape(B, S, MIX_WIDTH)
    return o @ w_out


def setup_inputs(seed: int = 0) -> dict:
    key = jax.random.key(seed)
    k = jax.random.split(key, 14)
    beta = (8 * DEPTH) ** -0.25

    def nrm(kk, shape, scale):
        return jax.random.normal(kk, shape, jnp.float32) * scale

    return {
        'x': nrm(k[0], (BATCH, SEQ, D_MODEL), 1.0),
        'ffn_w_in': nrm(k[1], (DEPTH, 2, D_MODEL, 2 * D_FF), D_MODEL ** -0.5),
        'ffn_w_out': nrm(k[2], (DEPTH, 2, D_FF, D_MODEL), beta * D_FF ** -0.5),
        'ln_gain': 1.0 + nrm(k[3], (DEPTH, 3, D_MODEL), 0.02),
        'ln_bias': nrm(k[4], (DEPTH, 3, D_MODEL), 0.02),
        'w_in': nrm(k[5], (DEPTH, D_MODEL, IN_WIDTH), D_MODEL ** -0.5),
        'w_out': nrm(k[6], (DEPTH, MIX_WIDTH, D_MODEL), beta * MIX_WIDTH ** -0.5),
        'mla_q_norm': 1.0 + nrm(k[7], (DEPTH, MLA_Q_LORA), 0.02),
        'mla_kv_norm': 1.0 + nrm(k[8], (DEPTH, MLA_KV_LORA), 0.02),
        'mla_w_uq': nrm(k[9], (DEPTH, MLA_Q_LORA, MLA_HEADS * (MLA_NOPE + MLA_ROPE)), MLA_Q_LORA ** -0.5),
        'mla_w_ukv': nrm(k[10], (DEPTH, MLA_KV_LORA, MLA_HEADS * (MLA_NOPE + MLA_V)), MLA_KV_LORA ** -0.5),
        'nsa_cmp_pos': nrm(k[11], (DEPTH, 2, NSA_CMP_BLOCK, HEAD_DIM), 0.1),
        'nsa_cmp_w1': nrm(k[12], (DEPTH, 2, NSA_CMP_BLOCK * HEAD_DIM, NSA_CMP_HIDDEN),
                          (NSA_CMP_BLOCK * HEAD_DIM) ** -0.5),
        'nsa_cmp_w2': nrm(k[13], (DEPTH, 2, NSA_CMP_HIDDEN, HEAD_DIM), NSA_CMP_HIDDEN ** -0.5),
    }


def reference(x, ffn_w_in, ffn_w_out, ln_gain, ln_bias, w_in, w_out, mla_q_norm, mla_kv_norm,
              mla_w_uq, mla_w_ukv, nsa_cmp_pos, nsa_cmp_w1, nsa_cmp_w2):
    S = x.shape[1]
    alpha = (2 * DEPTH) ** 0.25
    cos, sin = rope_cos_sin(S, HEAD_DIM)
    cos_m, sin_m = rope_cos_sin(S, MLA_ROPE)
    for l in range(DEPTH):
        x = layer_norm(alpha * x + 0.5 * swiglu(x, ffn_w_in[l, 0], ffn_w_out[l, 0]),
                       ln_gain[l, 0], ln_bias[l, 0])
        mix = token_mixing(x, w_in[l], w_out[l], mla_q_norm[l], mla_kv_norm[l], mla_w_uq[l],
                           mla_w_ukv[l], nsa_cmp_pos[l], nsa_cmp_w1[l], nsa_cmp_w2[l],
                           cos, sin, cos_m, sin_m)
        x = layer_norm(alpha * x + mix, ln_gain[l, 1], ln_bias[l, 1])
        x = layer_norm(alpha * x + 0.5 * swiglu(x, ffn_w_in[l, 1], ffn_w_out[l, 1]),
                       ln_gain[l, 2], ln_bias[l, 2])
    return x
```

```python
import functools

import numpy as np
import jax
import jax.numpy as jnp
from jax import lax
from jax.experimental import pallas as pl
from jax.experimental.pallas import tpu as pltpu

F32 = jnp.float32
BF16 = jnp.bfloat16

D_MODEL = 1024
DEPTH = 2
HEAD_DIM = 64
MLA_HEADS = 4
MLA_Q_LORA = 256
MLA_KV_LORA = 128
MLA_NOPE = 64
MLA_ROPE = 32
MLA_V = 64
DIL_HEADS = 6
DIL_PATTERNS = ((128, 1), (512, 4), (2048, 16))
NSA_HEADS = 6
NSA_KV_GROUPS = 2
NSA_HEADS_PER_GROUP = NSA_HEADS // NSA_KV_GROUPS
NSA_CMP_BLOCK = 32
NSA_CMP_STRIDE = 16
NSA_CMP_HIDDEN = 256
NSA_SEL_BLOCK = 64
NSA_TOP_N = 16
NSA_WINDOW = 512
D_FF = ((8 * D_MODEL // 3 + 255) // 256) * 256
ROPE_THETA = 10000.0
LN_EPS = 1e-5
RMS_EPS = 1e-6
NEG_INF = -1e30
FORCE_SCORE = 1e4
ALPHA = (2 * DEPTH) ** 0.25

LANES = 128
SEL_LANES = 128
UNSELECTED_BIAS = -1e9
VMEM_LIMIT = 56 * 1024 * 1024

_SRC_CQ, _SRC_CKV, _SRC_KPE = 0, 256, 384
_SRC_DQ, _SRC_DK, _SRC_DV = 416, 800, 1184
_SRC_NQ = 1568
_SRC_KC, _SRC_VC, _SRC_KSL, _SRC_VSL, _SRC_KW, _SRC_VW = 1952, 2080, 2208, 2336, 2464, 2592
_SRC_GL = 2720
_C_CQ, _C_CKV, _C_KPE, _C_DQ, _C_DK, _C_DV, _C_NQ = 0, 256, 384, 896, 1664, 2048, 2432
_C_KC, _C_VC, _C_KSL, _C_VSL, _C_KW, _C_VW, _C_GL = 3200, 3328, 3456, 3584, 3712, 3840, 3968
_PROJ_WIDTH = 4096


def _cparams(*sem):
    return pltpu.CompilerParams(dimension_semantics=sem, vmem_limit_bytes=VMEM_LIMIT)


def _layer_norm(y, g, b):
    mu = jnp.mean(y, axis=-1, keepdims=True)
    d = y - mu
    var = jnp.mean(d * d, axis=-1, keepdims=True)
    return d * lax.rsqrt(var + LN_EPS) * g + b


def _dot(a, b):
    return jnp.dot(a, b, preferred_element_type=F32)


def _dot_nt(a, b):
    return lax.dot_general(a, b, (((1,), (1,)), ((), ())), preferred_element_type=F32)


def _lane_half(shape):
    return lax.shift_right_logical(lax.broadcasted_iota(jnp.int32, shape, len(shape) - 1), 6)


def _ffn_kernel(x_ref, wg_ref, wu_ref, wo_ref, g_ref, b_ref, o_ref, xb_ref, acc_ref):
    j = pl.program_id(1)

    @pl.when(j == 0)
    def _():
        xb_ref[...] = x_ref[...].astype(BF16)
        acc_ref[...] = jnp.zeros_like(acc_ref)

    xb = xb_ref[...]
    gate = _dot(xb, wg_ref[...])
    up = _dot(xb, wu_ref[...])
    h = (gate * jax.nn.sigmoid(gate) * up).astype(BF16)
    acc_ref[...] += _dot(h, wo_ref[...])

    @pl.when(j == pl.num_programs(1) - 1)
    def _():
        y = ALPHA * x_ref[...] + 0.5 * acc_ref[...]
        o_ref[...] = _layer_norm(y, g_ref[...], b_ref[...])


def _ffn_ln(x, w_in, w_out, gain, bias, *, tm=1024, tf=256):
    T = x.shape[0]
    nf = D_FF // tf
    return pl.pallas_call(
        _ffn_kernel,
        grid=(T // tm, nf),
        in_specs=[
            pl.BlockSpec((tm, D_MODEL), lambda i, j: (i, 0)),
            pl.BlockSpec((D_MODEL, tf), lambda i, j: (0, j)),
            pl.BlockSpec((D_MODEL, tf), lambda i, j: (0, j + nf)),
            pl.BlockSpec((tf, D_MODEL), lambda i, j: (j, 0)),
            pl.BlockSpec((1, D_MODEL), lambda i, j: (0, 0)),
            pl.BlockSpec((1, D_MODEL), lambda i, j: (0, 0)),
        ],
        out_specs=pl.BlockSpec((tm, D_MODEL), lambda i, j: (i, 0)),
        out_shape=jax.ShapeDtypeStruct((T, D_MODEL), F32),
        scratch_shapes=[pltpu.VMEM((tm, D_MODEL), BF16), pltpu.VMEM((tm, D_MODEL), F32)],
        compiler_params=_cparams("parallel", "arbitrary"),
        name="ffn_ln",
    )(x, w_in, w_in, w_out, gain.reshape(1, -1), bias.reshape(1, -1))


def _rope_chunk(x, c, sa, sb, half):
    return x * c + pltpu.roll(x, LANES - half, 1) * sa + pltpu.roll(x, half, 1) * sb


def _rms_norm(x, g):
    return x * lax.rsqrt(jnp.mean(x * x, axis=-1, keepdims=True) + RMS_EPS) * g


def _proj_kernel(x_ref, w_ref, qn_ref, kvn_ref, wuq_ref, wkn_ref, wv_ref,
                 c64_ref, sa64_ref, sb64_ref, cm_ref, sam_ref, sbm_ref,
                 qa_ref, ka_ref, va_ref, dq_ref, dk_ref, dv_ref, nq_ref, nqr_ref,
                 kc_ref, vc_ref, ksl_ref, vsl_ref, kw_ref, vw_ref, gl_ref):
    xb = x_ref[...].astype(BF16)
    c64, sa64, sb64 = c64_ref[...], sa64_ref[...], sb64_ref[...]
    cm, sam, sbm = cm_ref[...], sam_ref[...], sbm_ref[...]

    def proj(off, width):
        return _dot(xb, w_ref[:, off:off + width])

    def rope64(v):
        return _rope_chunk(v, c64, sa64, sb64, HEAD_DIM // 2)

    def rope_mla(v):
        return _rope_chunk(v, cm, sam, sbm, MLA_ROPE // 2)

    cq = _rms_norm(proj(_C_CQ, MLA_Q_LORA), qn_ref[...]).astype(BF16)
    q_raw = _dot(cq, wuq_ref[...])
    ckv = _rms_norm(proj(_C_CKV, MLA_KV_LORA), kvn_ref[...]).astype(BF16)
    k_nope = _dot(ckv, wkn_ref[...])
    va_ref[...] = _dot(ckv, wv_ref[...]).astype(BF16)
    kpe = proj(_C_KPE, MLA_HEADS * LANES)
    q_scale = (MLA_NOPE + MLA_ROPE) ** -0.5
    for h in range(MLA_HEADS):
        sl = slice(h * LANES, (h + 1) * LANES)
        qa_ref[:, sl] = (rope_mla(q_raw[:, sl]) * q_scale).astype(BF16)
        ka_ref[:, sl] = (k_nope[:, sl] + rope_mla(kpe[:, sl])).astype(BF16)

    scale = HEAD_DIM ** -0.5
    dq = proj(_C_DQ, DIL_HEADS * LANES)
    for h in range(DIL_HEADS):
        sl = slice(h * LANES, (h + 1) * LANES)
        dq_ref[:, sl] = (rope64(dq[:, sl]) * scale).astype(BF16)
    dk = proj(_C_DK, DIL_HEADS * HEAD_DIM)
    for c in range(DIL_HEADS * HEAD_DIM // LANES):
        sl = slice(c * LANES, (c + 1) * LANES)
        dk_ref[:, sl] = rope64(dk[:, sl]).astype(BF16)
    dv_ref[...] = proj(_C_DV, DIL_HEADS * HEAD_DIM).astype(BF16)

    nq = proj(_C_NQ, NSA_HEADS * LANES)
    nq_ref[...] = (nq * scale).astype(BF16)
    for h in range(NSA_HEADS):
        sl = slice(h * LANES, (h + 1) * LANES)
        nqr_ref[:, sl] = (rope64(nq[:, sl]) * scale).astype(BF16)
    kc_ref[...] = proj(_C_KC, LANES)
    vc_ref[...] = proj(_C_VC, LANES)
    ksl_ref[...] = rope64(proj(_C_KSL, LANES)).astype(BF16)
    vsl_ref[...] = proj(_C_VSL, LANES).astype(BF16)
    kw_ref[...] = rope64(proj(_C_KW, LANES)).astype(BF16)
    vw_ref[...] = proj(_C_VW, LANES).astype(BF16)
    gl_ref[...] = jax.nn.sigmoid(proj(_C_GL, LANES))


def _proj_source_columns():
    src = -np.ones(_PROJ_WIDTH, np.int64)

    def put(dst, s0, n):
        src[dst:dst + n] = s0 + np.arange(n)

    put(_C_CQ, _SRC_CQ, MLA_Q_LORA)
    put(_C_CKV, _SRC_CKV, MLA_KV_LORA)
    for h in range(MLA_HEADS):
        put(_C_KPE + h * LANES + MLA_NOPE, _SRC_KPE, MLA_ROPE)
    for h in range(DIL_HEADS):
        put(_C_DQ + h * LANES + HEAD_DIM * (h % 2), _SRC_DQ + h * HEAD_DIM, HEAD_DIM)
    put(_C_DK, _SRC_DK, DIL_HEADS * HEAD_DIM)
    put(_C_DV, _SRC_DV, DIL_HEADS * HEAD_DIM)
    for h in range(NSA_HEADS):
        put(_C_NQ + h * LANES + HEAD_DIM * (h // NSA_HEADS_PER_GROUP), _SRC_NQ + h * HEAD_DIM, HEAD_DIM)
    for dst, s0 in ((_C_KC, _SRC_KC), (_C_VC, _SRC_VC), (_C_KSL, _SRC_KSL), (_C_VSL, _SRC_VSL),
                    (_C_KW, _SRC_KW), (_C_VW, _SRC_VW)):
        put(dst, s0, NSA_KV_GROUPS * HEAD_DIM)
    put(_C_GL, _SRC_GL, NSA_HEADS * 3)
    return src


def _gather_columns(w, src, axis):
    valid = jnp.asarray(src >= 0)
    taken = jnp.take(w, jnp.asarray(np.maximum(src, 0)), axis=axis)
    shape = [1, 1]
    shape[axis] = -1
    return jnp.where(valid.reshape(shape), taken, 0.0)


def _mla_weight_layout(w_uq, w_ukv):
    dq = MLA_NOPE + MLA_ROPE
    src_q = -np.ones(MLA_HEADS * LANES, np.int64)
    src_kn = -np.ones(MLA_HEADS * LANES, np.int64)
    src_v = np.zeros(MLA_HEADS * MLA_V, np.int64)
    for h in range(MLA_HEADS):
        src_q[h * LANES:h * LANES + dq] = h * dq + np.arange(dq)
        src_kn[h * LANES:h * LANES + MLA_NOPE] = h * (MLA_NOPE + MLA_V) + np.arange(MLA_NOPE)
        src_v[h * MLA_V:(h + 1) * MLA_V] = h * (MLA_NOPE + MLA_V) + MLA_NOPE + np.arange(MLA_V)
    return (_gather_columns(w_uq, src_q, 1).astype(BF16),
            _gather_columns(w_ukv, src_kn, 1).astype(BF16),
            _gather_columns(w_ukv, src_v, 1).astype(BF16))


def _rope_tables(S):
    def cos_sin(dim):
        inv_freq = ROPE_THETA ** (-jnp.arange(0, dim, 2, dtype=F32) / dim)
        ang = jnp.arange(S, dtype=F32)[:, None] * inv_freq[None, :]
        return jnp.cos(ang), jnp.sin(ang)

    cos, sin = cos_sin(HEAD_DIM)
    zero = jnp.zeros_like(sin)
    c64 = jnp.concatenate([cos, cos] * 2, axis=1)
    sa64 = jnp.concatenate([-sin, zero] * 2, axis=1)
    sb64 = jnp.concatenate([zero, sin] * 2, axis=1)
    cos_m, sin_m = cos_sin(MLA_ROPE)
    ones = jnp.ones((S, MLA_NOPE), F32)
    z64 = jnp.zeros((S, MLA_NOPE), F32)
    z16 = jnp.zeros_like(sin_m)
    z32 = jnp.zeros((S, LANES - MLA_NOPE - MLA_ROPE), F32)
    cm = jnp.concatenate([ones, cos_m, cos_m, z32], axis=1)
    sam = jnp.concatenate([z64, -sin_m, z16, z32], axis=1)
    sbm = jnp.concatenate([z64, z16, sin_m, z32], axis=1)
    return c64, sa64, sb64, cm, sam, sbm


def _input_projection(h, w_big, q_norm, kv_norm, wuq, wkn, wv, tables, S, *, tm=512):
    T = h.shape[0]
    n_pos = S // tm
    row = lambda w: pl.BlockSpec((tm, w), lambda i: (i, 0))
    full = lambda a: pl.BlockSpec(a.shape, lambda i: (0, 0))
    tab = pl.BlockSpec((tm, LANES), lambda i: (i % n_pos, 0))
    out_widths = [(MLA_HEADS * LANES, BF16), (MLA_HEADS * LANES, BF16), (MLA_HEADS * MLA_V, BF16),
                  (DIL_HEADS * LANES, BF16), (DIL_HEADS * HEAD_DIM, BF16), (DIL_HEADS * HEAD_DIM, BF16),
                  (NSA_HEADS * LANES, BF16), (NSA_HEADS * LANES, BF16),
                  (LANES, F32), (LANES, F32), (LANES, BF16), (LANES, BF16), (LANES, BF16), (LANES, BF16),
                  (LANES, F32)]
    qn = q_norm.reshape(1, -1)
    kvn = kv_norm.reshape(1, -1)
    return pl.pallas_call(
        _proj_kernel,
        grid=(T // tm,),
        in_specs=[row(D_MODEL), full(w_big), full(qn), full(kvn), full(wuq), full(wkn), full(wv)]
                 + [tab] * 6,
        out_specs=[row(w) for w, _ in out_widths],
        out_shape=[jax.ShapeDtypeStruct((T, w), dt) for w, dt in out_widths],
        compiler_params=_cparams("parallel"),
        name="in_proj",
    )(h, w_big, qn, kvn, wuq, wkn, wv, *tables)


def _online_softmax_step(s, v, m_ref, l_ref, acc_ref):
    m_prev = m_ref[...]
    m_new = jnp.maximum(m_prev, jnp.max(s, axis=1, keepdims=True))
    alpha = jnp.exp(m_prev - m_new)
    p = jnp.exp(s - m_new)
    l_ref[...] = alpha * l_ref[...] + jnp.sum(p, axis=1, keepdims=True)
    acc_ref[...] = alpha * acc_ref[...] + _dot(p.astype(BF16), v)
    m_ref[...] = m_new


def _init_softmax_state(m_ref, l_ref, acc_ref):
    m_ref[...] = jnp.full(m_ref.shape, NEG_INF, F32)
    l_ref[...] = jnp.zeros_like(l_ref)
    acc_ref[...] = jnp.zeros_like(acc_ref)


def _mla_kernel(q_ref, k_ref, v_ref, o_ref, m_ref, l_ref, acc_ref, *, t):
    qi = pl.program_id(1)
    half = _lane_half((t, LANES))
    row = lax.broadcasted_iota(jnp.int32, (t, t), 0)
    col = lax.broadcasted_iota(jnp.int32, (t, t), 1)
    for pair in range(MLA_HEADS // 2):
        vsl = slice(pair * LANES, (pair + 1) * LANES)
        out = jnp.zeros((t, LANES), F32)
        for hh in range(2):
            h = 2 * pair + hh
            hsl = slice(h * LANES, (h + 1) * LANES)
            q = q_ref[0, :, hsl]
            _init_softmax_state(m_ref, l_ref, acc_ref)

            def tile(kt, masked):
                start = pl.multiple_of(kt * t, t)
                k = k_ref[0, pl.ds(start, t), hsl]
                v = v_ref[0, pl.ds(start, t), vsl]
                v = jnp.where(half == hh, v, jnp.zeros_like(v))
                s = _dot_nt(q, k)
                if masked:
                    s = jnp.where(col <= row, s, NEG_INF)
                _online_softmax_step(s, v, m_ref, l_ref, acc_ref)

            def body(kt, carry):
                tile(kt, False)
                return carry

            lax.fori_loop(0, qi, body, 0)
            tile(qi, True)
            out = out + acc_ref[...] / l_ref[...]
        o_ref[0, :, vsl] = out.astype(o_ref.dtype)


def _mla_attention(q, k, v, *, t=512):
    B, S, _ = q.shape
    return pl.pallas_call(
        functools.partial(_mla_kernel, t=t),
        grid=(B, S // t),
        in_specs=[
            pl.BlockSpec((1, t, MLA_HEADS * LANES), lambda b, i: (b, i, 0)),
            pl.BlockSpec((1, S, MLA_HEADS * LANES), lambda b, i: (b, 0, 0)),
            pl.BlockSpec((1, S, MLA_HEADS * MLA_V), lambda b, i: (b, 0, 0)),
        ],
        out_specs=pl.BlockSpec((1, t, MLA_HEADS * MLA_V), lambda b, i: (b, i, 0)),
        out_shape=jax.ShapeDtypeStruct((B, S, MLA_HEADS * MLA_V), BF16),
        scratch_shapes=[pltpu.VMEM((t, 1), F32), pltpu.VMEM((t, 1), F32), pltpu.VMEM((t, LANES), F32)],
        compiler_params=_cparams("parallel", "arbitrary"),
        name="mla_attn",
    )(q, k, v)


def _band_mask(q0, kstart, tq, span, window):
    qpos = q0 + lax.broadcasted_iota(jnp.int32, (tq, span), 0)
    kpos = kstart + lax.broadcasted_iota(jnp.int32, (tq, span), 1)
    dist = qpos - kpos
    return (dist >= 0) & (dist <= window)


def _dil_kernel(q_ref, k_ref, v_ref, o_ref, lse_ref, *, tq, wpad, window):
    q0 = pl.program_id(2) * tq
    n = k_ref.shape[1]
    span = min(tq + wpad, n)
    kstart = pl.multiple_of(jnp.clip(q0 - wpad, 0, n - span), LANES)
    mask = _band_mask(q0, kstart, tq, span, window)
    half_v = _lane_half((span, LANES))
    half_o = _lane_half((tq, LANES))
    for c in range(DIL_HEADS // 2):
        csl = slice(c * LANES, (c + 1) * LANES)
        k = k_ref[0, pl.ds(kstart, span), csl]
        v = v_ref[0, pl.ds(kstart, span), csl]
        out = jnp.zeros((tq, LANES), F32)
        lse = jnp.zeros((tq, LANES), F32)
        for hh in range(2):
            h = 2 * c + hh
            q = q_ref[0, :, h * LANES:(h + 1) * LANES]
            s = jnp.where(mask, _dot_nt(q, k), NEG_INF)
            m = jnp.max(s, axis=1, keepdims=True)
            e = jnp.exp(s - m)
            l = jnp.sum(e, axis=1, keepdims=True)
            vm = jnp.where(half_v == hh, v, jnp.zeros_like(v))
            out = out + _dot(e.astype(BF16), vm) / l
            lse = lse + jnp.where(half_o == hh, m + jnp.log(l), 0.0)
        o_ref[0, :, csl] = out
        lse_ref[0, :, csl] = lse


def _dilated_pattern(q, k, v, window, dil, *, tq=128):
    B, S, _ = q.shape
    n = S // dil
    wsub = window // dil
    wpad = -(-wsub // LANES) * LANES
    wq, wk = DIL_HEADS * LANES, DIL_HEADS * HEAD_DIM
    qv = q.reshape(B, n, dil * wq)
    kv = k.reshape(B, n, dil * wk)
    vv = v.reshape(B, n, dil * wk)
    o, lse = pl.pallas_call(
        functools.partial(_dil_kernel, tq=tq, wpad=wpad, window=wsub),
        grid=(B, dil, n // tq),
        in_specs=[
            pl.BlockSpec((1, tq, wq), lambda b, j, i: (b, i, j)),
            pl.BlockSpec((1, n, wk), lambda b, j, i: (b, 0, j)),
            pl.BlockSpec((1, n, wk), lambda b, j, i: (b, 0, j)),
        ],
        out_specs=[pl.BlockSpec((1, tq, wk), lambda b, j, i: (b, i, j))] * 2,
        out_shape=[jax.ShapeDtypeStruct((B, n, dil * wk), F32)] * 2,
        compiler_params=_cparams("parallel", "parallel", "arbitrary"),
        name=f"dilated_r{dil}",
    )(qv, kv, vv)
    return o.reshape(B * S, wk), lse.reshape(B * S, wk)


def _compress_kernel(xk_ref, xv_ref, plo_ref, phi_ref, w1_ref, w2_ref, kc_ref, vc_ref):
    n = xk_ref.shape[2]
    half_rows = w1_ref.shape[1] // 2
    for t, (x_ref, o_ref) in enumerate(((xk_ref, kc_ref), (xv_ref, vc_ref))):
        x = x_ref[0, 0]
        a = _dot((x + plo_ref[t]).astype(BF16), w1_ref[t, :half_rows, :])
        b = _dot((x + phi_ref[t]).astype(BF16), w1_ref[t, half_rows:, :])
        hid = a + pltpu.roll(b, n - 1, 0)
        hid = (hid * jax.nn.sigmoid(hid)).astype(BF16)
        o_ref[0, 0] = _dot(hid, w2_ref[t])


def _nsa_compress(kc, vc, cmp_pos, cmp_w1, cmp_w2):
    B, S, _ = kc.shape
    G, d, st = NSA_KV_GROUPS, HEAD_DIM, NSA_CMP_STRIDE
    n = S // st

    def rows(t):
        return t.reshape(B, n, st, G, d).transpose(0, 3, 1, 2, 4).reshape(B, G, n, st * d)

    pos = cmp_pos.reshape(2, NSA_CMP_BLOCK // st, 1, st * d)
    xspec = pl.BlockSpec((1, 1, n, st * d), lambda b, g: (b, g, 0, 0))
    ospec = pl.BlockSpec((1, 1, n, d), lambda b, g: (b, g, 0, 0))
    full = lambda a: pl.BlockSpec(a.shape, lambda b, g: (0,) * a.ndim)
    plo, phi = pos[:, 0], pos[:, 1]
    w1 = cmp_w1.astype(BF16)
    w2 = cmp_w2.astype(BF16)
    kcc, vcc = pl.pallas_call(
        _compress_kernel,
        grid=(B, G),
        in_specs=[xspec, xspec, full(plo), full(phi), full(w1), full(w2)],
        out_specs=[ospec, ospec],
        out_shape=[jax.ShapeDtypeStruct((B, G, n, d), F32)] * 2,
        compiler_params=_cparams("parallel", "parallel"),
        name="nsa_compress",
    )(rows(kc), rows(vc), plo, phi, w1, w2)
    pair = lambda t: t.transpose(0, 2, 1, 3).reshape(B, n, G * d).astype(BF16)
    return pair(kcc), pair(vcc)


def _dot_f32_by_01(p, o01):
    p1 = p.astype(BF16)
    r1 = p - p1.astype(F32)
    p2 = r1.astype(BF16)
    p3 = (r1 - p2.astype(F32)).astype(BF16)
    return _dot(p1, o01) + _dot(p2, o01) + _dot(p3, o01)


def _cmp_kernel(q_ref, kc_ref, vc_ref, ov_ref, o_ref, bias_ref, sc_ref, *, tq):
    g = pl.program_id(1)
    q0 = pl.program_id(2) * tq
    n = kc_ref.shape[1]
    kc = kc_ref[0]
    vc = vc_ref[0]
    qpos = q0 + lax.broadcasted_iota(jnp.int32, (tq, n), 0)
    cmp_end = lax.broadcasted_iota(jnp.int32, (tq, n), 1) * NSA_CMP_STRIDE + (NSA_CMP_BLOCK - 1)
    visible = cmp_end <= qpos
    in_group = _lane_half((tq, LANES)) == g
    psum = jnp.zeros((tq, n), F32)
    for hh in range(NSA_HEADS_PER_GROUP):
        hsl = slice(hh * LANES, (hh + 1) * LANES)
        s = jnp.where(visible, _dot_nt(q_ref[0, :, hsl], kc), NEG_INF)
        m = jnp.max(s, axis=1, keepdims=True)
        e = jnp.where(visible, jnp.exp(s - m), 0.0)
        l = jnp.maximum(jnp.sum(e, axis=1, keepdims=True), 1e-30)
        p = e / l
        o = _dot(p.astype(BF16), vc)
        o_ref[0, :, hsl] = jnp.where(in_group, o, 0.0).astype(o_ref.dtype)
        psum = psum + p

    imp_t = _dot_f32_by_01(psum, ov_ref[...]).T
    blk = lax.broadcasted_iota(jnp.int32, (SEL_LANES, tq), 0)
    cur = lax.shift_right_logical(q0 + lax.broadcasted_iota(jnp.int32, (SEL_LANES, tq), 1), 6)
    forced = (blk == 0) | (blk == cur) | (blk == cur - 1)
    sc_ref[...] = jnp.where(forced, FORCE_SCORE, jnp.where(blk <= cur, imp_t, -FORCE_SCORE))
    sc = sc_ref[...]

    def count(i, cnt):
        row = sc_ref[pl.ds(i, 1), :]
        ahead = (row > sc) | ((row == sc) & (blk > i))
        return cnt + jnp.where(ahead, 1.0, 0.0)

    n_causal = jnp.minimum(lax.shift_right_logical(q0 + tq - 1, 6) + 1, SEL_LANES)
    rank = lax.fori_loop(0, n_causal, count, jnp.zeros((SEL_LANES, tq), F32))
    bias_t = jnp.where(rank < float(NSA_TOP_N), 0.0, UNSELECTED_BIAS)
    bias_ref[0, 0] = bias_t.T.astype(bias_ref.dtype)


def _nsa_compressed_branch(nq, kcc, vcc, overlap, *, tq=128):
    B, S, _ = nq.shape
    G = NSA_KV_GROUPS
    n = kcc.shape[1]
    wq = NSA_HEADS_PER_GROUP * LANES
    return pl.pallas_call(
        functools.partial(_cmp_kernel, tq=tq),
        grid=(B, G, S // tq),
        in_specs=[
            pl.BlockSpec((1, tq, wq), lambda b, g, i: (b, i, g)),
            pl.BlockSpec((1, n, LANES), lambda b, g, i: (b, 0, 0)),
            pl.BlockSpec((1, n, LANES), lambda b, g, i: (b, 0, 0)),
            pl.BlockSpec((n, SEL_LANES), lambda b, g, i: (0, 0)),
        ],
        out_specs=[
            pl.BlockSpec((1, tq, wq), lambda b, g, i: (b, i, g)),
            pl.BlockSpec((1, 1, tq, SEL_LANES), lambda b, g, i: (b, g, i, 0)),
        ],
        out_shape=[jax.ShapeDtypeStruct((B, S, NSA_HEADS * LANES), BF16),
                   jax.ShapeDtypeStruct((B, G, S, SEL_LANES), BF16)],
        scratch_shapes=[pltpu.VMEM((SEL_LANES, tq), F32)],
        compiler_params=_cparams("parallel", "parallel", "arbitrary"),
        name="nsa_cmp_topk",
    )(nq, kcc, vcc, overlap)


def _sel_kernel(q_ref, bias_ref, k_ref, v_ref, oh_ref, o_ref, m_ref, l_ref, acc_ref, *, tq, tk):
    g = pl.program_id(1)
    q0 = pl.program_id(2) * tq
    nh = NSA_HEADS_PER_GROUP
    bias = bias_ref[0, 0]
    qs = jnp.concatenate(
        [jnp.concatenate([q_ref[0, :, hh * LANES:(hh + 1) * LANES], bias], axis=1) for hh in range(nh)],
        axis=0)
    qpos = q0 + lax.broadcasted_iota(jnp.int32, (tq, tk), 0)
    kofs = lax.broadcasted_iota(jnp.int32, (tq, tk), 1)
    _init_softmax_state(m_ref, l_ref, acc_ref)

    def tile(kt, masked):
        start = pl.multiple_of(kt * tk, tk)
        ka = jnp.concatenate([k_ref[0, pl.ds(start, tk), :], oh_ref[pl.ds(start, tk), :]], axis=1)
        s = _dot_nt(qs, ka)
        if masked:
            causal = (start + kofs) <= qpos
            s = jnp.where(jnp.concatenate([causal] * nh, axis=0), s, NEG_INF)
        _online_softmax_step(s, v_ref[0, pl.ds(start, tk), :], m_ref, l_ref, acc_ref)

    def body(kt, carry):
        tile(kt, False)
        return carry

    n_full = q0 // tk
    lax.fori_loop(0, n_full, body, 0)
    tile(n_full, True)
    out = acc_ref[...] / l_ref[...]
    in_group = _lane_half((tq, LANES)) == g
    for hh in range(nh):
        o_ref[0, :, hh * LANES:(hh + 1) * LANES] = jnp.where(
            in_group, out[hh * tq:(hh + 1) * tq], 0.0).astype(o_ref.dtype)


def _nsa_selected_branch(nqr, bias, ksl, vsl, onehot, *, tq=256, tk=512):
    B, S, _ = nqr.shape
    G = NSA_KV_GROUPS
    wq = NSA_HEADS_PER_GROUP * LANES
    rows = NSA_HEADS_PER_GROUP * tq
    return pl.pallas_call(
        functools.partial(_sel_kernel, tq=tq, tk=tk),
        grid=(B, G, S // tq),
        in_specs=[
            pl.BlockSpec((1, tq, wq), lambda b, g, i: (b, i, g)),
            pl.BlockSpec((1, 1, tq, SEL_LANES), lambda b, g, i: (b, g, i, 0)),
            pl.BlockSpec((1, S, LANES), lambda b, g, i: (b, 0, 0)),
            pl.BlockSpec((1, S, LANES), lambda b, g, i: (b, 0, 0)),
            pl.BlockSpec((S, SEL_LANES), lambda b, g, i: (0, 0)),
        ],
        out_specs=pl.BlockSpec((1, tq, wq), lambda b, g, i: (b, i, g)),
        out_shape=jax.ShapeDtypeStruct((B, S, NSA_HEADS * LANES), BF16),
        scratch_shapes=[pltpu.VMEM((rows, 1), F32), pltpu.VMEM((rows, 1), F32),
                        pltpu.VMEM((rows, LANES), F32)],
        compiler_params=_cparams("parallel", "parallel", "arbitrary"),
        name="nsa_selected",
    )(nqr, bias, ksl, vsl, onehot)


def _win_kernel(q_ref, k_ref, v_ref, o_ref, *, tq, wpad, window):
    g = pl.program_id(1)
    q0 = pl.program_id(2) * tq
    nh = NSA_HEADS_PER_GROUP
    span = tq + wpad
    kstart = pl.multiple_of(jnp.maximum(q0 - wpad, 0), LANES)
    mask = _band_mask(q0, kstart, tq, span, window)
    k = k_ref[0, pl.ds(kstart, span), :]
    v = v_ref[0, pl.ds(kstart, span), :]
    in_group = _lane_half((tq, LANES)) == g
    for hh in range(nh):
        hsl = slice(hh * LANES, (hh + 1) * LANES)
        s = jnp.where(mask, _dot_nt(q_ref[0, :, hsl], k), NEG_INF)
        m = jnp.max(s, axis=1, keepdims=True)
        e = jnp.exp(s - m)
        l = jnp.sum(e, axis=1, keepdims=True)
        o = _dot(e.astype(BF16), v) / l
        o_ref[0, :, hsl] = jnp.where(in_group, o, 0.0).astype(o_ref.dtype)


def _nsa_window_branch(nqr, kw, vw, *, tq=128):
    B, S, _ = nqr.shape
    G = NSA_KV_GROUPS
    wq = NSA_HEADS_PER_GROUP * LANES
    wpad = -(-NSA_WINDOW // LANES) * LANES
    return pl.pallas_call(
        functools.partial(_win_kernel, tq=tq, wpad=wpad, window=NSA_WINDOW),
        grid=(B, G, S // tq),
        in_specs=[
            pl.BlockSpec((1, tq, wq), lambda b, g, i: (b, i, g)),
            pl.BlockSpec((1, S, LANES), lambda b, g, i: (b, 0, 0)),
            pl.BlockSpec((1, S, LANES), lambda b, g, i: (b, 0, 0)),
        ],
        out_specs=pl.BlockSpec((1, tq, wq), lambda b, g, i: (b, i, g)),
        out_shape=jax.ShapeDtypeStruct((B, S, NSA_HEADS * LANES), BF16),
        compiler_params=_cparams("parallel", "parallel", "arbitrary"),
        name="nsa_window",
    )(nqr, kw, vw)


def _selection_constants(S):
    n_cmp_rows = S // NSA_CMP_STRIDE
    c = np.arange(n_cmp_rows)
    cmp_start = c * NSA_CMP_STRIDE
    cmp_end = cmp_start + NSA_CMP_BLOCK - 1
    sel_start = np.arange(SEL_LANES) * NSA_SEL_BLOCK
    overlap = ((cmp_start[:, None] < sel_start[None, :] + NSA_SEL_BLOCK) &
               (cmp_end[:, None] >= sel_start[None, :]))
    n_cmp = (S - NSA_CMP_BLOCK) // NSA_CMP_STRIDE + 1
    overlap &= (c < n_cmp)[:, None]
    onehot = (np.arange(S)[:, None] // NSA_SEL_BLOCK) == np.arange(SEL_LANES)[None, :]
    return jnp.asarray(overlap, BF16), jnp.asarray(onehot, BF16)


def _out_kernel(x_ref, oa_ref, d0_ref, d1_ref, d2_ref, l0_ref, l1_ref, l2_ref,
                oc_ref, os_ref, ow_ref, gl_ref, wa_ref, wb_ref, wc_ref, g_ref, b_ref, o_ref):
    mix = _dot(oa_ref[...], wa_ref[...])

    l0, l1, l2 = l0_ref[...], l1_ref[...], l2_ref[...]
    mx = jnp.maximum(jnp.maximum(l0, l1), l2)
    e0, e1, e2 = jnp.exp(l0 - mx), jnp.exp(l1 - mx), jnp.exp(l2 - mx)
    den = e0 + e1 + e2
    ob = (e0 / den) * d0_ref[...] + (e1 / den) * d1_ref[...] + (e2 / den) * d2_ref[...]
    mix = mix + _dot(ob.astype(BF16), wb_ref[...])

    gl = gl_ref[...]
    for h in range(NSA_HEADS):
        sl = slice(h * LANES, (h + 1) * LANES)
        oc = (gl[:, 3 * h:3 * h + 1] * oc_ref[:, sl].astype(F32)
              + gl[:, 3 * h + 1:3 * h + 2] * os_ref[:, sl].astype(F32)
              + gl[:, 3 * h + 2:3 * h + 3] * ow_ref[:, sl].astype(F32))
        mix = mix + _dot(oc.astype(BF16), wc_ref[sl, :])

    y = ALPHA * x_ref[...] + mix
    o_ref[...] = _layer_norm(y, g_ref[...], b_ref[...])


def _out_weight_layout(w_out):
    na = MLA_HEADS * MLA_V
    nb = DIL_HEADS * HEAD_DIM
    src = -np.ones(NSA_HEADS * LANES, np.int64)
    for h in range(NSA_HEADS):
        d0 = h * LANES + HEAD_DIM * (h // NSA_HEADS_PER_GROUP)
        src[d0:d0 + HEAD_DIM] = na + nb + h * HEAD_DIM + np.arange(HEAD_DIM)
    return (w_out[:na].astype(BF16), w_out[na:na + nb].astype(BF16),
            _gather_columns(w_out, src, 0).astype(BF16))


def _output_projection(x, oa, dil, oc, osl, ow, gl, wa, wb, wc, gain, bias, *, tm=512):
    T = x.shape[0]
    row = lambda a: pl.BlockSpec((tm, a.shape[1]), lambda i: (i, 0))
    full = lambda a: pl.BlockSpec(a.shape, lambda i: (0, 0))
    g2, b2 = gain.reshape(1, -1), bias.reshape(1, -1)
    (d0, l0), (d1, l1), (d2, l2) = dil
    rows = [x, oa, d0, d1, d2, l0, l1, l2, oc, osl, ow, gl]
    consts = [wa, wb, wc, g2, b2]
    return pl.pallas_call(
        _out_kernel,
        grid=(T // tm,),
        in_specs=[row(a) for a in rows] + [full(a) for a in consts],
        out_specs=pl.BlockSpec((tm, D_MODEL), lambda i: (i, 0)),
        out_shape=jax.ShapeDtypeStruct((T, D_MODEL), F32),
        compiler_params=_cparams("parallel"),
        name="out_proj_ln",
    )(*rows, *consts)


def kernel(x, ffn_w_in, ffn_w_out, ln_gain, ln_bias, w_in, w_out, mla_q_norm, mla_kv_norm,
           mla_w_uq, mla_w_ukv, nsa_cmp_pos, nsa_cmp_w1, nsa_cmp_w2):
    B, S, D = x.shape
    assert D == D_MODEL and S % 2048 == 0 and S // NSA_SEL_BLOCK <= SEL_LANES
    T = B * S
    tables = _rope_tables(S)
    overlap, onehot = _selection_constants(S)
    proj_src = _proj_source_columns()
    bs = lambda t: t.reshape(B, S, t.shape[-1])

    xf = x.reshape(T, D)
    for l in range(DEPTH):
        xf = _ffn_ln(xf, ffn_w_in[l, 0].astype(BF16), ffn_w_out[l, 0].astype(BF16),
                     ln_gain[l, 0], ln_bias[l, 0])

        w_big = _gather_columns(w_in[l], proj_src, 1).astype(BF16)
        wuq, wkn, wv = _mla_weight_layout(mla_w_uq[l], mla_w_ukv[l])
        (qa, ka, va, dq, dk, dv, nq, nqr, kc, vc, ksl, vsl, kw, vw, gl) = _input_projection(
            xf, w_big, mla_q_norm[l], mla_kv_norm[l], wuq, wkn, wv, tables, S)

        oa = _mla_attention(bs(qa), bs(ka), bs(va)).reshape(T, -1)
        dil = [_dilated_pattern(bs(dq), bs(dk), bs(dv), window, r) for window, r in DIL_PATTERNS]
        kcc, vcc = _nsa_compress(bs(kc), bs(vc), nsa_cmp_pos[l], nsa_cmp_w1[l], nsa_cmp_w2[l])
        oc, sel_bias = _nsa_compressed_branch(bs(nq), kcc, vcc, overlap)
        osl = _nsa_selected_branch(bs(nqr), sel_bias, bs(ksl), bs(vsl), onehot)
        ow = _nsa_window_branch(bs(nqr), bs(kw), bs(vw))

        wa, wb, wc = _out_weight_layout(w_out[l])
        xf = _output_projection(xf, oa, dil, oc.reshape(T, -1), osl.reshape(T, -1), ow.reshape(T, -1),
                                gl, wa, wb, wc, ln_gain[l, 1], ln_bias[l, 1])

        xf = _ffn_ln(xf, ffn_w_in[l, 1].astype(BF16), ffn_w_out[l, 1].astype(BF16),
                     ln_gain[l, 2], ln_bias[l, 2])
    return xf.reshape(B, S, D)
```

```python
import functools

import numpy as np
import jax
import jax.numpy as jnp
from jax import lax
from jax.experimental import pallas as pl
from jax.experimental.pallas import tpu as pltpu

F32 = jnp.float32
BF16 = jnp.bfloat16

D_MODEL = 1024
DEPTH = 2
HEAD_DIM = 64
MLA_HEADS = 4
MLA_Q_LORA = 256
MLA_KV_LORA = 128
MLA_NOPE = 64
MLA_ROPE = 32
MLA_V = 64
DIL_HEADS = 6
DIL_PATTERNS = ((128, 1), (512, 4), (2048, 16))
NSA_HEADS = 6
NSA_KV_GROUPS = 2
NSA_HEADS_PER_GROUP = NSA_HEADS // NSA_KV_GROUPS
NSA_CMP_BLOCK = 32
NSA_CMP_STRIDE = 16
NSA_CMP_HIDDEN = 256
NSA_SEL_BLOCK = 64
NSA_TOP_N = 16
NSA_WINDOW = 512
D_FF = ((8 * D_MODEL // 3 + 255) // 256) * 256
ROPE_THETA = 10000.0
LN_EPS = 1e-5
RMS_EPS = 1e-6
NEG_INF = -1e30
FORCE_SCORE = 1e4
ALPHA = (2 * DEPTH) ** 0.25
LOG2_E = 1.4426950408889634

LANES = 128
SEL_LANES = 128
UNSELECTED_BIAS = -1e9
VMEM_LIMIT = 56 * 1024 * 1024
KV_TILE = 512

_SRC_CQ, _SRC_CKV, _SRC_KPE = 0, 256, 384
_SRC_DQ, _SRC_DK, _SRC_DV = 416, 800, 1184
_SRC_NQ = 1568
_SRC_KC, _SRC_VC, _SRC_KSL, _SRC_VSL, _SRC_KW, _SRC_VW = 1952, 2080, 2208, 2336, 2464, 2592
_SRC_GL = 2720
_C_CQ, _C_CKV, _C_KPE, _C_DQ, _C_DK, _C_DV, _C_NQ = 0, 256, 384, 896, 1664, 2048, 2432
_C_KC, _C_VC, _C_KSL, _C_VSL, _C_KW, _C_VW, _C_GL = 3200, 3328, 3456, 3584, 3712, 3840, 3968
_PROJ_WIDTH = 4096


def _cparams(*sem):
    return pltpu.CompilerParams(dimension_semantics=sem, vmem_limit_bytes=VMEM_LIMIT)


def _layer_norm(y, g, b):
    mu = jnp.mean(y, axis=-1, keepdims=True)
    d = y - mu
    var = jnp.mean(d * d, axis=-1, keepdims=True)
    return d * lax.rsqrt(var + LN_EPS) * g + b


def _dot(a, b):
    return jnp.dot(a, b, preferred_element_type=F32)


def _dot_nt(a, b):
    return lax.dot_general(a, b, (((1,), (1,)), ((), ())), preferred_element_type=F32)


def _lane_half(shape):
    return lax.shift_right_logical(lax.broadcasted_iota(jnp.int32, shape, len(shape) - 1), 6)


def _ffn_kernel(x_ref, wg_ref, wu_ref, wo_ref, g_ref, b_ref, o_ref, xb_ref, acc_ref):
    j = pl.program_id(1)

    @pl.when(j == 0)
    def _():
        xb_ref[...] = x_ref[...].astype(BF16)
        acc_ref[...] = jnp.zeros_like(acc_ref)

    xb = xb_ref[...]
    gate = _dot(xb, wg_ref[...])
    up = _dot(xb, wu_ref[...])
    h = (gate * jax.nn.sigmoid(gate) * up).astype(BF16)
    acc_ref[...] += _dot(h, wo_ref[...])

    @pl.when(j == pl.num_programs(1) - 1)
    def _():
        y = ALPHA * x_ref[...] + 0.5 * acc_ref[...]
        o_ref[...] = _layer_norm(y, g_ref[...], b_ref[...])


def _ffn_ln(x, w_in, w_out, gain, bias, *, tm=1024, tf=256):
    T = x.shape[0]
    nf = D_FF // tf
    return pl.pallas_call(
        _ffn_kernel,
        grid=(T // tm, nf),
        in_specs=[
            pl.BlockSpec((tm, D_MODEL), lambda i, j: (i, 0)),
            pl.BlockSpec((D_MODEL, tf), lambda i, j: (0, j)),
            pl.BlockSpec((D_MODEL, tf), lambda i, j: (0, j + nf)),
            pl.BlockSpec((tf, D_MODEL), lambda i, j: (j, 0)),
            pl.BlockSpec((1, D_MODEL), lambda i, j: (0, 0)),
            pl.BlockSpec((1, D_MODEL), lambda i, j: (0, 0)),
        ],
        out_specs=pl.BlockSpec((tm, D_MODEL), lambda i, j: (i, 0)),
        out_shape=jax.ShapeDtypeStruct((T, D_MODEL), F32),
        scratch_shapes=[pltpu.VMEM((tm, D_MODEL), BF16), pltpu.VMEM((tm, D_MODEL), F32)],
        compiler_params=_cparams("parallel", "arbitrary"),
        name="ffn_ln",
    )(x, w_in, w_in, w_out, gain.reshape(1, -1), bias.reshape(1, -1))


def _rope_chunk(x, c, sa, sb, half):
    return x * c + pltpu.roll(x, LANES - half, 1) * sa + pltpu.roll(x, half, 1) * sb


def _rms_norm(x, g):
    return x * lax.rsqrt(jnp.mean(x * x, axis=-1, keepdims=True) + RMS_EPS) * g


def _proj_kernel(x_ref, w_ref, qn_ref, kvn_ref, wuq_ref, wkn_ref, wv_ref,
                 c64_ref, sa64_ref, sb64_ref, cm_ref, sam_ref, sbm_ref,
                 qa_ref, ka_ref, va_ref, dq_ref, dk_ref, dv_ref, nq_ref, nqr_ref,
                 kc_ref, vc_ref, ksl_ref, vsl_ref, kw_ref, vw_ref, gl_ref):
    xb = x_ref[...].astype(BF16)
    c64, sa64, sb64 = c64_ref[...], sa64_ref[...], sb64_ref[...]
    cm, sam, sbm = cm_ref[...], sam_ref[...], sbm_ref[...]

    def proj(off, width):
        return _dot(xb, w_ref[:, off:off + width])

    def rope64(v):
        return _rope_chunk(v, c64, sa64, sb64, HEAD_DIM // 2)

    def rope_mla(v):
        return _rope_chunk(v, cm, sam, sbm, MLA_ROPE // 2)

    cq = _rms_norm(proj(_C_CQ, MLA_Q_LORA), qn_ref[...]).astype(BF16)
    q_raw = _dot(cq, wuq_ref[...])
    ckv = _rms_norm(proj(_C_CKV, MLA_KV_LORA), kvn_ref[...]).astype(BF16)
    k_nope = _dot(ckv, wkn_ref[...])
    va_ref[0, 0] = _dot(ckv, wv_ref[...]).T.astype(BF16)
    kpe = proj(_C_KPE, MLA_HEADS * LANES)
    q_scale = (MLA_NOPE + MLA_ROPE) ** -0.5 * LOG2_E
    for h in range(MLA_HEADS):
        sl = slice(h * LANES, (h + 1) * LANES)
        qa_ref[:, sl] = (rope_mla(q_raw[:, sl]) * q_scale).astype(BF16)
        ka_ref[:, sl] = (k_nope[:, sl] + rope_mla(kpe[:, sl])).astype(BF16)

    scale = HEAD_DIM ** -0.5
    dq = proj(_C_DQ, DIL_HEADS * LANES)
    for h in range(DIL_HEADS):
        sl = slice(h * LANES, (h + 1) * LANES)
        dq_ref[:, sl] = (rope64(dq[:, sl]) * scale).astype(BF16)
    dk = proj(_C_DK, DIL_HEADS * HEAD_DIM)
    for c in range(DIL_HEADS * HEAD_DIM // LANES):
        sl = slice(c * LANES, (c + 1) * LANES)
        dk_ref[:, sl] = rope64(dk[:, sl]).astype(BF16)
    dv_ref[...] = proj(_C_DV, DIL_HEADS * HEAD_DIM).astype(BF16)

    nq = proj(_C_NQ, NSA_HEADS * LANES)
    nq_ref[...] = (nq * scale).astype(BF16)
    for h in range(NSA_HEADS):
        sl = slice(h * LANES, (h + 1) * LANES)
        nqr_ref[:, sl] = (rope64(nq[:, sl]) * (scale * LOG2_E)).astype(BF16)
    kc_ref[...] = proj(_C_KC, LANES)
    vc_ref[...] = proj(_C_VC, LANES)
    ksl_ref[...] = rope64(proj(_C_KSL, LANES)).astype(BF16)
    vsl_ref[0, 0] = proj(_C_VSL, LANES).T.astype(BF16)
    kw_ref[...] = rope64(proj(_C_KW, LANES)).astype(BF16)
    vw_ref[...] = proj(_C_VW, LANES).astype(BF16)
    gl_ref[...] = jax.nn.sigmoid(proj(_C_GL, LANES))


def _proj_source_columns():
    src = -np.ones(_PROJ_WIDTH, np.int64)

    def put(dst, s0, n):
        src[dst:dst + n] = s0 + np.arange(n)

    put(_C_CQ, _SRC_CQ, MLA_Q_LORA)
    put(_C_CKV, _SRC_CKV, MLA_KV_LORA)
    for h in range(MLA_HEADS):
        put(_C_KPE + h * LANES + MLA_NOPE, _SRC_KPE, MLA_ROPE)
    for h in range(DIL_HEADS):
        put(_C_DQ + h * LANES + HEAD_DIM * (h % 2), _SRC_DQ + h * HEAD_DIM, HEAD_DIM)
    put(_C_DK, _SRC_DK, DIL_HEADS * HEAD_DIM)
    put(_C_DV, _SRC_DV, DIL_HEADS * HEAD_DIM)
    for h in range(NSA_HEADS):
        put(_C_NQ + h * LANES + HEAD_DIM * (h // NSA_HEADS_PER_GROUP), _SRC_NQ + h * HEAD_DIM, HEAD_DIM)
    for dst, s0 in ((_C_KC, _SRC_KC), (_C_VC, _SRC_VC), (_C_KSL, _SRC_KSL), (_C_VSL, _SRC_VSL),
                    (_C_KW, _SRC_KW), (_C_VW, _SRC_VW)):
        put(dst, s0, NSA_KV_GROUPS * HEAD_DIM)
    put(_C_GL, _SRC_GL, NSA_HEADS * 3)
    return src


def _gather_columns(w, src, axis):
    valid = jnp.asarray(src >= 0)
    taken = jnp.take(w, jnp.asarray(np.maximum(src, 0)), axis=axis)
    shape = [1, 1]
    shape[axis] = -1
    return jnp.where(valid.reshape(shape), taken, 0.0)


def _mla_weight_layout(w_uq, w_ukv):
    dq = MLA_NOPE + MLA_ROPE
    src_q = -np.ones(MLA_HEADS * LANES, np.int64)
    src_kn = -np.ones(MLA_HEADS * LANES, np.int64)
    src_v = np.zeros(MLA_HEADS * MLA_V, np.int64)
    for h in range(MLA_HEADS):
        src_q[h * LANES:h * LANES + dq] = h * dq + np.arange(dq)
        src_kn[h * LANES:h * LANES + MLA_NOPE] = h * (MLA_NOPE + MLA_V) + np.arange(MLA_NOPE)
        src_v[h * MLA_V:(h + 1) * MLA_V] = h * (MLA_NOPE + MLA_V) + MLA_NOPE + np.arange(MLA_V)
    return (_gather_columns(w_uq, src_q, 1).astype(BF16),
            _gather_columns(w_ukv, src_kn, 1).astype(BF16),
            _gather_columns(w_ukv, src_v, 1).astype(BF16))


def _rope_tables(S):
    def cos_sin(dim):
        inv_freq = ROPE_THETA ** (-jnp.arange(0, dim, 2, dtype=F32) / dim)
        ang = jnp.arange(S, dtype=F32)[:, None] * inv_freq[None, :]
        return jnp.cos(ang), jnp.sin(ang)

    cos, sin = cos_sin(HEAD_DIM)
    zero = jnp.zeros_like(sin)
    c64 = jnp.concatenate([cos, cos] * 2, axis=1)
    sa64 = jnp.concatenate([-sin, zero] * 2, axis=1)
    sb64 = jnp.concatenate([zero, sin] * 2, axis=1)
    cos_m, sin_m = cos_sin(MLA_ROPE)
    ones = jnp.ones((S, MLA_NOPE), F32)
    z64 = jnp.zeros((S, MLA_NOPE), F32)
    z16 = jnp.zeros_like(sin_m)
    z32 = jnp.zeros((S, LANES - MLA_NOPE - MLA_ROPE), F32)
    cm = jnp.concatenate([ones, cos_m, cos_m, z32], axis=1)
    sam = jnp.concatenate([z64, -sin_m, z16, z32], axis=1)
    sbm = jnp.concatenate([z64, z16, sin_m, z32], axis=1)
    return c64, sa64, sb64, cm, sam, sbm


def _input_projection(h, w_big, q_norm, kv_norm, wuq, wkn, wv, tables, S, *, tm=KV_TILE):
    T = h.shape[0]
    n_pos = S // tm
    row = lambda w: pl.BlockSpec((tm, w), lambda i: (i, 0))
    full = lambda a: pl.BlockSpec(a.shape, lambda i: (0, 0))
    tab = pl.BlockSpec((tm, LANES), lambda i: (i % n_pos, 0))
    tile_t = lambda c: pl.BlockSpec((1, 1, c, tm), lambda i: (i // n_pos, i % n_pos, 0, 0))
    outs = [(MLA_HEADS * LANES, BF16), (MLA_HEADS * LANES, BF16), (None, MLA_HEADS * MLA_V),
            (DIL_HEADS * LANES, BF16), (DIL_HEADS * HEAD_DIM, BF16), (DIL_HEADS * HEAD_DIM, BF16),
            (NSA_HEADS * LANES, BF16), (NSA_HEADS * LANES, BF16),
            (LANES, F32), (LANES, F32), (LANES, BF16), (None, LANES), (LANES, BF16), (LANES, BF16),
            (LANES, F32)]
    qn = q_norm.reshape(1, -1)
    kvn = kv_norm.reshape(1, -1)
    return pl.pallas_call(
        _proj_kernel,
        grid=(T // tm,),
        in_specs=[row(D_MODEL), full(w_big), full(qn), full(kvn), full(wuq), full(wkn), full(wv)]
                 + [tab] * 6,
        out_specs=[tile_t(dt) if w is None else row(w) for w, dt in outs],
        out_shape=[jax.ShapeDtypeStruct((T // S, n_pos, dt, tm), BF16) if w is None
                   else jax.ShapeDtypeStruct((T, w), dt) for w, dt in outs],
        compiler_params=_cparams("parallel"),
        name="in_proj",
    )(h, w_big, qn, kvn, wuq, wkn, wv, *tables)


ONES_ROWS = 16


def _flash_step_t(s, vt1, m_ref, acc_ref, idx):
    m_prev = m_ref[idx]
    m_new = jnp.maximum(m_prev, jnp.max(s, axis=0, keepdims=True))
    alpha = jnp.exp2(m_prev - m_new)
    p = jnp.exp2(s - m_new).astype(BF16)
    acc_ref[idx] = alpha * acc_ref[idx] + _dot(vt1, p)
    m_ref[idx] = m_new


def _init_softmax_state(m_ref, acc_ref):
    m_ref[...] = jnp.full(m_ref.shape, NEG_INF, F32)
    acc_ref[...] = jnp.zeros_like(acc_ref)


def _flash_state(slots, channels, queries):
    return [pltpu.VMEM((slots, 1, queries), F32), pltpu.VMEM((slots, channels + ONES_ROWS, queries), F32)]


def _with_ones_rows(vt):
    return jnp.concatenate([vt, jnp.ones((ONES_ROWS, vt.shape[1]), vt.dtype)], axis=0)


def _normalised(acc, channels):
    return acc[:channels] / acc[channels:channels + 1]


def _mla_kernel(q_ref, k_ref, vt_ref, o_ref, m_ref, acc_ref, *, t):
    qi = pl.program_id(1)
    _init_softmax_state(m_ref, acc_ref)
    causal = lax.broadcasted_iota(jnp.int32, (t, t), 0) <= lax.broadcasted_iota(jnp.int32, (t, t), 1)

    def tile(kt, masked):
        start = pl.multiple_of(kt * t, t)

        def scores(h):
            hsl = slice(h * LANES, (h + 1) * LANES)
            s = _dot_nt(k_ref[0, pl.ds(start, t), hsl], q_ref[0, :, hsl])
            return jnp.where(causal, s, NEG_INF) if masked else s

        s_next = scores(0)
        for h in range(MLA_HEADS):
            s = s_next
            if h + 1 < MLA_HEADS:
                s_next = scores(h + 1)
            vt = vt_ref[0, kt, (h // 2) * LANES:(h // 2 + 1) * LANES, :]
            _flash_step_t(s, _with_ones_rows(vt), m_ref, acc_ref, h)

    def body(kt, carry):
        tile(kt, False)
        return carry

    lax.fori_loop(0, qi, body, 0)
    tile(qi, True)
    even_rows = lax.broadcasted_iota(jnp.int32, (LANES, t), 0) < MLA_V
    for pair in range(MLA_HEADS // 2):
        h0, h1 = 2 * pair, 2 * pair + 1
        out_t = jnp.where(even_rows, _normalised(acc_ref[h0], LANES), _normalised(acc_ref[h1], LANES))
        o_ref[0, :, pair * LANES:(pair + 1) * LANES] = out_t.T.astype(o_ref.dtype)


def _mla_attention(q, k, vt, *, t=KV_TILE):
    B, S, _ = q.shape
    return pl.pallas_call(
        functools.partial(_mla_kernel, t=t),
        grid=(B, S // t),
        in_specs=[
            pl.BlockSpec((1, t, MLA_HEADS * LANES), lambda b, i: (b, i, 0)),
            pl.BlockSpec((1, S, MLA_HEADS * LANES), lambda b, i: (b, 0, 0)),
            pl.BlockSpec((1, S // t, MLA_HEADS * MLA_V, t), lambda b, i: (b, 0, 0, 0)),
        ],
        out_specs=pl.BlockSpec((1, t, MLA_HEADS * MLA_V), lambda b, i: (b, i, 0)),
        out_shape=jax.ShapeDtypeStruct((B, S, MLA_HEADS * MLA_V), BF16),
        scratch_shapes=_flash_state(MLA_HEADS, LANES, t),
        compiler_params=_cparams("parallel", "arbitrary"),
        name="mla_attn",
    )(q, k, vt)


def _band_mask(q0, kstart, tq, span, window):
    qpos = q0 + lax.broadcasted_iota(jnp.int32, (tq, span), 0)
    kpos = kstart + lax.broadcasted_iota(jnp.int32, (tq, span), 1)
    dist = qpos - kpos
    return (dist >= 0) & (dist <= window)


def _dil_kernel(q_ref, k_ref, v_ref, o_ref, lse_ref, *, tq, wpad, window):
    q0 = pl.program_id(2) * tq
    n = k_ref.shape[1]
    span = min(tq + wpad, n)
    kstart = pl.multiple_of(jnp.clip(q0 - wpad, 0, n - span), LANES)
    mask = _band_mask(q0, kstart, tq, span, window)
    half_v = _lane_half((span, LANES))
    half_o = _lane_half((tq, LANES))
    for c in range(DIL_HEADS // 2):
        csl = slice(c * LANES, (c + 1) * LANES)
        k = k_ref[0, pl.ds(kstart, span), csl]
        v = v_ref[0, pl.ds(kstart, span), csl]
        out = jnp.zeros((tq, LANES), F32)
        lse = jnp.zeros((tq, LANES), F32)
        for hh in range(2):
            h = 2 * c + hh
            q = q_ref[0, :, h * LANES:(h + 1) * LANES]
            s = jnp.where(mask, _dot_nt(q, k), NEG_INF)
            m = jnp.max(s, axis=1, keepdims=True)
            e = jnp.exp(s - m)
            l = jnp.sum(e, axis=1, keepdims=True)
            vm = jnp.where(half_v == hh, v, jnp.zeros_like(v))
            out = out + _dot(e.astype(BF16), vm) / l
            lse = lse + jnp.where(half_o == hh, m + jnp.log(l), 0.0)
        o_ref[0, :, csl] = out
        lse_ref[0, :, csl] = lse


def _dilated_pattern(q, k, v, window, dil, *, tq=128):
    B, S, _ = q.shape
    n = S // dil
    wsub = window // dil
    wpad = -(-wsub // LANES) * LANES
    wq, wk = DIL_HEADS * LANES, DIL_HEADS * HEAD_DIM
    qv = q.reshape(B, n, dil * wq)
    kv = k.reshape(B, n, dil * wk)
    vv = v.reshape(B, n, dil * wk)
    o, lse = pl.pallas_call(
        functools.partial(_dil_kernel, tq=tq, wpad=wpad, window=wsub),
        grid=(B, dil, n // tq),
        in_specs=[
            pl.BlockSpec((1, tq, wq), lambda b, j, i: (b, i, j)),
            pl.BlockSpec((1, n, wk), lambda b, j, i: (b, 0, j)),
            pl.BlockSpec((1, n, wk), lambda b, j, i: (b, 0, j)),
        ],
        out_specs=[pl.BlockSpec((1, tq, wk), lambda b, j, i: (b, i, j))] * 2,
        out_shape=[jax.ShapeDtypeStruct((B, n, dil * wk), F32)] * 2,
        compiler_params=_cparams("parallel", "parallel", "arbitrary"),
        name=f"dilated_r{dil}",
    )(qv, kv, vv)
    return o.reshape(B * S, wk), lse.reshape(B * S, wk)


def _compress_kernel(xk_ref, xv_ref, plo_ref, phi_ref, w1_ref, w2_ref, kc_ref, vc_ref):
    n = xk_ref.shape[2]
    half_rows = w1_ref.shape[1] // 2
    for t, (x_ref, o_ref) in enumerate(((xk_ref, kc_ref), (xv_ref, vc_ref))):
        x = x_ref[0, 0]
        a = _dot((x + plo_ref[t]).astype(BF16), w1_ref[t, :half_rows, :])
        b = _dot((x + phi_ref[t]).astype(BF16), w1_ref[t, half_rows:, :])
        hid = a + pltpu.roll(b, n - 1, 0)
        hid = (hid * jax.nn.sigmoid(hid)).astype(BF16)
        o_ref[0, 0] = _dot(hid, w2_ref[t])


def _nsa_compress(kc, vc, cmp_pos, cmp_w1, cmp_w2):
    B, S, _ = kc.shape
    G, d, st = NSA_KV_GROUPS, HEAD_DIM, NSA_CMP_STRIDE
    n = S // st

    def rows(t):
        return t.reshape(B, n, st, G, d).transpose(0, 3, 1, 2, 4).reshape(B, G, n, st * d)

    pos = cmp_pos.reshape(2, NSA_CMP_BLOCK // st, 1, st * d)
    xspec = pl.BlockSpec((1, 1, n, st * d), lambda b, g: (b, g, 0, 0))
    ospec = pl.BlockSpec((1, 1, n, d), lambda b, g: (b, g, 0, 0))
    full = lambda a: pl.BlockSpec(a.shape, lambda b, g: (0,) * a.ndim)
    plo, phi = pos[:, 0], pos[:, 1]
    w1 = cmp_w1.astype(BF16)
    w2 = cmp_w2.astype(BF16)
    kcc, vcc = pl.pallas_call(
        _compress_kernel,
        grid=(B, G),
        in_specs=[xspec, xspec, full(plo), full(phi), full(w1), full(w2)],
        out_specs=[ospec, ospec],
        out_shape=[jax.ShapeDtypeStruct((B, G, n, d), F32)] * 2,
        compiler_params=_cparams("parallel", "parallel"),
        name="nsa_compress",
    )(rows(kc), rows(vc), plo, phi, w1, w2)
    pair = lambda t: t.transpose(0, 2, 1, 3).reshape(B, n, G * d).astype(BF16)
    return pair(kcc), pair(vcc)


def _dot_01_by_f32(o01, p):
    p1 = p.astype(BF16)
    r1 = p - p1.astype(F32)
    p2 = r1.astype(BF16)
    p3 = (r1 - p2.astype(F32)).astype(BF16)
    return _dot(o01, p1) + _dot(o01, p2) + _dot(o01, p3)


SUBLANES = 8


def _count_outranking(sc_ref, cnt_ref, n_sources, tq):
    groups = SEL_LANES // SUBLANES
    sub = lax.broadcasted_iota(jnp.int32, (SUBLANES, tq), 0)
    cnt_ref[...] = jnp.zeros_like(cnt_ref)
    for bi in range(groups):
        @pl.when(bi * SUBLANES < n_sources)
        def _():
            src = sc_ref[bi * SUBLANES:(bi + 1) * SUBLANES, :]
            rows = [jnp.broadcast_to(src[ii:ii + 1, :], (SUBLANES, tq)) for ii in range(SUBLANES)]
            for r in range(groups):
                rsl = slice(r * SUBLANES, (r + 1) * SUBLANES)
                tgt = sc_ref[rsl, :]
                cnt = cnt_ref[rsl, :]
                for ii, row in enumerate(rows):
                    if r > bi:
                        ahead = row >= tgt
                    elif r < bi:
                        ahead = row > tgt
                    else:
                        ahead = (row > tgt) | ((row == tgt) & (sub > ii))
                    cnt = cnt + jnp.where(ahead, 1.0, 0.0)
                cnt_ref[rsl, :] = cnt


def _cmp_kernel(q_ref, kc_ref, vct_ref, ovt_ref, o_ref, bias_ref, sc_ref, cnt_ref, *, tq):
    g = pl.program_id(1)
    q0 = pl.program_id(2) * tq
    n = kc_ref.shape[1]
    kc = kc_ref[0]
    vct = vct_ref[0]
    qpos = q0 + lax.broadcasted_iota(jnp.int32, (n, tq), 1)
    cmp_end = lax.broadcasted_iota(jnp.int32, (n, tq), 0) * NSA_CMP_STRIDE + (NSA_CMP_BLOCK - 1)
    visible = cmp_end <= qpos
    in_group = _lane_half((tq, LANES)) == g
    psum = jnp.zeros((n, tq), F32)

    def scores(hh):
        return jnp.where(visible, _dot_nt(kc, q_ref[0, :, hh * LANES:(hh + 1) * LANES]), NEG_INF)

    s_next = scores(0)
    for hh in range(NSA_HEADS_PER_GROUP):
        s = s_next
        if hh + 1 < NSA_HEADS_PER_GROUP:
            s_next = scores(hh + 1)
        m = jnp.max(s, axis=0, keepdims=True)
        e = jnp.where(visible, jnp.exp(s - m), 0.0)
        l = jnp.maximum(jnp.sum(e, axis=0, keepdims=True), 1e-30)
        p = e * (1.0 / l)
        o = _dot(vct, p.astype(BF16)).T
        o_ref[0, :, hh * LANES:(hh + 1) * LANES] = jnp.where(in_group, o, 0.0).astype(o_ref.dtype)
        psum = psum + p

    imp_t = _dot_01_by_f32(ovt_ref[...], psum)
    blk = lax.broadcasted_iota(jnp.int32, (SEL_LANES, tq), 0)
    cur = lax.shift_right_logical(q0 + lax.broadcasted_iota(jnp.int32, (SEL_LANES, tq), 1), 6)
    forced = (blk == 0) | (blk == cur) | (blk == cur - 1)
    sc_ref[...] = jnp.where(forced, FORCE_SCORE, jnp.where(blk <= cur, imp_t, -FORCE_SCORE))
    n_causal = jnp.minimum(lax.shift_right_logical(q0 + tq - 1, 6) + 1, SEL_LANES)
    _count_outranking(sc_ref, cnt_ref, n_causal, tq)
    bias_t = jnp.where(cnt_ref[...] < float(NSA_TOP_N), 0.0, UNSELECTED_BIAS)
    bias_ref[0, 0] = bias_t.T.astype(bias_ref.dtype)


def _nsa_compressed_branch(nq, kcc, vcc, overlap, *, tq=256):
    B, S, _ = nq.shape
    G = NSA_KV_GROUPS
    n = kcc.shape[1]
    wq = NSA_HEADS_PER_GROUP * LANES
    return pl.pallas_call(
        functools.partial(_cmp_kernel, tq=tq),
        grid=(B, G, S // tq),
        in_specs=[
            pl.BlockSpec((1, tq, wq), lambda b, g, i: (b, i, g)),
            pl.BlockSpec((1, n, LANES), lambda b, g, i: (b, 0, 0)),
            pl.BlockSpec((1, LANES, n), lambda b, g, i: (b, 0, 0)),
            pl.BlockSpec((SEL_LANES, n), lambda b, g, i: (0, 0)),
        ],
        out_specs=[
            pl.BlockSpec((1, tq, wq), lambda b, g, i: (b, i, g)),
            pl.BlockSpec((1, 1, tq, SEL_LANES), lambda b, g, i: (b, g, i, 0)),
        ],
        out_shape=[jax.ShapeDtypeStruct((B, S, NSA_HEADS * LANES), BF16),
                   jax.ShapeDtypeStruct((B, G, S, SEL_LANES), BF16)],
        scratch_shapes=[pltpu.VMEM((SEL_LANES, tq), F32), pltpu.VMEM((SEL_LANES, tq), F32)],
        compiler_params=_cparams("parallel", "parallel", "arbitrary"),
        name="nsa_cmp_topk",
    )(nq, kcc, jnp.swapaxes(vcc, 1, 2), overlap.T)


def _sel_kernel(q_ref, bias_ref, k_ref, vt_ref, oh_ref, o_ref, qa_ref, m_ref, acc_ref, *, tq, tk):
    q0 = pl.program_id(1) * tq
    for h in range(NSA_HEADS):
        qa_ref[h, :, :LANES] = q_ref[0, :, h * LANES:(h + 1) * LANES]
        qa_ref[h, :, LANES:] = bias_ref[0, h // NSA_HEADS_PER_GROUP]
    kofs = lax.broadcasted_iota(jnp.int32, (tk, tq), 0)
    qpos = q0 + lax.broadcasted_iota(jnp.int32, (tk, tq), 1)
    _init_softmax_state(m_ref, acc_ref)

    def tile(kt, masked):
        start = pl.multiple_of(kt * tk, tk)
        ka = jnp.concatenate([k_ref[0, pl.ds(start, tk), :], oh_ref[pl.ds(start, tk), :]], axis=1)
        vt1 = _with_ones_rows(vt_ref[0, kt])

        def scores(h):
            s = _dot_nt(ka, qa_ref[h])
            return jnp.where((start + kofs) <= qpos, s, NEG_INF) if masked else s

        s_next = scores(0)
        for h in range(NSA_HEADS):
            s = s_next
            if h + 1 < NSA_HEADS:
                s_next = scores(h + 1)
            _flash_step_t(s, vt1, m_ref, acc_ref, h)

    def body(kt, carry):
        tile(kt, False)
        return carry

    n_full = q0 // tk
    lax.fori_loop(0, n_full, body, 0)
    tile(n_full, True)
    half = _lane_half((tq, LANES))
    for h in range(NSA_HEADS):
        out = _normalised(acc_ref[h], LANES).T
        o_ref[0, :, h * LANES:(h + 1) * LANES] = jnp.where(
            half == h // NSA_HEADS_PER_GROUP, out, 0.0).astype(o_ref.dtype)


def _nsa_selected_branch(nqr, bias, ksl, vslt, onehot, *, tq=256, tk=KV_TILE):
    B, S, wq = nqr.shape
    G = NSA_KV_GROUPS
    return pl.pallas_call(
        functools.partial(_sel_kernel, tq=tq, tk=tk),
        grid=(B, S // tq),
        in_specs=[
            pl.BlockSpec((1, tq, wq), lambda b, i: (b, i, 0)),
            pl.BlockSpec((1, G, tq, SEL_LANES), lambda b, i: (b, 0, i, 0)),
            pl.BlockSpec((1, S, LANES), lambda b, i: (b, 0, 0)),
            pl.BlockSpec((1, S // tk, LANES, tk), lambda b, i: (b, 0, 0, 0)),
            pl.BlockSpec((S, SEL_LANES), lambda b, i: (0, 0)),
        ],
        out_specs=pl.BlockSpec((1, tq, wq), lambda b, i: (b, i, 0)),
        out_shape=jax.ShapeDtypeStruct((B, S, wq), BF16),
        scratch_shapes=[pltpu.VMEM((NSA_HEADS, tq, LANES + SEL_LANES), BF16)]
                       + _flash_state(NSA_HEADS, LANES, tq),
        compiler_params=_cparams("parallel", "arbitrary"),
        name="nsa_selected",
    )(nqr, bias, ksl, vslt, onehot)


def _win_kernel(q_ref, k_ref, v_ref, o_ref, *, tq, wpad, window):
    g = pl.program_id(1)
    q0 = pl.program_id(2) * tq
    nh = NSA_HEADS_PER_GROUP
    span = tq + wpad
    kstart = pl.multiple_of(jnp.maximum(q0 - wpad, 0), LANES)
    mask = _band_mask(q0, kstart, tq, span, window)
    k = k_ref[0, pl.ds(kstart, span), :]
    v = v_ref[0, pl.ds(kstart, span), :]
    in_group = _lane_half((tq, LANES)) == g
    for hh in range(nh):
        hsl = slice(hh * LANES, (hh + 1) * LANES)
        s = jnp.where(mask, _dot_nt(q_ref[0, :, hsl], k), NEG_INF)
        m = jnp.max(s, axis=1, keepdims=True)
        e = jnp.exp2(s - m)
        l = jnp.sum(e, axis=1, keepdims=True)
        o = _dot(e.astype(BF16), v) / l
        o_ref[0, :, hsl] = jnp.where(in_group, o, 0.0).astype(o_ref.dtype)


def _nsa_window_branch(nqr, kw, vw, *, tq=128):
    B, S, _ = nqr.shape
    G = NSA_KV_GROUPS
    wq = NSA_HEADS_PER_GROUP * LANES
    wpad = -(-NSA_WINDOW // LANES) * LANES
    return pl.pallas_call(
        functools.partial(_win_kernel, tq=tq, wpad=wpad, window=NSA_WINDOW),
        grid=(B, G, S // tq),
        in_specs=[
            pl.BlockSpec((1, tq, wq), lambda b, g, i: (b, i, g)),
            pl.BlockSpec((1, S, LANES), lambda b, g, i: (b, 0, 0)),
            pl.BlockSpec((1, S, LANES), lambda b, g, i: (b, 0, 0)),
        ],
        out_specs=pl.BlockSpec((1, tq, wq), lambda b, g, i: (b, i, g)),
        out_shape=jax.ShapeDtypeStruct((B, S, NSA_HEADS * LANES), BF16),
        compiler_params=_cparams("parallel", "parallel", "arbitrary"),
        name="nsa_window",
    )(nqr, kw, vw)


def _selection_constants(S):
    n_cmp_rows = S // NSA_CMP_STRIDE
    c = np.arange(n_cmp_rows)
    cmp_start = c * NSA_CMP_STRIDE
    cmp_end = cmp_start + NSA_CMP_BLOCK - 1
    sel_start = np.arange(SEL_LANES) * NSA_SEL_BLOCK
    overlap = ((cmp_start[:, None] < sel_start[None, :] + NSA_SEL_BLOCK) &
               (cmp_end[:, None] >= sel_start[None, :]))
    n_cmp = (S - NSA_CMP_BLOCK) // NSA_CMP_STRIDE + 1
    overlap &= (c < n_cmp)[:, None]
    onehot = (np.arange(S)[:, None] // NSA_SEL_BLOCK) == np.arange(SEL_LANES)[None, :]
    return jnp.asarray(overlap, BF16), jnp.asarray(onehot, BF16)


def _out_kernel(x_ref, oa_ref, d0_ref, d1_ref, d2_ref, l0_ref, l1_ref, l2_ref,
                oc_ref, os_ref, ow_ref, gl_ref, wa_ref, wb_ref, wc_ref, g_ref, b_ref, o_ref):
    mix = _dot(oa_ref[...], wa_ref[...])

    l0, l1, l2 = l0_ref[...], l1_ref[...], l2_ref[...]
    mx = jnp.maximum(jnp.maximum(l0, l1), l2)
    e0, e1, e2 = jnp.exp(l0 - mx), jnp.exp(l1 - mx), jnp.exp(l2 - mx)
    den = e0 + e1 + e2
    ob = (e0 / den) * d0_ref[...] + (e1 / den) * d1_ref[...] + (e2 / den) * d2_ref[...]
    mix = mix + _dot(ob.astype(BF16), wb_ref[...])

    gl = gl_ref[...]
    for h in range(NSA_HEADS):
        sl = slice(h * LANES, (h + 1) * LANES)
        oc = (gl[:, 3 * h:3 * h + 1] * oc_ref[:, sl].astype(F32)
              + gl[:, 3 * h + 1:3 * h + 2] * os_ref[:, sl].astype(F32)
              + gl[:, 3 * h + 2:3 * h + 3] * ow_ref[:, sl].astype(F32))
        mix = mix + _dot(oc.astype(BF16), wc_ref[sl, :])

    y = ALPHA * x_ref[...] + mix
    o_ref[...] = _layer_norm(y, g_ref[...], b_ref[...])


def _out_weight_layout(w_out):
    na = MLA_HEADS * MLA_V
    nb = DIL_HEADS * HEAD_DIM
    src = -np.ones(NSA_HEADS * LANES, np.int64)
    for h in range(NSA_HEADS):
        d0 = h * LANES + HEAD_DIM * (h // NSA_HEADS_PER_GROUP)
        src[d0:d0 + HEAD_DIM] = na + nb + h * HEAD_DIM + np.arange(HEAD_DIM)
    return (w_out[:na].astype(BF16), w_out[na:na + nb].astype(BF16),
            _gather_columns(w_out, src, 0).astype(BF16))


def _output_projection(x, oa, dil, oc, osl, ow, gl, wa, wb, wc, gain, bias, *, tm=512):
    T = x.shape[0]
    row = lambda a: pl.BlockSpec((tm, a.shape[1]), lambda i: (i, 0))
    full = lambda a: pl.BlockSpec(a.shape, lambda i: (0, 0))
    g2, b2 = gain.reshape(1, -1), bias.reshape(1, -1)
    (d0, l0), (d1, l1), (d2, l2) = dil
    rows = [x, oa, d0, d1, d2, l0, l1, l2, oc, osl, ow, gl]
    consts = [wa, wb, wc, g2, b2]
    return pl.pallas_call(
        _out_kernel,
        grid=(T // tm,),
        in_specs=[row(a) for a in rows] + [full(a) for a in consts],
        out_specs=pl.BlockSpec((tm, D_MODEL), lambda i: (i, 0)),
        out_shape=jax.ShapeDtypeStruct((T, D_MODEL), F32),
        compiler_params=_cparams("parallel"),
        name="out_proj_ln",
    )(*rows, *consts)


def kernel(x, ffn_w_in, ffn_w_out, ln_gain, ln_bias, w_in, w_out, mla_q_norm, mla_kv_norm,
           mla_w_uq, mla_w_ukv, nsa_cmp_pos, nsa_cmp_w1, nsa_cmp_w2):
    B, S, D = x.shape
    assert D == D_MODEL and S % 2048 == 0 and S // NSA_SEL_BLOCK <= SEL_LANES
    T = B * S
    tables = _rope_tables(S)
    overlap, onehot = _selection_constants(S)
    proj_src = _proj_source_columns()
    bs = lambda t: t.reshape(B, S, t.shape[-1])

    xf = x.reshape(T, D)
    for l in range(DEPTH):
        xf = _ffn_ln(xf, ffn_w_in[l, 0].astype(BF16), ffn_w_out[l, 0].astype(BF16),
                     ln_gain[l, 0], ln_bias[l, 0])

        w_big = _gather_columns(w_in[l], proj_src, 1).astype(BF16)
        wuq, wkn, wv = _mla_weight_layout(mla_w_uq[l], mla_w_ukv[l])
        (qa, ka, va, dq, dk, dv, nq, nqr, kc, vc, ksl, vsl, kw, vw, gl) = _input_projection(
            xf, w_big, mla_q_norm[l], mla_kv_norm[l], wuq, wkn, wv, tables, S)

        oa = _mla_attention(bs(qa), bs(ka), va).reshape(T, -1)
        dil = [_dilated_pattern(bs(dq), bs(dk), bs(dv), window, r) for window, r in DIL_PATTERNS]
        kcc, vcc = _nsa_compress(bs(kc), bs(vc), nsa_cmp_pos[l], nsa_cmp_w1[l], nsa_cmp_w2[l])
        oc, sel_bias = _nsa_compressed_branch(bs(nq), kcc, vcc, overlap)
        osl = _nsa_selected_branch(bs(nqr), sel_bias, bs(ksl), vsl, onehot)
        ow = _nsa_window_branch(bs(nqr), bs(kw), bs(vw))

        wa, wb, wc = _out_weight_layout(w_out[l])
        xf = _output_projection(xf, oa, dil, oc.reshape(T, -1), osl.reshape(T, -1), ow.reshape(T, -1),
                                gl, wa, wb, wc, ln_gain[l, 1], ln_bias[l, 1])

        xf = _ffn_ln(xf, ffn_w_in[l, 1].astype(BF16), ffn_w_out[l, 1].astype(BF16),
                     ln_gain[l, 2], ln_bias[l, 2])
    return xf.reshape(B, S, D)
```

```python
import functools

import numpy as np
import jax
import jax.numpy as jnp
from jax import lax
from jax.experimental import pallas as pl
from jax.experimental.pallas import tpu as pltpu

F32 = jnp.float32
BF16 = jnp.bfloat16

D_MODEL = 1024
DEPTH = 2
HEAD_DIM = 64
MLA_HEADS = 4
MLA_Q_LORA = 256
MLA_KV_LORA = 128
MLA_NOPE = 64
MLA_ROPE = 32
MLA_V = 64
DIL_HEADS = 6
DIL_PATTERNS = ((128, 1), (512, 4), (2048, 16))
NSA_HEADS = 6
NSA_KV_GROUPS = 2
NSA_HEADS_PER_GROUP = NSA_HEADS // NSA_KV_GROUPS
NSA_CMP_BLOCK = 32
NSA_CMP_STRIDE = 16
NSA_CMP_HIDDEN = 256
NSA_SEL_BLOCK = 64
NSA_TOP_N = 16
NSA_WINDOW = 512
D_FF = ((8 * D_MODEL // 3 + 255) // 256) * 256
ROPE_THETA = 10000.0
LN_EPS = 1e-5
RMS_EPS = 1e-6
NEG_INF = -1e30
FORCE_SCORE = 1e4
ALPHA = (2 * DEPTH) ** 0.25
LOG2_E = 1.4426950408889634

LANES = 128
SEL_LANES = 128
UNSELECTED_BIAS = -1e9
VMEM_LIMIT = 56 * 1024 * 1024
KV_TILE = 512

_SRC_CQ, _SRC_CKV, _SRC_KPE = 0, 256, 384
_SRC_DQ, _SRC_DK, _SRC_DV = 416, 800, 1184
_SRC_NQ = 1568
_SRC_KC, _SRC_VC, _SRC_KSL, _SRC_VSL, _SRC_KW, _SRC_VW = 1952, 2080, 2208, 2336, 2464, 2592
_SRC_GL = 2720
_C_CQ, _C_CKV, _C_KPE, _C_DQ, _C_DK, _C_DV, _C_NQ = 0, 256, 384, 896, 1664, 2048, 2432
_C_KC, _C_VC, _C_KSL, _C_VSL, _C_KW, _C_VW, _C_GL = 3200, 3328, 3456, 3584, 3712, 3840, 3968
_PROJ_WIDTH = 4096


def _cparams(*sem):
    return pltpu.CompilerParams(dimension_semantics=sem, vmem_limit_bytes=VMEM_LIMIT)


def _layer_norm(y, g, b):
    mu = jnp.mean(y, axis=-1, keepdims=True)
    d = y - mu
    var = jnp.mean(d * d, axis=-1, keepdims=True)
    return d * lax.rsqrt(var + LN_EPS) * g + b


def _dot(a, b):
    return jnp.dot(a, b, preferred_element_type=F32)


def _dot_nt(a, b):
    return lax.dot_general(a, b, (((1,), (1,)), ((), ())), preferred_element_type=F32)


def _lane_half(shape):
    return lax.shift_right_logical(lax.broadcasted_iota(jnp.int32, shape, len(shape) - 1), 6)


def _ffn_kernel(x_ref, wg_ref, wu_ref, wo_ref, g_ref, b_ref, o_ref, xb_ref, acc_ref):
    j = pl.program_id(1)

    @pl.when(j == 0)
    def _():
        xb_ref[...] = x_ref[...].astype(BF16)
        acc_ref[...] = jnp.zeros_like(acc_ref)

    xb = xb_ref[...]
    gate = _dot(xb, wg_ref[...])
    up = _dot(xb, wu_ref[...])
    h = (gate * jax.nn.sigmoid(gate) * up).astype(BF16)
    acc_ref[...] += _dot(h, wo_ref[...])

    @pl.when(j == pl.num_programs(1) - 1)
    def _():
        y = ALPHA * x_ref[...] + 0.5 * acc_ref[...]
        o_ref[...] = _layer_norm(y, g_ref[...], b_ref[...])


def _ffn_ln(x, w_in, w_out, gain, bias, *, tm=512, tf=1408):
    T = x.shape[0]
    nf = D_FF // tf
    return pl.pallas_call(
        _ffn_kernel,
        grid=(T // tm, nf),
        in_specs=[
            pl.BlockSpec((tm, D_MODEL), lambda i, j: (i, 0)),
            pl.BlockSpec((D_MODEL, tf), lambda i, j: (0, j)),
            pl.BlockSpec((D_MODEL, tf), lambda i, j: (0, j + nf)),
            pl.BlockSpec((tf, D_MODEL), lambda i, j: (j, 0)),
            pl.BlockSpec((1, D_MODEL), lambda i, j: (0, 0)),
            pl.BlockSpec((1, D_MODEL), lambda i, j: (0, 0)),
        ],
        out_specs=pl.BlockSpec((tm, D_MODEL), lambda i, j: (i, 0)),
        out_shape=jax.ShapeDtypeStruct((T, D_MODEL), F32),
        scratch_shapes=[pltpu.VMEM((tm, D_MODEL), BF16), pltpu.VMEM((tm, D_MODEL), F32)],
        compiler_params=_cparams("parallel", "arbitrary"),
        name="ffn_ln",
    )(x, w_in, w_in, w_out, gain.reshape(1, -1), bias.reshape(1, -1))


def _rope_chunk(x, c, sa, sb, half):
    return x * c + pltpu.roll(x, LANES - half, 1) * sa + pltpu.roll(x, half, 1) * sb


def _rms_norm(x, g):
    return x * lax.rsqrt(jnp.mean(x * x, axis=-1, keepdims=True) + RMS_EPS) * g


_DIL_Q_CHUNKS = DIL_HEADS
_DIL_KV_CHUNKS = DIL_HEADS * HEAD_DIM // LANES


def _proj_kernel(*refs):
    (x_ref, w_ref, qn_ref, kvn_ref, wuq_ref, wkn_ref, wv_ref,
     c64_ref, sa64_ref, sb64_ref, cm_ref, sam_ref, sbm_ref, qa_ref, ka_ref, va_ref) = refs[:16]
    dil_refs = refs[16:16 + 3 * len(DIL_PATTERNS)]
    (nq_ref, nqr_ref, kc_ref, vc_ref, ksl_ref, vsl_ref, kw_ref, vw_ref, gl_ref,
     stage_ref) = refs[16 + 3 * len(DIL_PATTERNS):]
    tm = x_ref.shape[0]
    xb = x_ref[...].astype(BF16)
    c64, sa64, sb64 = c64_ref[...], sa64_ref[...], sb64_ref[...]
    cm, sam, sbm = cm_ref[...], sam_ref[...], sbm_ref[...]

    def proj(off, width):
        return _dot(xb, w_ref[:, off:off + width])

    def rope64(v):
        return _rope_chunk(v, c64, sa64, sb64, HEAD_DIM // 2)

    def rope_mla(v):
        return _rope_chunk(v, cm, sam, sbm, MLA_ROPE // 2)

    cq = _rms_norm(proj(_C_CQ, MLA_Q_LORA), qn_ref[...]).astype(BF16)
    q_raw = _dot(cq, wuq_ref[...])
    ckv = _rms_norm(proj(_C_CKV, MLA_KV_LORA), kvn_ref[...]).astype(BF16)
    k_nope = _dot(ckv, wkn_ref[...])
    va_ref[0, 0] = _dot(ckv, wv_ref[...]).T.astype(BF16)
    kpe = proj(_C_KPE, MLA_HEADS * LANES)
    q_scale = (MLA_NOPE + MLA_ROPE) ** -0.5 * LOG2_E
    for h in range(MLA_HEADS):
        sl = slice(h * LANES, (h + 1) * LANES)
        qa_ref[:, sl] = (rope_mla(q_raw[:, sl]) * q_scale).astype(BF16)
        ka_ref[:, sl] = (k_nope[:, sl] + rope_mla(kpe[:, sl])).astype(BF16)

    scale = HEAD_DIM ** -0.5
    dq = proj(_C_DQ, DIL_HEADS * LANES)
    dk = proj(_C_DK, DIL_HEADS * HEAD_DIM)
    dv = proj(_C_DV, DIL_HEADS * HEAD_DIM)
    for c in range(_DIL_Q_CHUNKS):
        stage_ref[c] = rope64(dq[:, c * LANES:(c + 1) * LANES]) * scale
    for c in range(_DIL_KV_CHUNKS):
        stage_ref[_DIL_Q_CHUNKS + c] = rope64(dk[:, c * LANES:(c + 1) * LANES])
        stage_ref[_DIL_Q_CHUNKS + _DIL_KV_CHUNKS + c] = dv[:, c * LANES:(c + 1) * LANES]
    for p, (_, r) in enumerate(DIL_PATTERNS):
        for first, chunks, o_ref in ((0, _DIL_Q_CHUNKS, dil_refs[3 * p]),
                                     (_DIL_Q_CHUNKS, _DIL_KV_CHUNKS, dil_refs[3 * p + 1]),
                                     (_DIL_Q_CHUNKS + _DIL_KV_CHUNKS, _DIL_KV_CHUNKS, dil_refs[3 * p + 2])):
            for j in range(r):
                for c in range(chunks):
                    rows = stage_ref[first + c, pl.ds(j, tm // r, stride=r), :]
                    o_ref[0, j, :, c * LANES:(c + 1) * LANES] = rows.astype(BF16)

    nq = proj(_C_NQ, NSA_HEADS * LANES)
    nq_ref[...] = (nq * scale).astype(BF16)
    for h in range(NSA_HEADS):
        sl = slice(h * LANES, (h + 1) * LANES)
        nqr_ref[:, sl] = (rope64(nq[:, sl]) * (scale * LOG2_E)).astype(BF16)
    kc_ref[...] = proj(_C_KC, LANES)
    vc_ref[...] = proj(_C_VC, LANES)
    ksl_ref[...] = rope64(proj(_C_KSL, LANES)).astype(BF16)
    vsl_ref[0, 0] = proj(_C_VSL, LANES).T.astype(BF16)
    kw_ref[...] = rope64(proj(_C_KW, LANES)).astype(BF16)
    vw_ref[...] = proj(_C_VW, LANES).astype(BF16)
    gl_ref[...] = jax.nn.sigmoid(proj(_C_GL, LANES))


def _proj_source_columns():
    src = -np.ones(_PROJ_WIDTH, np.int64)

    def put(dst, s0, n):
        src[dst:dst + n] = s0 + np.arange(n)

    put(_C_CQ, _SRC_CQ, MLA_Q_LORA)
    put(_C_CKV, _SRC_CKV, MLA_KV_LORA)
    for h in range(MLA_HEADS):
        put(_C_KPE + h * LANES + MLA_NOPE, _SRC_KPE, MLA_ROPE)
    for h in range(DIL_HEADS):
        put(_C_DQ + h * LANES + HEAD_DIM * (h % 2), _SRC_DQ + h * HEAD_DIM, HEAD_DIM)
    put(_C_DK, _SRC_DK, DIL_HEADS * HEAD_DIM)
    put(_C_DV, _SRC_DV, DIL_HEADS * HEAD_DIM)
    for h in range(NSA_HEADS):
        put(_C_NQ + h * LANES + HEAD_DIM * (h // NSA_HEADS_PER_GROUP), _SRC_NQ + h * HEAD_DIM, HEAD_DIM)
    for dst, s0 in ((_C_KC, _SRC_KC), (_C_VC, _SRC_VC), (_C_KSL, _SRC_KSL), (_C_VSL, _SRC_VSL),
                    (_C_KW, _SRC_KW), (_C_VW, _SRC_VW)):
        put(dst, s0, NSA_KV_GROUPS * HEAD_DIM)
    put(_C_GL, _SRC_GL, NSA_HEADS * 3)
    return src


def _gather_columns(w, src, axis):
    valid = jnp.asarray(src >= 0)
    taken = jnp.take(w, jnp.asarray(np.maximum(src, 0)), axis=axis)
    shape = [1, 1]
    shape[axis] = -1
    return jnp.where(valid.reshape(shape), taken, 0.0)


def _mla_weight_layout(w_uq, w_ukv):
    dq = MLA_NOPE + MLA_ROPE
    src_q = -np.ones(MLA_HEADS * LANES, np.int64)
    src_kn = -np.ones(MLA_HEADS * LANES, np.int64)
    src_v = np.zeros(MLA_HEADS * MLA_V, np.int64)
    for h in range(MLA_HEADS):
        src_q[h * LANES:h * LANES + dq] = h * dq + np.arange(dq)
        src_kn[h * LANES:h * LANES + MLA_NOPE] = h * (MLA_NOPE + MLA_V) + np.arange(MLA_NOPE)
        src_v[h * MLA_V:(h + 1) * MLA_V] = h * (MLA_NOPE + MLA_V) + MLA_NOPE + np.arange(MLA_V)
    return (_gather_columns(w_uq, src_q, 1).astype(BF16),
            _gather_columns(w_ukv, src_kn, 1).astype(BF16),
            _gather_columns(w_ukv, src_v, 1).astype(BF16))


def _rope_tables(S):
    def cos_sin(dim):
        inv_freq = ROPE_THETA ** (-jnp.arange(0, dim, 2, dtype=F32) / dim)
        ang = jnp.arange(S, dtype=F32)[:, None] * inv_freq[None, :]
        return jnp.cos(ang), jnp.sin(ang)

    cos, sin = cos_sin(HEAD_DIM)
    zero = jnp.zeros_like(sin)
    c64 = jnp.concatenate([cos, cos] * 2, axis=1)
    sa64 = jnp.concatenate([-sin, zero] * 2, axis=1)
    sb64 = jnp.concatenate([zero, sin] * 2, axis=1)
    cos_m, sin_m = cos_sin(MLA_ROPE)
    ones = jnp.ones((S, MLA_NOPE), F32)
    z64 = jnp.zeros((S, MLA_NOPE), F32)
    z16 = jnp.zeros_like(sin_m)
    z32 = jnp.zeros((S, LANES - MLA_NOPE - MLA_ROPE), F32)
    cm = jnp.concatenate([ones, cos_m, cos_m, z32], axis=1)
    sam = jnp.concatenate([z64, -sin_m, z16, z32], axis=1)
    sbm = jnp.concatenate([z64, z16, sin_m, z32], axis=1)
    return c64, sa64, sb64, cm, sam, sbm


def _input_projection(h, w_big, q_norm, kv_norm, wuq, wkn, wv, tables, S, *, tm=KV_TILE):
    T = h.shape[0]
    n_pos = S // tm
    row = lambda w: pl.BlockSpec((tm, w), lambda i: (i, 0))
    full = lambda a: pl.BlockSpec(a.shape, lambda i: (0, 0))
    tab = pl.BlockSpec((tm, LANES), lambda i: (i % n_pos, 0))
    B = T // S
    specs, shapes = [], []

    def rows_out(w, dt):
        specs.append(row(w))
        shapes.append(jax.ShapeDtypeStruct((T, w), dt))

    def tile_t_out(c):
        specs.append(pl.BlockSpec((1, 1, c, tm), lambda i: (i // n_pos, i % n_pos, 0, 0)))
        shapes.append(jax.ShapeDtypeStruct((B, n_pos, c, tm), BF16))

    def residue_out(r, w):
        specs.append(pl.BlockSpec((1, r, tm // r, w), lambda i: (i // n_pos, 0, i % n_pos, 0)))
        shapes.append(jax.ShapeDtypeStruct((B, r, S // r, w), BF16))

    rows_out(MLA_HEADS * LANES, BF16)
    rows_out(MLA_HEADS * LANES, BF16)
    tile_t_out(MLA_HEADS * MLA_V)
    for _, r in DIL_PATTERNS:
        residue_out(r, DIL_HEADS * LANES)
        residue_out(r, DIL_HEADS * HEAD_DIM)
        residue_out(r, DIL_HEADS * HEAD_DIM)
    rows_out(NSA_HEADS * LANES, BF16)
    rows_out(NSA_HEADS * LANES, BF16)
    rows_out(LANES, F32)
    rows_out(LANES, F32)
    rows_out(LANES, BF16)
    tile_t_out(LANES)
    rows_out(LANES, BF16)
    rows_out(LANES, BF16)
    rows_out(LANES, F32)
    qn = q_norm.reshape(1, -1)
    kvn = kv_norm.reshape(1, -1)
    return pl.pallas_call(
        _proj_kernel,
        grid=(T // tm,),
        in_specs=[row(D_MODEL), full(w_big), full(qn), full(kvn), full(wuq), full(wkn), full(wv)]
                 + [tab] * 6,
        out_specs=specs,
        out_shape=shapes,
        scratch_shapes=[pltpu.VMEM((_DIL_Q_CHUNKS + 2 * _DIL_KV_CHUNKS, tm, LANES), F32)],
        compiler_params=_cparams("parallel"),
        name="in_proj",
    )(h, w_big, qn, kvn, wuq, wkn, wv, *tables)


ONES_ROWS = 16


def _flash_step_t(s, vt1, m_ref, acc_ref, idx):
    m_prev = m_ref[idx]
    m_new = jnp.maximum(m_prev, jnp.max(s, axis=0, keepdims=True))
    alpha = jnp.exp2(m_prev - m_new)
    p = jnp.exp2(s - m_new).astype(BF16)
    acc_ref[idx] = alpha * acc_ref[idx] + _dot(vt1, p)
    m_ref[idx] = m_new


def _init_softmax_state(m_ref, acc_ref):
    m_ref[...] = jnp.full(m_ref.shape, NEG_INF, F32)
    acc_ref[...] = jnp.zeros_like(acc_ref)


def _flash_state(slots, channels, queries):
    return [pltpu.VMEM((slots, 1, queries), F32), pltpu.VMEM((slots, channels + ONES_ROWS, queries), F32)]


def _with_ones_rows(vt):
    return jnp.concatenate([vt, jnp.ones((ONES_ROWS, vt.shape[1]), vt.dtype)], axis=0)


def _normalised(acc, channels):
    return acc[:channels] / acc[channels:channels + 1]


def _mla_kernel(q_ref, k_ref, vt_ref, o_ref, m_ref, acc_ref, *, tq, tk):
    q0 = pl.program_id(1) * tq
    _init_softmax_state(m_ref, acc_ref)
    kofs = lax.broadcasted_iota(jnp.int32, (tk, tq), 0)
    qpos = q0 + lax.broadcasted_iota(jnp.int32, (tk, tq), 1)

    def tile(kt, masked):
        start = pl.multiple_of(kt * tk, tk)

        def scores(h):
            hsl = slice(h * LANES, (h + 1) * LANES)
            s = _dot_nt(k_ref[0, pl.ds(start, tk), hsl], q_ref[0, :, hsl])
            return jnp.where((start + kofs) <= qpos, s, NEG_INF) if masked else s

        s_next = scores(0)
        for h in range(MLA_HEADS):
            s = s_next
            if h + 1 < MLA_HEADS:
                s_next = scores(h + 1)
            vt = vt_ref[0, kt, (h // 2) * LANES:(h // 2 + 1) * LANES, :]
            _flash_step_t(s, _with_ones_rows(vt), m_ref, acc_ref, h)

    def body(kt, carry):
        tile(kt, False)
        return carry

    n_full = q0 // tk
    lax.fori_loop(0, n_full, body, 0)
    for d in range(tq // tk):
        tile(n_full + d, True)
    even_rows = lax.broadcasted_iota(jnp.int32, (LANES, tq), 0) < MLA_V
    for pair in range(MLA_HEADS // 2):
        h0, h1 = 2 * pair, 2 * pair + 1
        out_t = jnp.where(even_rows, _normalised(acc_ref[h0], LANES), _normalised(acc_ref[h1], LANES))
        o_ref[0, :, pair * LANES:(pair + 1) * LANES] = out_t.T.astype(o_ref.dtype)


def _mla_attention(q, k, vt, *, tq=1024, tk=KV_TILE):
    B, S, _ = q.shape
    return pl.pallas_call(
        functools.partial(_mla_kernel, tq=tq, tk=tk),
        grid=(B, S // tq),
        in_specs=[
            pl.BlockSpec((1, tq, MLA_HEADS * LANES), lambda b, i: (b, i, 0)),
            pl.BlockSpec((1, S, MLA_HEADS * LANES), lambda b, i: (b, 0, 0)),
            pl.BlockSpec((1, S // tk, MLA_HEADS * MLA_V, tk), lambda b, i: (b, 0, 0, 0)),
        ],
        out_specs=pl.BlockSpec((1, tq, MLA_HEADS * MLA_V), lambda b, i: (b, i, 0)),
        out_shape=jax.ShapeDtypeStruct((B, S, MLA_HEADS * MLA_V), BF16),
        scratch_shapes=_flash_state(MLA_HEADS, LANES, tq),
        compiler_params=_cparams("parallel", "arbitrary"),
        name="mla_attn",
    )(q, k, vt)


def _dot_tn(a, b):
    return lax.dot_general(a, b, (((0,), (0,)), ((), ())), preferred_element_type=F32)


def _band_mask_t(q0, kstart, tq, span, window):
    kpos = kstart + lax.broadcasted_iota(jnp.int32, (span, tq), 0)
    qpos = q0 + lax.broadcasted_iota(jnp.int32, (span, tq), 1)
    dist = qpos - kpos
    return (dist >= 0) & (dist <= window)


def _dil_kernel(q_ref, k_ref, v_ref, o_ref, lse_ref, *, tq, wpad, window):
    q0 = pl.program_id(2) * tq
    n = k_ref.shape[2]
    span = min(tq + wpad, n)
    kstart = pl.multiple_of(jnp.clip(q0 - wpad, 0, n - span), LANES)
    mask = _band_mask_t(q0, kstart, tq, span, window)
    even_rows = lax.broadcasted_iota(jnp.int32, (LANES, tq), 0) < HEAD_DIM
    for c in range(DIL_HEADS // 2):
        csl = slice(c * LANES, (c + 1) * LANES)
        k = k_ref[0, 0, pl.ds(kstart, span), csl]
        v = v_ref[0, 0, pl.ds(kstart, span), csl]
        outs, lses = [], []
        for hh in range(2):
            h = 2 * c + hh
            s = jnp.where(mask, _dot_nt(k, q_ref[0, 0, :, h * LANES:(h + 1) * LANES]), NEG_INF)
            m = jnp.max(s, axis=0, keepdims=True)
            e = jnp.exp(s - m)
            l = jnp.sum(e, axis=0, keepdims=True)
            outs.append(_dot_tn(v, e.astype(BF16)) * (1.0 / l))
            lses.append(jnp.broadcast_to(m + jnp.log(l), (LANES, tq)))
        o_ref[0, 0, :, csl] = jnp.where(even_rows, outs[0], outs[1]).T
        lse_ref[0, 0, :, csl] = jnp.where(even_rows, lses[0], lses[1]).T


def _dilated_pattern(q, k, v, window, dil, *, tq=512):
    B, _, n, wq = q.shape
    wk = k.shape[-1]
    wsub = window // dil
    wpad = -(-wsub // LANES) * LANES
    tq = min(tq, n)
    return pl.pallas_call(
        functools.partial(_dil_kernel, tq=tq, wpad=wpad, window=wsub),
        grid=(B, dil, n // tq),
        in_specs=[
            pl.BlockSpec((1, 1, tq, wq), lambda b, j, i: (b, j, i, 0)),
            pl.BlockSpec((1, 1, n, wk), lambda b, j, i: (b, j, 0, 0)),
            pl.BlockSpec((1, 1, n, wk), lambda b, j, i: (b, j, 0, 0)),
        ],
        out_specs=[pl.BlockSpec((1, 1, tq, wk), lambda b, j, i: (b, j, i, 0))] * 2,
        out_shape=[jax.ShapeDtypeStruct((B, dil, n, wk), F32)] * 2,
        compiler_params=_cparams("parallel", "parallel", "arbitrary"),
        name=f"dilated_r{dil}",
    )(q, k, v)


def _compress_kernel(xk_ref, xv_ref, plo_ref, phi_ref, w1_ref, w2_ref, kc_ref, vc_ref):
    n = xk_ref.shape[2]
    half_rows = w1_ref.shape[1] // 2
    for t, (x_ref, o_ref) in enumerate(((xk_ref, kc_ref), (xv_ref, vc_ref))):
        x = x_ref[0, 0]
        a = _dot((x + plo_ref[t]).astype(BF16), w1_ref[t, :half_rows, :])
        b = _dot((x + phi_ref[t]).astype(BF16), w1_ref[t, half_rows:, :])
        hid = a + pltpu.roll(b, n - 1, 0)
        hid = (hid * jax.nn.sigmoid(hid)).astype(BF16)
        o_ref[0, 0] = _dot(hid, w2_ref[t])


def _nsa_compress(kc, vc, cmp_pos, cmp_w1, cmp_w2):
    B, S, _ = kc.shape
    G, d, st = NSA_KV_GROUPS, HEAD_DIM, NSA_CMP_STRIDE
    n = S // st

    def rows(t):
        return t.reshape(B, n, st, G, d).transpose(0, 3, 1, 2, 4).reshape(B, G, n, st * d)

    pos = cmp_pos.reshape(2, NSA_CMP_BLOCK // st, 1, st * d)
    xspec = pl.BlockSpec((1, 1, n, st * d), lambda b, g: (b, g, 0, 0))
    ospec = pl.BlockSpec((1, 1, n, d), lambda b, g: (b, g, 0, 0))
    full = lambda a: pl.BlockSpec(a.shape, lambda b, g: (0,) * a.ndim)
    plo, phi = pos[:, 0], pos[:, 1]
    w1 = cmp_w1.astype(BF16)
    w2 = cmp_w2.astype(BF16)
    kcc, vcc = pl.pallas_call(
        _compress_kernel,
        grid=(B, G),
        in_specs=[xspec, xspec, full(plo), full(phi), full(w1), full(w2)],
        out_specs=[ospec, ospec],
        out_shape=[jax.ShapeDtypeStruct((B, G, n, d), F32)] * 2,
        compiler_params=_cparams("parallel", "parallel"),
        name="nsa_compress",
    )(rows(kc), rows(vc), plo, phi, w1, w2)
    pair = lambda t: t.transpose(0, 2, 1, 3).reshape(B, n, G * d).astype(BF16)
    return pair(kcc), pair(vcc)


def _dot_01_by_f32(o01, p):
    p1 = p.astype(BF16)
    r1 = p - p1.astype(F32)
    p2 = r1.astype(BF16)
    p3 = (r1 - p2.astype(F32)).astype(BF16)
    return _dot(o01, p1) + _dot(o01, p2) + _dot(o01, p3)


SUBLANES = 8


def _count_outranking(sc_ref, cnt_ref, n_sources, tq):
    groups = SEL_LANES // SUBLANES
    sub = lax.broadcasted_iota(jnp.int32, (SUBLANES, tq), 0)
    cnt_ref[...] = jnp.zeros_like(cnt_ref)
    for bi in range(groups):
        @pl.when(bi * SUBLANES < n_sources)
        def _():
            src = sc_ref[bi * SUBLANES:(bi + 1) * SUBLANES, :]
            rows = [jnp.broadcast_to(src[ii:ii + 1, :], (SUBLANES, tq)) for ii in range(SUBLANES)]
            for r in range(groups):
                rsl = slice(r * SUBLANES, (r + 1) * SUBLANES)
                tgt = sc_ref[rsl, :]
                cnt = cnt_ref[rsl, :]
                for ii, row in enumerate(rows):
                    if r > bi:
                        ahead = row >= tgt
                    elif r < bi:
                        ahead = row > tgt
                    else:
                        ahead = (row > tgt) | ((row == tgt) & (sub > ii))
                    cnt = cnt + jnp.where(ahead, 1.0, 0.0)
                cnt_ref[rsl, :] = cnt


def _cmp_kernel(q_ref, kc_ref, vct_ref, ovt_ref, o_ref, bias_ref, sc_ref, cnt_ref, *, tq):
    g = pl.program_id(1)
    q0 = pl.program_id(2) * tq
    n = kc_ref.shape[1]
    kc = kc_ref[0]
    vct = vct_ref[0]
    qpos = q0 + lax.broadcasted_iota(jnp.int32, (n, tq), 1)
    cmp_end = lax.broadcasted_iota(jnp.int32, (n, tq), 0) * NSA_CMP_STRIDE + (NSA_CMP_BLOCK - 1)
    visible = cmp_end <= qpos
    in_group = _lane_half((tq, LANES)) == g
    psum = jnp.zeros((n, tq), F32)

    def scores(hh):
        return jnp.where(visible, _dot_nt(kc, q_ref[0, :, hh * LANES:(hh + 1) * LANES]), NEG_INF)

    s_next = scores(0)
    for hh in range(NSA_HEADS_PER_GROUP):
        s = s_next
        if hh + 1 < NSA_HEADS_PER_GROUP:
            s_next = scores(hh + 1)
        m = jnp.max(s, axis=0, keepdims=True)
        e = jnp.where(visible, jnp.exp(s - m), 0.0)
        l = jnp.maximum(jnp.sum(e, axis=0, keepdims=True), 1e-30)
        p = e * (1.0 / l)
        o = _dot(vct, p.astype(BF16)).T
        o_ref[0, :, hh * LANES:(hh + 1) * LANES] = jnp.where(in_group, o, 0.0).astype(o_ref.dtype)
        psum = psum + p

    imp_t = _dot_01_by_f32(ovt_ref[...], psum)
    blk = lax.broadcasted_iota(jnp.int32, (SEL_LANES, tq), 0)
    cur = lax.shift_right_logical(q0 + lax.broadcasted_iota(jnp.int32, (SEL_LANES, tq), 1), 6)
    forced = (blk == 0) | (blk == cur) | (blk == cur - 1)
    sc_ref[...] = jnp.where(forced, FORCE_SCORE, jnp.where(blk <= cur, imp_t, -FORCE_SCORE))
    n_causal = jnp.minimum(lax.shift_right_logical(q0 + tq - 1, 6) + 1, SEL_LANES)
    _count_outranking(sc_ref, cnt_ref, n_causal, tq)
    bias_t = jnp.where(cnt_ref[...] < float(NSA_TOP_N), 0.0, UNSELECTED_BIAS)
    bias_ref[0, 0] = bias_t.T.astype(bias_ref.dtype)


def _nsa_compressed_branch(nq, kcc, vcc, overlap, *, tq=256):
    B, S, _ = nq.shape
    G = NSA_KV_GROUPS
    n = kcc.shape[1]
    wq = NSA_HEADS_PER_GROUP * LANES
    return pl.pallas_call(
        functools.partial(_cmp_kernel, tq=tq),
        grid=(B, G, S // tq),
        in_specs=[
            pl.BlockSpec((1, tq, wq), lambda b, g, i: (b, i, g)),
            pl.BlockSpec((1, n, LANES), lambda b, g, i: (b, 0, 0)),
            pl.BlockSpec((1, LANES, n), lambda b, g, i: (b, 0, 0)),
            pl.BlockSpec((SEL_LANES, n), lambda b, g, i: (0, 0)),
        ],
        out_specs=[
            pl.BlockSpec((1, tq, wq), lambda b, g, i: (b, i, g)),
            pl.BlockSpec((1, 1, tq, SEL_LANES), lambda b, g, i: (b, g, i, 0)),
        ],
        out_shape=[jax.ShapeDtypeStruct((B, S, NSA_HEADS * LANES), BF16),
                   jax.ShapeDtypeStruct((B, G, S, SEL_LANES), BF16)],
        scratch_shapes=[pltpu.VMEM((SEL_LANES, tq), F32), pltpu.VMEM((SEL_LANES, tq), F32)],
        compiler_params=_cparams("parallel", "parallel", "arbitrary"),
        name="nsa_cmp_topk",
    )(nq, kcc, jnp.swapaxes(vcc, 1, 2), overlap.T)


def _sel_kernel(q_ref, bias_ref, k_ref, vt_ref, oh_ref, o_ref, qa_ref, m_ref, acc_ref, *, tq, tk):
    q0 = pl.program_id(1) * tq
    for h in range(NSA_HEADS):
        qa_ref[h, :, :LANES] = q_ref[0, :, h * LANES:(h + 1) * LANES]
        qa_ref[h, :, LANES:] = bias_ref[0, h // NSA_HEADS_PER_GROUP]
    kofs = lax.broadcasted_iota(jnp.int32, (tk, tq), 0)
    qpos = q0 + lax.broadcasted_iota(jnp.int32, (tk, tq), 1)
    _init_softmax_state(m_ref, acc_ref)

    def tile(kt, masked):
        start = pl.multiple_of(kt * tk, tk)
        ka = jnp.concatenate([k_ref[0, pl.ds(start, tk), :], oh_ref[pl.ds(start, tk), :]], axis=1)
        vt1 = _with_ones_rows(vt_ref[0, kt])

        def scores(h):
            s = _dot_nt(ka, qa_ref[h])
            return jnp.where((start + kofs) <= qpos, s, NEG_INF) if masked else s

        s_next = scores(0)
        for h in range(NSA_HEADS):
            s = s_next
            if h + 1 < NSA_HEADS:
                s_next = scores(h + 1)
            _flash_step_t(s, vt1, m_ref, acc_ref, h)

    def body(kt, carry):
        tile(kt, False)
        return carry

    n_full = q0 // tk
    lax.fori_loop(0, n_full, body, 0)
    tile(n_full, True)
    half = _lane_half((tq, LANES))
    for h in range(NSA_HEADS):
        out = _normalised(acc_ref[h], LANES).T
        o_ref[0, :, h * LANES:(h + 1) * LANES] = jnp.where(
            half == h // NSA_HEADS_PER_GROUP, out, 0.0).astype(o_ref.dtype)


def _nsa_selected_branch(nqr, bias, ksl, vslt, onehot, *, tq=512, tk=KV_TILE):
    B, S, wq = nqr.shape
    G = NSA_KV_GROUPS
    return pl.pallas_call(
        functools.partial(_sel_kernel, tq=tq, tk=tk),
        grid=(B, S // tq),
        in_specs=[
            pl.BlockSpec((1, tq, wq), lambda b, i: (b, i, 0)),
            pl.BlockSpec((1, G, tq, SEL_LANES), lambda b, i: (b, 0, i, 0)),
            pl.BlockSpec((1, S, LANES), lambda b, i: (b, 0, 0)),
            pl.BlockSpec((1, S // tk, LANES, tk), lambda b, i: (b, 0, 0, 0)),
            pl.BlockSpec((S, SEL_LANES), lambda b, i: (0, 0)),
        ],
        out_specs=pl.BlockSpec((1, tq, wq), lambda b, i: (b, i, 0)),
        out_shape=jax.ShapeDtypeStruct((B, S, wq), BF16),
        scratch_shapes=[pltpu.VMEM((NSA_HEADS, tq, LANES + SEL_LANES), BF16)]
                       + _flash_state(NSA_HEADS, LANES, tq),
        compiler_params=_cparams("parallel", "arbitrary"),
        name="nsa_selected",
    )(nqr, bias, ksl, vslt, onehot)


def _win_kernel(q_ref, k_ref, v_ref, o_ref, *, tq, wpad, window):
    g = pl.program_id(1)
    q0 = pl.program_id(2) * tq
    nh = NSA_HEADS_PER_GROUP
    span = tq + wpad
    kstart = pl.multiple_of(jnp.maximum(q0 - wpad, 0), LANES)
    mask = _band_mask_t(q0, kstart, tq, span, window)
    k = k_ref[0, pl.ds(kstart, span), :]
    v = v_ref[0, pl.ds(kstart, span), :]
    in_group = _lane_half((tq, LANES)) == g
    for hh in range(nh):
        hsl = slice(hh * LANES, (hh + 1) * LANES)
        s = jnp.where(mask, _dot_nt(k, q_ref[0, :, hsl]), NEG_INF)
        m = jnp.max(s, axis=0, keepdims=True)
        e = jnp.exp2(s - m)
        l = jnp.sum(e, axis=0, keepdims=True)
        o = (_dot_tn(v, e.astype(BF16)) * (1.0 / l)).T
        o_ref[0, :, hsl] = jnp.where(in_group, o, 0.0).astype(o_ref.dtype)


def _nsa_window_branch(nqr, kw, vw, *, tq=256):
    B, S, _ = nqr.shape
    G = NSA_KV_GROUPS
    wq = NSA_HEADS_PER_GROUP * LANES
    wpad = -(-NSA_WINDOW // LANES) * LANES
    return pl.pallas_call(
        functools.partial(_win_kernel, tq=tq, wpad=wpad, window=NSA_WINDOW),
        grid=(B, G, S // tq),
        in_specs=[
            pl.BlockSpec((1, tq, wq), lambda b, g, i: (b, i, g)),
            pl.BlockSpec((1, S, LANES), lambda b, g, i: (b, 0, 0)),
            pl.BlockSpec((1, S, LANES), lambda b, g, i: (b, 0, 0)),
        ],
        out_specs=pl.BlockSpec((1, tq, wq), lambda b, g, i: (b, i, g)),
        out_shape=jax.ShapeDtypeStruct((B, S, NSA_HEADS * LANES), BF16),
        compiler_params=_cparams("parallel", "parallel", "arbitrary"),
        name="nsa_window",
    )(nqr, kw, vw)


def _selection_constants(S):
    n_cmp_rows = S // NSA_CMP_STRIDE
    c = np.arange(n_cmp_rows)
    cmp_start = c * NSA_CMP_STRIDE
    cmp_end = cmp_start + NSA_CMP_BLOCK - 1
    sel_start = np.arange(SEL_LANES) * NSA_SEL_BLOCK
    overlap = ((cmp_start[:, None] < sel_start[None, :] + NSA_SEL_BLOCK) &
               (cmp_end[:, None] >= sel_start[None, :]))
    n_cmp = (S - NSA_CMP_BLOCK) // NSA_CMP_STRIDE + 1
    overlap &= (c < n_cmp)[:, None]
    onehot = (np.arange(S)[:, None] // NSA_SEL_BLOCK) == np.arange(SEL_LANES)[None, :]
    return jnp.asarray(overlap, BF16), jnp.asarray(onehot, BF16)


def _out_kernel(*refs):
    n_pat = len(DIL_PATTERNS)
    x_ref, oa_ref = refs[:2]
    dil_refs = refs[2:2 + 2 * n_pat]
    (oc_ref, os_ref, ow_ref, gl_ref, wa_ref, wb_ref, wc_ref, g_ref, b_ref, o_ref,
     nat_ref) = refs[2 + 2 * n_pat:]
    tm = x_ref.shape[0]
    mix = _dot(oa_ref[...], wa_ref[...])

    for p, (_, r) in enumerate(DIL_PATTERNS):
        for a in range(2):
            for j in range(r):
                for c in range(_DIL_KV_CHUNKS):
                    nat_ref[2 * p + a, c, pl.ds(j, tm // r, stride=r), :] = (
                        dil_refs[2 * p + a][0, j, :, c * LANES:(c + 1) * LANES])
    for c in range(_DIL_KV_CHUNKS):
        outs = [nat_ref[2 * p, c] for p in range(n_pat)]
        lses = [nat_ref[2 * p + 1, c] for p in range(n_pat)]
        mx = functools.reduce(jnp.maximum, lses)
        es = [jnp.exp(l - mx) for l in lses]
        inv = 1.0 / functools.reduce(jnp.add, es)
        ob = functools.reduce(jnp.add, [(e * inv) * o for e, o in zip(es, outs)])
        mix = mix + _dot(ob.astype(BF16), wb_ref[c * LANES:(c + 1) * LANES, :])

    gl = gl_ref[...]
    for h in range(NSA_HEADS):
        sl = slice(h * LANES, (h + 1) * LANES)
        oc = (gl[:, 3 * h:3 * h + 1] * oc_ref[:, sl].astype(F32)
              + gl[:, 3 * h + 1:3 * h + 2] * os_ref[:, sl].astype(F32)
              + gl[:, 3 * h + 2:3 * h + 3] * ow_ref[:, sl].astype(F32))
        mix = mix + _dot(oc.astype(BF16), wc_ref[sl, :])

    y = ALPHA * x_ref[...] + mix
    o_ref[...] = _layer_norm(y, g_ref[...], b_ref[...])


def _out_weight_layout(w_out):
    na = MLA_HEADS * MLA_V
    nb = DIL_HEADS * HEAD_DIM
    src = -np.ones(NSA_HEADS * LANES, np.int64)
    for h in range(NSA_HEADS):
        d0 = h * LANES + HEAD_DIM * (h // NSA_HEADS_PER_GROUP)
        src[d0:d0 + HEAD_DIM] = na + nb + h * HEAD_DIM + np.arange(HEAD_DIM)
    return (w_out[:na].astype(BF16), w_out[na:na + nb].astype(BF16),
            _gather_columns(w_out, src, 0).astype(BF16))


def _output_projection(x, oa, dil, oc, osl, ow, gl, wa, wb, wc, gain, bias, S, *, tm=512):
    T = x.shape[0]
    n_pos = S // tm
    row = lambda a: pl.BlockSpec((tm, a.shape[1]), lambda i: (i, 0))
    full = lambda a: pl.BlockSpec(a.shape, lambda i: (0, 0))
    residue = lambda a: pl.BlockSpec((1, a.shape[1], tm // a.shape[1], a.shape[3]),
                                     lambda i: (i // n_pos, 0, i % n_pos, 0))
    g2, b2 = gain.reshape(1, -1), bias.reshape(1, -1)
    dil_flat = [a for pair in dil for a in pair]
    rows = [oc, osl, ow, gl]
    consts = [wa, wb, wc, g2, b2]
    return pl.pallas_call(
        _out_kernel,
        grid=(T // tm,),
        in_specs=[row(x), row(oa)] + [residue(a) for a in dil_flat] + [row(a) for a in rows]
                 + [full(a) for a in consts],
        out_specs=pl.BlockSpec((tm, D_MODEL), lambda i: (i, 0)),
        out_shape=jax.ShapeDtypeStruct((T, D_MODEL), F32),
        scratch_shapes=[pltpu.VMEM((len(dil_flat), _DIL_KV_CHUNKS, tm, LANES), F32)],
        compiler_params=_cparams("parallel"),
        name="out_proj_ln",
    )(x, oa, *dil_flat, *rows, *consts)


def kernel(x, ffn_w_in, ffn_w_out, ln_gain, ln_bias, w_in, w_out, mla_q_norm, mla_kv_norm,
           mla_w_uq, mla_w_ukv, nsa_cmp_pos, nsa_cmp_w1, nsa_cmp_w2):
    B, S, D = x.shape
    assert D == D_MODEL and S % 2048 == 0 and S // NSA_SEL_BLOCK <= SEL_LANES
    T = B * S
    tables = _rope_tables(S)
    overlap, onehot = _selection_constants(S)
    proj_src = _proj_source_columns()
    bs = lambda t: t.reshape(B, S, t.shape[-1])

    xf = x.reshape(T, D)
    for l in range(DEPTH):
        xf = _ffn_ln(xf, ffn_w_in[l, 0].astype(BF16), ffn_w_out[l, 0].astype(BF16),
                     ln_gain[l, 0], ln_bias[l, 0])

        w_big = _gather_columns(w_in[l], proj_src, 1).astype(BF16)
        wuq, wkn, wv = _mla_weight_layout(mla_w_uq[l], mla_w_ukv[l])
        outs = _input_projection(xf, w_big, mla_q_norm[l], mla_kv_norm[l], wuq, wkn, wv, tables, S)
        qa, ka, va = outs[:3]
        n_dil = 3 * len(DIL_PATTERNS)
        dqkv = outs[3:3 + n_dil]
        nq, nqr, kc, vc, ksl, vsl, kw, vw, gl = outs[3 + n_dil:]

        oa = _mla_attention(bs(qa), bs(ka), va).reshape(T, -1)
        dil = [_dilated_pattern(*dqkv[3 * p:3 * p + 3], window, r)
               for p, (window, r) in enumerate(DIL_PATTERNS)]
        kcc, vcc = _nsa_compress(bs(kc), bs(vc), nsa_cmp_pos[l], nsa_cmp_w1[l], nsa_cmp_w2[l])
        oc, sel_bias = _nsa_compressed_branch(bs(nq), kcc, vcc, overlap)
        osl = _nsa_selected_branch(bs(nqr), sel_bias, bs(ksl), vsl, onehot)
        ow = _nsa_window_branch(bs(nqr), bs(kw), bs(vw))

        wa, wb, wc = _out_weight_layout(w_out[l])
        xf = _output_projection(xf, oa, dil, oc.reshape(T, -1), osl.reshape(T, -1), ow.reshape(T, -1),
                                gl, wa, wb, wc, ln_gain[l, 1], ln_bias[l, 1], S)

        xf = _ffn_ln(xf, ffn_w_in[l, 1].astype(BF16), ffn_w_out[l, 1].astype(BF16),
                     ln_gain[l, 2], ln_bias[l, 2])
    return xf.reshape(B, S, D)
```

```python
import functools

import numpy as np
import jax
import jax.numpy as jnp
from jax import lax
from jax.experimental import pallas as pl
from jax.experimental.pallas import tpu as pltpu

F32 = jnp.float32
BF16 = jnp.bfloat16

D_MODEL = 1024
DEPTH = 2
HEAD_DIM = 64
MLA_HEADS = 4
MLA_Q_LORA = 256
MLA_KV_LORA = 128
MLA_NOPE = 64
MLA_ROPE = 32
MLA_V = 64
DIL_HEADS = 6
DIL_PATTERNS = ((128, 1), (512, 4), (2048, 16))
NSA_HEADS = 6
NSA_KV_GROUPS = 2
NSA_HEADS_PER_GROUP = NSA_HEADS // NSA_KV_GROUPS
NSA_CMP_BLOCK = 32
NSA_CMP_STRIDE = 16
NSA_CMP_HIDDEN = 256
NSA_SEL_BLOCK = 64
NSA_TOP_N = 16
NSA_WINDOW = 512
D_FF = ((8 * D_MODEL // 3 + 255) // 256) * 256
ROPE_THETA = 10000.0
LN_EPS = 1e-5
RMS_EPS = 1e-6
NEG_INF = -1e30
FORCE_SCORE = 1e4
ALPHA = (2 * DEPTH) ** 0.25
LOG2_E = 1.4426950408889634
LN_2 = 0.6931471805599453

LANES = 128
SEL_LANES = 128
UNSELECTED_BIAS = -1e9
VMEM_LIMIT = 56 * 1024 * 1024
KV_TILE = 512

_SRC_CQ, _SRC_CKV, _SRC_KPE = 0, 256, 384
_SRC_DQ, _SRC_DK, _SRC_DV = 416, 800, 1184
_SRC_NQ = 1568
_SRC_KC, _SRC_VC, _SRC_KSL, _SRC_VSL, _SRC_KW, _SRC_VW = 1952, 2080, 2208, 2336, 2464, 2592
_SRC_GL = 2720
_C_CQ, _C_CKV, _C_KPE, _C_DQ, _C_DK, _C_DV, _C_NQ = 0, 256, 384, 896, 1664, 2048, 2432
_C_KC, _C_VC, _C_KSL, _C_VSL, _C_KW, _C_VW, _C_GL = 3200, 3328, 3456, 3584, 3712, 3840, 3968
_PROJ_WIDTH = 4096


def _cparams(*sem):
    return pltpu.CompilerParams(dimension_semantics=sem, vmem_limit_bytes=VMEM_LIMIT)


def _layer_norm(y, g, b):
    mu = jnp.mean(y, axis=-1, keepdims=True)
    d = y - mu
    var = jnp.mean(d * d, axis=-1, keepdims=True)
    return d * lax.rsqrt(var + LN_EPS) * g + b


def _dot(a, b):
    return jnp.dot(a, b, preferred_element_type=F32)


def _dot_nt(a, b):
    return lax.dot_general(a, b, (((1,), (1,)), ((), ())), preferred_element_type=F32)


def _lane_half(shape):
    return lax.shift_right_logical(lax.broadcasted_iota(jnp.int32, shape, len(shape) - 1), 6)


def _ffn_kernel(x_ref, wg_ref, wu_ref, wo_ref, g_ref, b_ref, o_ref, xb_ref, acc_ref):
    j = pl.program_id(1)

    @pl.when(j == 0)
    def _():
        xb_ref[...] = x_ref[...].astype(BF16)
        acc_ref[...] = jnp.zeros_like(acc_ref)

    xb = xb_ref[...]
    gate = _dot(xb, wg_ref[...])
    up = _dot(xb, wu_ref[...])
    h = (gate * jax.nn.sigmoid(gate) * up).astype(BF16)
    acc_ref[...] += _dot(h, wo_ref[...])

    @pl.when(j == pl.num_programs(1) - 1)
    def _():
        y = ALPHA * x_ref[...] + 0.5 * acc_ref[...]
        o_ref[...] = _layer_norm(y, g_ref[...], b_ref[...])


def _ffn_ln(x, w_in, w_out, gain, bias, *, tm=512, tf=1408):
    T = x.shape[0]
    nf = D_FF // tf
    return pl.pallas_call(
        _ffn_kernel,
        grid=(T // tm, nf),
        in_specs=[
            pl.BlockSpec((tm, D_MODEL), lambda i, j: (i, 0)),
            pl.BlockSpec((D_MODEL, tf), lambda i, j: (0, j)),
            pl.BlockSpec((D_MODEL, tf), lambda i, j: (0, j + nf)),
            pl.BlockSpec((tf, D_MODEL), lambda i, j: (j, 0)),
            pl.BlockSpec((1, D_MODEL), lambda i, j: (0, 0)),
            pl.BlockSpec((1, D_MODEL), lambda i, j: (0, 0)),
        ],
        out_specs=pl.BlockSpec((tm, D_MODEL), lambda i, j: (i, 0)),
        out_shape=jax.ShapeDtypeStruct((T, D_MODEL), F32),
        scratch_shapes=[pltpu.VMEM((tm, D_MODEL), BF16), pltpu.VMEM((tm, D_MODEL), F32)],
        compiler_params=_cparams("parallel", "arbitrary"),
        name="ffn_ln",
    )(x, w_in, w_in, w_out, gain.reshape(1, -1), bias.reshape(1, -1))


def _rope_chunk(x, c, sa, sb, half):
    return x * c + pltpu.roll(x, LANES - half, 1) * sa + pltpu.roll(x, half, 1) * sb


def _rms_norm(x, g):
    return x * lax.rsqrt(jnp.mean(x * x, axis=-1, keepdims=True) + RMS_EPS) * g


_DIL_Q_CHUNKS = DIL_HEADS
_DIL_KV_CHUNKS = DIL_HEADS * HEAD_DIM // LANES


def _proj_kernel(*refs):
    (x_ref, w_ref, qn_ref, kvn_ref, wuq_ref, wkn_ref, wv_ref,
     c64_ref, sa64_ref, sb64_ref, cm_ref, sam_ref, sbm_ref, qa_ref, ka_ref, va_ref) = refs[:16]
    dil_refs = refs[16:16 + 3 * len(DIL_PATTERNS)]
    (nq_ref, nqr_ref, kc_ref, vc_ref, ksl_ref, vsl_ref, kw_ref, vw_ref, gl_ref,
     stage_ref) = refs[16 + 3 * len(DIL_PATTERNS):]
    tm = x_ref.shape[0]
    xb = x_ref[...].astype(BF16)
    c64, sa64, sb64 = c64_ref[...], sa64_ref[...], sb64_ref[...]
    cm, sam, sbm = cm_ref[...], sam_ref[...], sbm_ref[...]

    def proj(off, width):
        return _dot(xb, w_ref[:, off:off + width])

    def rope64(v):
        return _rope_chunk(v, c64, sa64, sb64, HEAD_DIM // 2)

    def rope_mla(v):
        return _rope_chunk(v, cm, sam, sbm, MLA_ROPE // 2)

    cq = _rms_norm(proj(_C_CQ, MLA_Q_LORA), qn_ref[...]).astype(BF16)
    q_raw = _dot(cq, wuq_ref[...])
    ckv = _rms_norm(proj(_C_CKV, MLA_KV_LORA), kvn_ref[...]).astype(BF16)
    k_nope = _dot(ckv, wkn_ref[...])
    va_ref[0, 0] = _dot(ckv, wv_ref[...]).T.astype(BF16)
    kpe = proj(_C_KPE, MLA_HEADS * LANES)
    q_scale = (MLA_NOPE + MLA_ROPE) ** -0.5 * LOG2_E
    for h in range(MLA_HEADS):
        sl = slice(h * LANES, (h + 1) * LANES)
        qa_ref[:, sl] = (rope_mla(q_raw[:, sl]) * q_scale).astype(BF16)
        ka_ref[:, sl] = (k_nope[:, sl] + rope_mla(kpe[:, sl])).astype(BF16)

    scale = HEAD_DIM ** -0.5
    dq = proj(_C_DQ, DIL_HEADS * LANES)
    dk = proj(_C_DK, DIL_HEADS * HEAD_DIM)
    dv = proj(_C_DV, DIL_HEADS * HEAD_DIM)
    for c in range(_DIL_Q_CHUNKS):
        stage_ref[c] = rope64(dq[:, c * LANES:(c + 1) * LANES]) * (scale * LOG2_E)
    for c in range(_DIL_KV_CHUNKS):
        stage_ref[_DIL_Q_CHUNKS + c] = rope64(dk[:, c * LANES:(c + 1) * LANES])
        stage_ref[_DIL_Q_CHUNKS + _DIL_KV_CHUNKS + c] = dv[:, c * LANES:(c + 1) * LANES]
    for p, (_, r) in enumerate(DIL_PATTERNS):
        for first, chunks, o_ref in ((0, _DIL_Q_CHUNKS, dil_refs[3 * p]),
                                     (_DIL_Q_CHUNKS, _DIL_KV_CHUNKS, dil_refs[3 * p + 1]),
                                     (_DIL_Q_CHUNKS + _DIL_KV_CHUNKS, _DIL_KV_CHUNKS, dil_refs[3 * p + 2])):
            for j in range(r):
                for c in range(chunks):
                    rows = stage_ref[first + c, pl.ds(j, tm // r, stride=r), :]
                    o_ref[0, j, :, c * LANES:(c + 1) * LANES] = rows.astype(BF16)

    nq = proj(_C_NQ, NSA_HEADS * LANES)
    nq_ref[...] = (nq * scale).astype(BF16)
    for h in range(NSA_HEADS):
        sl = slice(h * LANES, (h + 1) * LANES)
        nqr_ref[:, sl] = (rope64(nq[:, sl]) * (scale * LOG2_E)).astype(BF16)
    kc_ref[...] = proj(_C_KC, LANES)
    vc_ref[...] = proj(_C_VC, LANES)
    ksl_ref[...] = rope64(proj(_C_KSL, LANES)).astype(BF16)
    vsl_ref[0, 0] = proj(_C_VSL, LANES).T.astype(BF16)
    kw_ref[...] = rope64(proj(_C_KW, LANES)).astype(BF16)
    vw_ref[...] = proj(_C_VW, LANES).astype(BF16)
    gl_ref[...] = jax.nn.sigmoid(proj(_C_GL, LANES))


def _proj_source_columns():
    src = -np.ones(_PROJ_WIDTH, np.int64)

    def put(dst, s0, n):
        src[dst:dst + n] = s0 + np.arange(n)

    put(_C_CQ, _SRC_CQ, MLA_Q_LORA)
    put(_C_CKV, _SRC_CKV, MLA_KV_LORA)
    for h in range(MLA_HEADS):
        put(_C_KPE + h * LANES + MLA_NOPE, _SRC_KPE, MLA_ROPE)
    for h in range(DIL_HEADS):
        put(_C_DQ + h * LANES + HEAD_DIM * (h % 2), _SRC_DQ + h * HEAD_DIM, HEAD_DIM)
    put(_C_DK, _SRC_DK, DIL_HEADS * HEAD_DIM)
    put(_C_DV, _SRC_DV, DIL_HEADS * HEAD_DIM)
    for h in range(NSA_HEADS):
        put(_C_NQ + h * LANES + HEAD_DIM * (h // NSA_HEADS_PER_GROUP), _SRC_NQ + h * HEAD_DIM, HEAD_DIM)
    for dst, s0 in ((_C_KC, _SRC_KC), (_C_VC, _SRC_VC), (_C_KSL, _SRC_KSL), (_C_VSL, _SRC_VSL),
                    (_C_KW, _SRC_KW), (_C_VW, _SRC_VW)):
        put(dst, s0, NSA_KV_GROUPS * HEAD_DIM)
    put(_C_GL, _SRC_GL, NSA_HEADS * 3)
    return src


def _gather_columns(w, src, axis):
    valid = jnp.asarray(src >= 0)
    taken = jnp.take(w, jnp.asarray(np.maximum(src, 0)), axis=axis)
    shape = [1, 1]
    shape[axis] = -1
    return jnp.where(valid.reshape(shape), taken, 0.0)


def _mla_weight_layout(w_uq, w_ukv):
    dq = MLA_NOPE + MLA_ROPE
    src_q = -np.ones(MLA_HEADS * LANES, np.int64)
    src_kn = -np.ones(MLA_HEADS * LANES, np.int64)
    src_v = np.zeros(MLA_HEADS * MLA_V, np.int64)
    for h in range(MLA_HEADS):
        src_q[h * LANES:h * LANES + dq] = h * dq + np.arange(dq)
        src_kn[h * LANES:h * LANES + MLA_NOPE] = h * (MLA_NOPE + MLA_V) + np.arange(MLA_NOPE)
        src_v[h * MLA_V:(h + 1) * MLA_V] = h * (MLA_NOPE + MLA_V) + MLA_NOPE + np.arange(MLA_V)
    return (_gather_columns(w_uq, src_q, 1).astype(BF16),
            _gather_columns(w_ukv, src_kn, 1).astype(BF16),
            _gather_columns(w_ukv, src_v, 1).astype(BF16))


def _rope_tables(S):
    def cos_sin(dim):
        inv_freq = ROPE_THETA ** (-jnp.arange(0, dim, 2, dtype=F32) / dim)
        ang = jnp.arange(S, dtype=F32)[:, None] * inv_freq[None, :]
        return jnp.cos(ang), jnp.sin(ang)

    cos, sin = cos_sin(HEAD_DIM)
    zero = jnp.zeros_like(sin)
    c64 = jnp.concatenate([cos, cos] * 2, axis=1)
    sa64 = jnp.concatenate([-sin, zero] * 2, axis=1)
    sb64 = jnp.concatenate([zero, sin] * 2, axis=1)
    cos_m, sin_m = cos_sin(MLA_ROPE)
    ones = jnp.ones((S, MLA_NOPE), F32)
    z64 = jnp.zeros((S, MLA_NOPE), F32)
    z16 = jnp.zeros_like(sin_m)
    z32 = jnp.zeros((S, LANES - MLA_NOPE - MLA_ROPE), F32)
    cm = jnp.concatenate([ones, cos_m, cos_m, z32], axis=1)
    sam = jnp.concatenate([z64, -sin_m, z16, z32], axis=1)
    sbm = jnp.concatenate([z64, z16, sin_m, z32], axis=1)
    return c64, sa64, sb64, cm, sam, sbm


def _input_projection(h, w_big, q_norm, kv_norm, wuq, wkn, wv, tables, S, *, tm=KV_TILE):
    T = h.shape[0]
    n_pos = S // tm
    row = lambda w: pl.BlockSpec((tm, w), lambda i: (i, 0))
    full = lambda a: pl.BlockSpec(a.shape, lambda i: (0, 0))
    tab = pl.BlockSpec((tm, LANES), lambda i: (i % n_pos, 0))
    B = T // S
    specs, shapes = [], []

    def rows_out(w, dt):
        specs.append(row(w))
        shapes.append(jax.ShapeDtypeStruct((T, w), dt))

    def tile_t_out(c):
        specs.append(pl.BlockSpec((1, 1, c, tm), lambda i: (i // n_pos, i % n_pos, 0, 0)))
        shapes.append(jax.ShapeDtypeStruct((B, n_pos, c, tm), BF16))

    def residue_out(r, w):
        specs.append(pl.BlockSpec((1, r, tm // r, w), lambda i: (i // n_pos, 0, i % n_pos, 0)))
        shapes.append(jax.ShapeDtypeStruct((B, r, S // r, w), BF16))

    rows_out(MLA_HEADS * LANES, BF16)
    rows_out(MLA_HEADS * LANES, BF16)
    tile_t_out(MLA_HEADS * MLA_V)
    for _, r in DIL_PATTERNS:
        residue_out(r, DIL_HEADS * LANES)
        residue_out(r, DIL_HEADS * HEAD_DIM)
        residue_out(r, DIL_HEADS * HEAD_DIM)
    rows_out(NSA_HEADS * LANES, BF16)
    rows_out(NSA_HEADS * LANES, BF16)
    rows_out(LANES, F32)
    rows_out(LANES, F32)
    rows_out(LANES, BF16)
    tile_t_out(LANES)
    rows_out(LANES, BF16)
    rows_out(LANES, BF16)
    rows_out(LANES, F32)
    qn = q_norm.reshape(1, -1)
    kvn = kv_norm.reshape(1, -1)
    return pl.pallas_call(
        _proj_kernel,
        grid=(T // tm,),
        in_specs=[row(D_MODEL), full(w_big), full(qn), full(kvn), full(wuq), full(wkn), full(wv)]
                 + [tab] * 6,
        out_specs=specs,
        out_shape=shapes,
        scratch_shapes=[pltpu.VMEM((_DIL_Q_CHUNKS + 2 * _DIL_KV_CHUNKS, tm, LANES), F32)],
        compiler_params=_cparams("parallel"),
        name="in_proj",
    )(h, w_big, qn, kvn, wuq, wkn, wv, *tables)


ONES_ROWS = 16


def _flash_step_t(s, vt1, m_ref, acc_ref, idx):
    m_prev = m_ref[idx]
    m_new = jnp.maximum(m_prev, jnp.max(s, axis=0, keepdims=True))
    alpha = jnp.exp2(m_prev - m_new)
    p = jnp.exp2(s - m_new).astype(BF16)
    acc_ref[idx] = alpha * acc_ref[idx] + _dot(vt1, p)
    m_ref[idx] = m_new


def _init_softmax_state(m_ref, acc_ref):
    m_ref[...] = jnp.full(m_ref.shape, NEG_INF, F32)
    acc_ref[...] = jnp.zeros_like(acc_ref)


def _flash_state(slots, channels, queries):
    return [pltpu.VMEM((slots, 1, queries), F32), pltpu.VMEM((slots, channels + ONES_ROWS, queries), F32)]


def _with_ones_rows(vt):
    return jnp.concatenate([vt, jnp.ones((ONES_ROWS, vt.shape[1]), vt.dtype)], axis=0)


def _normalised(acc, channels):
    return acc[:channels] / acc[channels:channels + 1]


def _mla_kernel(q_ref, k_ref, vt_ref, o_ref, m_ref, acc_ref, *, tq, tk):
    q0 = pl.program_id(1) * tq
    _init_softmax_state(m_ref, acc_ref)
    kofs = lax.broadcasted_iota(jnp.int32, (tk, tq), 0)
    qpos = q0 + lax.broadcasted_iota(jnp.int32, (tk, tq), 1)

    def tile(kt, masked):
        start = pl.multiple_of(kt * tk, tk)

        def scores(h):
            hsl = slice(h * LANES, (h + 1) * LANES)
            s = _dot_nt(k_ref[0, pl.ds(start, tk), hsl], q_ref[0, :, hsl])
            return jnp.where((start + kofs) <= qpos, s, NEG_INF) if masked else s

        s_next = scores(0)
        for h in range(MLA_HEADS):
            s = s_next
            if h + 1 < MLA_HEADS:
                s_next = scores(h + 1)
            vt = vt_ref[0, kt, (h // 2) * LANES:(h // 2 + 1) * LANES, :]
            _flash_step_t(s, _with_ones_rows(vt), m_ref, acc_ref, h)

    def body(kt, carry):
        tile(kt, False)
        return carry

    n_full = q0 // tk
    lax.fori_loop(0, n_full, body, 0)
    for d in range(tq // tk):
        tile(n_full + d, True)
    even_rows = lax.broadcasted_iota(jnp.int32, (LANES, tq), 0) < MLA_V
    for pair in range(MLA_HEADS // 2):
        h0, h1 = 2 * pair, 2 * pair + 1
        out_t = jnp.where(even_rows, _normalised(acc_ref[h0], LANES), _normalised(acc_ref[h1], LANES))
        o_ref[0, :, pair * LANES:(pair + 1) * LANES] = out_t.T.astype(o_ref.dtype)


def _mla_attention(q, k, vt, *, tq=1024, tk=KV_TILE):
    B, S, _ = q.shape
    return pl.pallas_call(
        functools.partial(_mla_kernel, tq=tq, tk=tk),
        grid=(B, S // tq),
        in_specs=[
            pl.BlockSpec((1, tq, MLA_HEADS * LANES), lambda b, i: (b, i, 0)),
            pl.BlockSpec((1, S, MLA_HEADS * LANES), lambda b, i: (b, 0, 0)),
            pl.BlockSpec((1, S // tk, MLA_HEADS * MLA_V, tk), lambda b, i: (b, 0, 0, 0)),
        ],
        out_specs=pl.BlockSpec((1, tq, MLA_HEADS * MLA_V), lambda b, i: (b, i, 0)),
        out_shape=jax.ShapeDtypeStruct((B, S, MLA_HEADS * MLA_V), BF16),
        scratch_shapes=_flash_state(MLA_HEADS, LANES, tq),
        compiler_params=_cparams("parallel", "arbitrary"),
        name="mla_attn",
    )(q, k, vt)


def _dot_tn(a, b):
    return lax.dot_general(a, b, (((0,), (0,)), ((), ())), preferred_element_type=F32)


def _band_mask_t(q0, kstart, tq, span, window):
    kpos = kstart + lax.broadcasted_iota(jnp.int32, (span, tq), 0)
    qpos = q0 + lax.broadcasted_iota(jnp.int32, (span, tq), 1)
    dist = qpos - kpos
    return (dist >= 0) & (dist <= window)


def _dil_kernel(q_ref, k_ref, v_ref, o_ref, lse_ref, *, tq, sub, wpad, window):
    q0 = pl.program_id(2) * tq
    n = k_ref.shape[2]
    span = min(sub + wpad, n)
    even_rows = lax.broadcasted_iota(jnp.int32, (LANES, sub), 0) < HEAD_DIM
    windows = []
    for a in range(tq // sub):
        qs = q0 + a * sub
        kstart = pl.multiple_of(jnp.clip(qs - wpad, 0, n - span), LANES)
        windows.append((kstart, _band_mask_t(qs, kstart, sub, span, window)))
    chains = [(a, c, hh) for a in range(tq // sub) for c in range(DIL_HEADS // 2) for hh in range(2)]

    def scores(a, c, hh):
        kstart, mask = windows[a]
        h = 2 * c + hh
        k = k_ref[0, 0, pl.ds(kstart, span), c * LANES:(c + 1) * LANES]
        q = q_ref[0, 0, a * sub:(a + 1) * sub, h * LANES:(h + 1) * LANES]
        return jnp.where(mask, _dot_nt(k, q), NEG_INF)

    s_next = scores(*chains[0])
    outs, lses = [], []
    for i, (a, c, hh) in enumerate(chains):
        s = s_next
        if i + 1 < len(chains):
            s_next = scores(*chains[i + 1])
        m = jnp.max(s, axis=0, keepdims=True)
        e = jnp.exp2(s - m)
        l = jnp.sum(e, axis=0, keepdims=True)
        v = v_ref[0, 0, pl.ds(windows[a][0], span), c * LANES:(c + 1) * LANES]
        outs.append(_dot_tn(v, e.astype(BF16)) * (1.0 / l))
        lses.append(jnp.broadcast_to(m * LN_2 + jnp.log(l), (LANES, sub)))
        if hh == 1:
            rows, csl = slice(a * sub, (a + 1) * sub), slice(c * LANES, (c + 1) * LANES)
            o_ref[0, 0, rows, csl] = jnp.where(even_rows, outs[-2], outs[-1]).T
            lse_ref[0, 0, rows, csl] = jnp.where(even_rows, lses[-2], lses[-1]).T


def _dilated_pattern(q, k, v, window, dil, *, tq=512, sub=256):
    B, _, n, wq = q.shape
    wk = k.shape[-1]
    wsub = window // dil
    wpad = -(-wsub // LANES) * LANES
    tq = min(tq, n)
    sub = min(sub, tq)
    return pl.pallas_call(
        functools.partial(_dil_kernel, tq=tq, sub=sub, wpad=wpad, window=wsub),
        grid=(B, dil, n // tq),
        in_specs=[
            pl.BlockSpec((1, 1, tq, wq), lambda b, j, i: (b, j, i, 0)),
            pl.BlockSpec((1, 1, n, wk), lambda b, j, i: (b, j, 0, 0)),
            pl.BlockSpec((1, 1, n, wk), lambda b, j, i: (b, j, 0, 0)),
        ],
        out_specs=[pl.BlockSpec((1, 1, tq, wk), lambda b, j, i: (b, j, i, 0))] * 2,
        out_shape=[jax.ShapeDtypeStruct((B, dil, n, wk), F32)] * 2,
        compiler_params=_cparams("parallel", "parallel", "arbitrary"),
        name=f"dilated_r{dil}",
    )(q, k, v)


def _compress_kernel(xk_ref, xv_ref, pos_ref, w1_ref, w2_ref, kc_ref, vct_ref):
    st = NSA_CMP_STRIDE
    n = xk_ref.shape[1] // st
    for t, x_ref in enumerate((xk_ref, xv_ref)):
        out = jnp.zeros((n, LANES), F32)
        for g in range(NSA_KV_GROUPS):
            first = jnp.zeros((n, NSA_CMP_HIDDEN), F32)
            second = jnp.zeros((n, NSA_CMP_HIDDEN), F32)
            for l in range(st):
                x = x_ref[0, pl.ds(l, n, stride=st), :]
                first = first + _dot((x + pos_ref[t, l]).astype(BF16), w1_ref[t, g, l])
                second = second + _dot((x + pos_ref[t, st + l]).astype(BF16), w1_ref[t, g, st + l])
            hid = first + pltpu.roll(second, n - 1, 0)
            hid = (hid * jax.nn.sigmoid(hid)).astype(BF16)
            out = out + _dot(hid, w2_ref[t, g])
        if t == 0:
            kc_ref[0] = out.astype(kc_ref.dtype)
        else:
            vct_ref[0] = out.T.astype(vct_ref.dtype)


def _nsa_compress(kc, vc, cmp_pos, cmp_w1, cmp_w2):
    B, S, _ = kc.shape
    G, d = NSA_KV_GROUPS, HEAD_DIM
    n = S // NSA_CMP_STRIDE
    w1 = cmp_w1.reshape(2, NSA_CMP_BLOCK, d, NSA_CMP_HIDDEN)
    zero1 = jnp.zeros_like(w1)
    w1g = jnp.stack([jnp.concatenate([w1, zero1], axis=2), jnp.concatenate([zero1, w1], axis=2)], axis=1)
    zero2 = jnp.zeros_like(cmp_w2)
    w2g = jnp.stack([jnp.concatenate([cmp_w2, zero2], axis=2), jnp.concatenate([zero2, cmp_w2], axis=2)], axis=1)
    pos = jnp.concatenate([cmp_pos] * G, axis=-1).reshape(2, NSA_CMP_BLOCK, 1, G * d)
    w1g, w2g = w1g.astype(BF16), w2g.astype(BF16)
    xspec = pl.BlockSpec((1, S, G * d), lambda b: (b, 0, 0))
    full = lambda a: pl.BlockSpec(a.shape, lambda b: (0,) * a.ndim)
    return pl.pallas_call(
        _compress_kernel,
        grid=(B,),
        in_specs=[xspec, xspec, full(pos), full(w1g), full(w2g)],
        out_specs=[pl.BlockSpec((1, n, G * d), lambda b: (b, 0, 0)),
                   pl.BlockSpec((1, G * d, n), lambda b: (b, 0, 0))],
        out_shape=[jax.ShapeDtypeStruct((B, n, G * d), BF16), jax.ShapeDtypeStruct((B, G * d, n), BF16)],
        compiler_params=_cparams("parallel"),
        name="nsa_compress",
    )(kc, vc, pos, w1g, w2g)


def _dot_01_by_f32(o01, p):
    p1 = p.astype(BF16)
    r1 = p - p1.astype(F32)
    p2 = r1.astype(BF16)
    p3 = (r1 - p2.astype(F32)).astype(BF16)
    return _dot(o01, p1) + _dot(o01, p2) + _dot(o01, p3)


SUBLANES = 8


def _count_outranking(sc_ref, cnt_ref, n_sources, tq):
    groups = SEL_LANES // SUBLANES
    sub = lax.broadcasted_iota(jnp.int32, (SUBLANES, tq), 0)
    cnt_ref[...] = jnp.zeros_like(cnt_ref)
    for bi in range(groups):
        @pl.when(bi * SUBLANES < n_sources)
        def _():
            src = sc_ref[bi * SUBLANES:(bi + 1) * SUBLANES, :]
            rows = [jnp.broadcast_to(src[ii:ii + 1, :], (SUBLANES, tq)) for ii in range(SUBLANES)]
            for r in range(groups):
                rsl = slice(r * SUBLANES, (r + 1) * SUBLANES)
                tgt = sc_ref[rsl, :]
                cnt = cnt_ref[rsl, :]
                for ii, row in enumerate(rows):
                    if r > bi:
                        ahead = row >= tgt
                    elif r < bi:
                        ahead = row > tgt
                    else:
                        ahead = (row > tgt) | ((row == tgt) & (sub > ii))
                    cnt = cnt + jnp.where(ahead, 1.0, 0.0)
                cnt_ref[rsl, :] = cnt


def _cmp_kernel(q_ref, kc_ref, vct_ref, ovt_ref, o_ref, bias_ref, sc_ref, cnt_ref, *, tq):
    g = pl.program_id(1)
    q0 = pl.program_id(2) * tq
    n = kc_ref.shape[1]
    kc = kc_ref[0]
    vct = vct_ref[0]
    qpos = q0 + lax.broadcasted_iota(jnp.int32, (n, tq), 1)
    cmp_end = lax.broadcasted_iota(jnp.int32, (n, tq), 0) * NSA_CMP_STRIDE + (NSA_CMP_BLOCK - 1)
    visible = cmp_end <= qpos
    in_group = _lane_half((tq, LANES)) == g
    psum = jnp.zeros((n, tq), F32)

    def scores(hh):
        return jnp.where(visible, _dot_nt(kc, q_ref[0, :, hh * LANES:(hh + 1) * LANES]), NEG_INF)

    s_next = scores(0)
    for hh in range(NSA_HEADS_PER_GROUP):
        s = s_next
        if hh + 1 < NSA_HEADS_PER_GROUP:
            s_next = scores(hh + 1)
        m = jnp.max(s, axis=0, keepdims=True)
        e = jnp.where(visible, jnp.exp(s - m), 0.0)
        l = jnp.maximum(jnp.sum(e, axis=0, keepdims=True), 1e-30)
        p = e * (1.0 / l)
        o = _dot(vct, p.astype(BF16)).T
        o_ref[0, :, hh * LANES:(hh + 1) * LANES] = jnp.where(in_group, o, 0.0).astype(o_ref.dtype)
        psum = psum + p

    imp_t = _dot_01_by_f32(ovt_ref[...], psum)
    blk = lax.broadcasted_iota(jnp.int32, (SEL_LANES, tq), 0)
    cur = lax.shift_right_logical(q0 + lax.broadcasted_iota(jnp.int32, (SEL_LANES, tq), 1), 6)
    forced = (blk == 0) | (blk == cur) | (blk == cur - 1)
    sc_ref[...] = jnp.where(forced, FORCE_SCORE, jnp.where(blk <= cur, imp_t, -FORCE_SCORE))
    n_causal = jnp.minimum(lax.shift_right_logical(q0 + tq - 1, 6) + 1, SEL_LANES)
    _count_outranking(sc_ref, cnt_ref, n_causal, tq)
    bias_t = jnp.where(cnt_ref[...] < float(NSA_TOP_N), 0.0, UNSELECTED_BIAS)
    bias_ref[0, 0] = bias_t.T.astype(bias_ref.dtype)


def _nsa_compressed_branch(nq, kcc, vcct, overlap, *, tq=256):
    B, S, _ = nq.shape
    G = NSA_KV_GROUPS
    n = kcc.shape[1]
    wq = NSA_HEADS_PER_GROUP * LANES
    return pl.pallas_call(
        functools.partial(_cmp_kernel, tq=tq),
        grid=(B, G, S // tq),
        in_specs=[
            pl.BlockSpec((1, tq, wq), lambda b, g, i: (b, i, g)),
            pl.BlockSpec((1, n, LANES), lambda b, g, i: (b, 0, 0)),
            pl.BlockSpec((1, LANES, n), lambda b, g, i: (b, 0, 0)),
            pl.BlockSpec((SEL_LANES, n), lambda b, g, i: (0, 0)),
        ],
        out_specs=[
            pl.BlockSpec((1, tq, wq), lambda b, g, i: (b, i, g)),
            pl.BlockSpec((1, 1, tq, SEL_LANES), lambda b, g, i: (b, g, i, 0)),
        ],
        out_shape=[jax.ShapeDtypeStruct((B, S, NSA_HEADS * LANES), BF16),
                   jax.ShapeDtypeStruct((B, G, S, SEL_LANES), BF16)],
        scratch_shapes=[pltpu.VMEM((SEL_LANES, tq), F32), pltpu.VMEM((SEL_LANES, tq), F32)],
        compiler_params=_cparams("parallel", "parallel", "arbitrary"),
        name="nsa_cmp_topk",
    )(nq, kcc, vcct, overlap.T)


def _sel_kernel(q_ref, bias_ref, k_ref, vt_ref, oh_ref, o_ref, qa_ref, m_ref, acc_ref, *, tq, tk):
    q0 = pl.program_id(1) * tq
    for h in range(NSA_HEADS):
        qa_ref[h, :, :LANES] = q_ref[0, :, h * LANES:(h + 1) * LANES]
        qa_ref[h, :, LANES:] = bias_ref[0, h // NSA_HEADS_PER_GROUP]
    kofs = lax.broadcasted_iota(jnp.int32, (tk, tq), 0)
    qpos = q0 + lax.broadcasted_iota(jnp.int32, (tk, tq), 1)
    _init_softmax_state(m_ref, acc_ref)

    def tile(kt, masked):
        start = pl.multiple_of(kt * tk, tk)
        ka = jnp.concatenate([k_ref[0, pl.ds(start, tk), :], oh_ref[pl.ds(start, tk), :]], axis=1)
        vt1 = _with_ones_rows(vt_ref[0, kt])

        def scores(h):
            s = _dot_nt(ka, qa_ref[h])
            return jnp.where((start + kofs) <= qpos, s, NEG_INF) if masked else s

        s_next = scores(0)
        for h in range(NSA_HEADS):
            s = s_next
            if h + 1 < NSA_HEADS:
                s_next = scores(h + 1)
            _flash_step_t(s, vt1, m_ref, acc_ref, h)

    def body(kt, carry):
        tile(kt, False)
        return carry

    n_full = q0 // tk
    lax.fori_loop(0, n_full, body, 0)
    tile(n_full, True)
    half = _lane_half((tq, LANES))
    for h in range(NSA_HEADS):
        out = _normalised(acc_ref[h], LANES).T
        o_ref[0, :, h * LANES:(h + 1) * LANES] = jnp.where(
            half == h // NSA_HEADS_PER_GROUP, out, 0.0).astype(o_ref.dtype)


def _nsa_selected_branch(nqr, bias, ksl, vslt, onehot, *, tq=512, tk=KV_TILE):
    B, S, wq = nqr.shape
    G = NSA_KV_GROUPS
    return pl.pallas_call(
        functools.partial(_sel_kernel, tq=tq, tk=tk),
        grid=(B, S // tq),
        in_specs=[
            pl.BlockSpec((1, tq, wq), lambda b, i: (b, i, 0)),
            pl.BlockSpec((1, G, tq, SEL_LANES), lambda b, i: (b, 0, i, 0)),
            pl.BlockSpec((1, S, LANES), lambda b, i: (b, 0, 0)),
            pl.BlockSpec((1, S // tk, LANES, tk), lambda b, i: (b, 0, 0, 0)),
            pl.BlockSpec((S, SEL_LANES), lambda b, i: (0, 0)),
        ],
        out_specs=pl.BlockSpec((1, tq, wq), lambda b, i: (b, i, 0)),
        out_shape=jax.ShapeDtypeStruct((B, S, wq), BF16),
        scratch_shapes=[pltpu.VMEM((NSA_HEADS, tq, LANES + SEL_LANES), BF16)]
                       + _flash_state(NSA_HEADS, LANES, tq),
        compiler_params=_cparams("parallel", "arbitrary"),
        name="nsa_selected",
    )(nqr, bias, ksl, vslt, onehot)


def _win_kernel(q_ref, k_ref, v_ref, o_ref, *, tq, wpad, window):
    q0 = pl.program_id(1) * tq
    span = tq + wpad
    kstart = pl.multiple_of(jnp.maximum(q0 - wpad, 0), LANES)
    mask = _band_mask_t(q0, kstart, tq, span, window)
    k = k_ref[0, pl.ds(kstart, span), :]
    v = v_ref[0, pl.ds(kstart, span), :]
    half = _lane_half((tq, LANES))

    def scores(h):
        return jnp.where(mask, _dot_nt(k, q_ref[0, :, h * LANES:(h + 1) * LANES]), NEG_INF)

    s_next = scores(0)
    for h in range(NSA_HEADS):
        s = s_next
        if h + 1 < NSA_HEADS:
            s_next = scores(h + 1)
        m = jnp.max(s, axis=0, keepdims=True)
        e = jnp.exp2(s - m)
        l = jnp.sum(e, axis=0, keepdims=True)
        o = (_dot_tn(v, e.astype(BF16)) * (1.0 / l)).T
        o_ref[0, :, h * LANES:(h + 1) * LANES] = jnp.where(
            half == h // NSA_HEADS_PER_GROUP, o, 0.0).astype(o_ref.dtype)


def _nsa_window_branch(nqr, kw, vw, *, tq=256):
    B, S, wq = nqr.shape
    wpad = -(-NSA_WINDOW // LANES) * LANES
    return pl.pallas_call(
        functools.partial(_win_kernel, tq=tq, wpad=wpad, window=NSA_WINDOW),
        grid=(B, S // tq),
        in_specs=[
            pl.BlockSpec((1, tq, wq), lambda b, i: (b, i, 0)),
            pl.BlockSpec((1, S, LANES), lambda b, i: (b, 0, 0)),
            pl.BlockSpec((1, S, LANES), lambda b, i: (b, 0, 0)),
        ],
        out_specs=pl.BlockSpec((1, tq, wq), lambda b, i: (b, i, 0)),
        out_shape=jax.ShapeDtypeStruct((B, S, wq), BF16),
        compiler_params=_cparams("parallel", "arbitrary"),
        name="nsa_window",
    )(nqr, kw, vw)


def _selection_constants(S):
    n_cmp_rows = S // NSA_CMP_STRIDE
    c = np.arange(n_cmp_rows)
    cmp_start = c * NSA_CMP_STRIDE
    cmp_end = cmp_start + NSA_CMP_BLOCK - 1
    sel_start = np.arange(SEL_LANES) * NSA_SEL_BLOCK
    overlap = ((cmp_start[:, None] < sel_start[None, :] + NSA_SEL_BLOCK) &
               (cmp_end[:, None] >= sel_start[None, :]))
    n_cmp = (S - NSA_CMP_BLOCK) // NSA_CMP_STRIDE + 1
    overlap &= (c < n_cmp)[:, None]
    onehot = (np.arange(S)[:, None] // NSA_SEL_BLOCK) == np.arange(SEL_LANES)[None, :]
    return jnp.asarray(overlap, BF16), jnp.asarray(onehot, BF16)


def _out_kernel(*refs):
    n_pat = len(DIL_PATTERNS)
    x_ref, oa_ref = refs[:2]
    dil_refs = refs[2:2 + 2 * n_pat]
    (oc_ref, os_ref, ow_ref, gl_ref, wa_ref, wb_ref, wc_ref, g_ref, b_ref, o_ref,
     nat_ref) = refs[2 + 2 * n_pat:]
    tm = x_ref.shape[0]
    mix = _dot(oa_ref[...], wa_ref[...])

    for p, (_, r) in enumerate(DIL_PATTERNS):
        for a in range(2):
            for j in range(r):
                for c in range(_DIL_KV_CHUNKS):
                    nat_ref[2 * p + a, c, pl.ds(j, tm // r, stride=r), :] = (
                        dil_refs[2 * p + a][0, j, :, c * LANES:(c + 1) * LANES])
    for c in range(_DIL_KV_CHUNKS):
        outs = [nat_ref[2 * p, c] for p in range(n_pat)]
        lses = [nat_ref[2 * p + 1, c] for p in range(n_pat)]
        mx = functools.reduce(jnp.maximum, lses)
        es = [jnp.exp(l - mx) for l in lses]
        inv = 1.0 / functools.reduce(jnp.add, es)
        ob = functools.reduce(jnp.add, [(e * inv) * o for e, o in zip(es, outs)])
        mix = mix + _dot(ob.astype(BF16), wb_ref[c * LANES:(c + 1) * LANES, :])

    gl = gl_ref[...]
    for h in range(NSA_HEADS):
        sl = slice(h * LANES, (h + 1) * LANES)
        oc = (gl[:, 3 * h:3 * h + 1] * oc_ref[:, sl].astype(F32)
              + gl[:, 3 * h + 1:3 * h + 2] * os_ref[:, sl].astype(F32)
              + gl[:, 3 * h + 2:3 * h + 3] * ow_ref[:, sl].astype(F32))
        mix = mix + _dot(oc.astype(BF16), wc_ref[sl, :])

    y = ALPHA * x_ref[...] + mix
    o_ref[...] = _layer_norm(y, g_ref[...], b_ref[...])


def _out_weight_layout(w_out):
    na = MLA_HEADS * MLA_V
    nb = DIL_HEADS * HEAD_DIM
    src = -np.ones(NSA_HEADS * LANES, np.int64)
    for h in range(NSA_HEADS):
        d0 = h * LANES + HEAD_DIM * (h // NSA_HEADS_PER_GROUP)
        src[d0:d0 + HEAD_DIM] = na + nb + h * HEAD_DIM + np.arange(HEAD_DIM)
    return (w_out[:na].astype(BF16), w_out[na:na + nb].astype(BF16),
            _gather_columns(w_out, src, 0).astype(BF16))


def _output_projection(x, oa, dil, oc, osl, ow, gl, wa, wb, wc, gain, bias, S, *, tm=512):
    T = x.shape[0]
    n_pos = S // tm
    row = lambda a: pl.BlockSpec((tm, a.shape[1]), lambda i: (i, 0))
    full = lambda a: pl.BlockSpec(a.shape, lambda i: (0, 0))
    residue = lambda a: pl.BlockSpec((1, a.shape[1], tm // a.shape[1], a.shape[3]),
                                     lambda i: (i // n_pos, 0, i % n_pos, 0))
    g2, b2 = gain.reshape(1, -1), bias.reshape(1, -1)
    dil_flat = [a for pair in dil for a in pair]
    rows = [oc, osl, ow, gl]
    consts = [wa, wb, wc, g2, b2]
    return pl.pallas_call(
        _out_kernel,
        grid=(T // tm,),
        in_specs=[row(x), row(oa)] + [residue(a) for a in dil_flat] + [row(a) for a in rows]
                 + [full(a) for a in consts],
        out_specs=pl.BlockSpec((tm, D_MODEL), lambda i: (i, 0)),
        out_shape=jax.ShapeDtypeStruct((T, D_MODEL), F32),
        scratch_shapes=[pltpu.VMEM((len(dil_flat), _DIL_KV_CHUNKS, tm, LANES), F32)],
        compiler_params=_cparams("parallel"),
        name="out_proj_ln",
    )(x, oa, *dil_flat, *rows, *consts)


def kernel(x, ffn_w_in, ffn_w_out, ln_gain, ln_bias, w_in, w_out, mla_q_norm, mla_kv_norm,
           mla_w_uq, mla_w_ukv, nsa_cmp_pos, nsa_cmp_w1, nsa_cmp_w2):
    B, S, D = x.shape
    assert D == D_MODEL and S % 2048 == 0 and S // NSA_SEL_BLOCK <= SEL_LANES
    T = B * S
    tables = _rope_tables(S)
    overlap, onehot = _selection_constants(S)
    proj_src = _proj_source_columns()
    bs = lambda t: t.reshape(B, S, t.shape[-1])

    xf = x.reshape(T, D)
    for l in range(DEPTH):
        xf = _ffn_ln(xf, ffn_w_in[l, 0].astype(BF16), ffn_w_out[l, 0].astype(BF16),
                     ln_gain[l, 0], ln_bias[l, 0])

        w_big = _gather_columns(w_in[l], proj_src, 1).astype(BF16)
        wuq, wkn, wv = _mla_weight_layout(mla_w_uq[l], mla_w_ukv[l])
        outs = _input_projection(xf, w_big, mla_q_norm[l], mla_kv_norm[l], wuq, wkn, wv, tables, S)
        qa, ka, va = outs[:3]
        n_dil = 3 * len(DIL_PATTERNS)
        dqkv = outs[3:3 + n_dil]
        nq, nqr, kc, vc, ksl, vsl, kw, vw, gl = outs[3 + n_dil:]

        oa = _mla_attention(bs(qa), bs(ka), va).reshape(T, -1)
        dil = [_dilated_pattern(*dqkv[3 * p:3 * p + 3], window, r)
               for p, (window, r) in enumerate(DIL_PATTERNS)]
        kcc, vcc = _nsa_compress(bs(kc), bs(vc), nsa_cmp_pos[l], nsa_cmp_w1[l], nsa_cmp_w2[l])
        oc, sel_bias = _nsa_compressed_branch(bs(nq), kcc, vcc, overlap)
        osl = _nsa_selected_branch(bs(nqr), sel_bias, bs(ksl), vsl, onehot)
        ow = _nsa_window_branch(bs(nqr), bs(kw), bs(vw))

        wa, wb, wc = _out_weight_layout(w_out[l])
        xf = _output_projection(xf, oa, dil, oc.reshape(T, -1), osl.reshape(T, -1), ow.reshape(T, -1),
                                gl, wa, wb, wc, ln_gain[l, 1], ln_bias[l, 1], S)

        xf = _ffn_ln(xf, ffn_w_in[l, 1].astype(BF16), ffn_w_out[l, 1].astype(BF16),
                     ln_gain[l, 2], ln_bias[l, 2])
    return xf.reshape(B, S, D)
```

```python
import functools

import numpy as np
import jax
import jax.numpy as jnp
from jax import lax
from jax.experimental import pallas as pl
from jax.experimental.pallas import tpu as pltpu

F32 = jnp.float32
BF16 = jnp.bfloat16

D_MODEL = 1024
DEPTH = 2
HEAD_DIM = 64
MLA_HEADS = 4
MLA_Q_LORA = 256
MLA_KV_LORA = 128
MLA_NOPE = 64
MLA_ROPE = 32
MLA_V = 64
DIL_HEADS = 6
DIL_PATTERNS = ((128, 1), (512, 4), (2048, 16))
NSA_HEADS = 6
NSA_KV_GROUPS = 2
NSA_HEADS_PER_GROUP = NSA_HEADS // NSA_KV_GROUPS
NSA_CMP_BLOCK = 32
NSA_CMP_STRIDE = 16
NSA_CMP_HIDDEN = 256
NSA_SEL_BLOCK = 64
NSA_TOP_N = 16
NSA_WINDOW = 512
D_FF = ((8 * D_MODEL // 3 + 255) // 256) * 256
ROPE_THETA = 10000.0
LN_EPS = 1e-5
RMS_EPS = 1e-6
NEG_INF = -1e30
FORCE_SCORE = 1e4
ALPHA = (2 * DEPTH) ** 0.25
LOG2_E = 1.4426950408889634
LN_2 = 0.6931471805599453

LANES = 128
SEL_LANES = 128
UNSELECTED_BIAS = -1e9
VMEM_LIMIT = 56 * 1024 * 1024
KV_TILE = 512

_SRC_CQ, _SRC_CKV, _SRC_KPE = 0, 256, 384
_SRC_DQ, _SRC_DK, _SRC_DV = 416, 800, 1184
_SRC_NQ = 1568
_SRC_KC, _SRC_VC, _SRC_KSL, _SRC_VSL, _SRC_KW, _SRC_VW = 1952, 2080, 2208, 2336, 2464, 2592
_SRC_GL = 2720
_C_CQ, _C_CKV, _C_KPE, _C_DQ, _C_DK, _C_DV, _C_NQ = 0, 256, 384, 512, 896, 1280, 1664
_C_KC, _C_VC, _C_KSL, _C_VSL, _C_KW, _C_VW, _C_GL = 2048, 2176, 2304, 2432, 2560, 2688, 2816
_PROJ_WIDTH = 2944


def _cparams(*sem):
    return pltpu.CompilerParams(dimension_semantics=sem, vmem_limit_bytes=VMEM_LIMIT)


def _layer_norm(y, g, b):
    mu = jnp.mean(y, axis=-1, keepdims=True)
    d = y - mu
    var = jnp.mean(d * d, axis=-1, keepdims=True)
    return d * lax.rsqrt(var + LN_EPS) * g + b


def _dot(a, b):
    return jnp.dot(a, b, preferred_element_type=F32)


def _dot_nt(a, b):
    return lax.dot_general(a, b, (((1,), (1,)), ((), ())), preferred_element_type=F32)


def _lane_half(shape):
    return lax.shift_right_logical(lax.broadcasted_iota(jnp.int32, shape, len(shape) - 1), 6)


def _ffn_kernel(x_ref, wg_ref, wu_ref, wo_ref, g_ref, b_ref, o_ref):
    x = x_ref[...]
    xb = x.astype(BF16)
    gate = _dot(xb, wg_ref[...])
    up = _dot(xb, wu_ref[...])
    h = (gate * jax.nn.sigmoid(gate) * up).astype(BF16)
    y = ALPHA * x + 0.5 * _dot(h, wo_ref[...])
    o_ref[...] = _layer_norm(y, g_ref[...], b_ref[...])


def _ffn_ln(x, w_in, w_out, gain, bias, *, tm=512):
    T = x.shape[0]
    resident = lambda shape, idx: pl.BlockSpec(shape, idx, pipeline_mode=pl.Buffered(1))
    return pl.pallas_call(
        _ffn_kernel,
        grid=(T // tm,),
        in_specs=[
            pl.BlockSpec((tm, D_MODEL), lambda i: (i, 0)),
            resident((D_MODEL, D_FF), lambda i: (0, 0)),
            resident((D_MODEL, D_FF), lambda i: (0, 1)),
            resident((D_FF, D_MODEL), lambda i: (0, 0)),
            pl.BlockSpec((1, D_MODEL), lambda i: (0, 0)),
            pl.BlockSpec((1, D_MODEL), lambda i: (0, 0)),
        ],
        out_specs=pl.BlockSpec((tm, D_MODEL), lambda i: (i, 0)),
        out_shape=jax.ShapeDtypeStruct((T, D_MODEL), F32),
        compiler_params=_cparams("parallel"),
        name="ffn_ln",
    )(x, w_in, w_in, w_out, gain.reshape(1, -1), bias.reshape(1, -1))


def _rope_chunk(x, c, sa, sb, half):
    return x * c + pltpu.roll(x, LANES - half, 1) * sa + pltpu.roll(x, half, 1) * sb


def _rms_norm(x, g):
    return x * lax.rsqrt(jnp.mean(x * x, axis=-1, keepdims=True) + RMS_EPS) * g


_DIL_Q_CHUNKS = DIL_HEADS
_DIL_KV_CHUNKS = DIL_HEADS * HEAD_DIM // LANES


def _proj_kernel(*refs):
    (x_ref, w_ref, qn_ref, kvn_ref, wuq_ref, wkn_ref, wv_ref,
     c64_ref, sa64_ref, sb64_ref, cm_ref, sam_ref, sbm_ref, qa_ref, ka_ref, va_ref) = refs[:16]
    dil_refs = refs[16:16 + 3 * len(DIL_PATTERNS)]
    (nq_ref, nqr_ref, kc_ref, vc_ref, ksl_ref, vsl_ref, kw_ref, vw_ref, gl_ref,
     stage_ref) = refs[16 + 3 * len(DIL_PATTERNS):]
    tm = x_ref.shape[0]
    xb = x_ref[...].astype(BF16)
    c64, sa64, sb64 = c64_ref[...], sa64_ref[...], sb64_ref[...]
    cm, sam, sbm = cm_ref[...], sam_ref[...], sbm_ref[...]

    y = _dot(xb, w_ref[...])

    def proj(off, width):
        return y[:, off:off + width]

    def rope64(v):
        return _rope_chunk(v, c64, sa64, sb64, HEAD_DIM // 2)

    def rope_mla(v):
        return _rope_chunk(v, cm, sam, sbm, MLA_ROPE // 2)

    cq = _rms_norm(proj(_C_CQ, MLA_Q_LORA), qn_ref[...]).astype(BF16)
    q_raw = _dot(cq, wuq_ref[...])
    ckv = _rms_norm(proj(_C_CKV, MLA_KV_LORA), kvn_ref[...]).astype(BF16)
    k_nope = _dot(ckv, wkn_ref[...])
    va_ref[0, 0] = _dot(ckv, wv_ref[...]).T.astype(BF16)
    kpe = rope_mla(proj(_C_KPE, LANES))
    q_scale = (MLA_NOPE + MLA_ROPE) ** -0.5 * LOG2_E
    for h in range(MLA_HEADS):
        sl = slice(h * LANES, (h + 1) * LANES)
        qa_ref[:, sl] = (rope_mla(q_raw[:, sl]) * q_scale).astype(BF16)
        ka_ref[:, sl] = (k_nope[:, sl] + kpe).astype(BF16)

    scale = HEAD_DIM ** -0.5
    low_half = _lane_half((tm, LANES)) == 0
    dq = proj(_C_DQ, DIL_HEADS * HEAD_DIM)
    dk = proj(_C_DK, DIL_HEADS * HEAD_DIM)
    dv = proj(_C_DV, DIL_HEADS * HEAD_DIM)
    for c in range(_DIL_KV_CHUNKS):
        pair = rope64(dq[:, c * LANES:(c + 1) * LANES]) * (scale * LOG2_E)
        stage_ref[2 * c] = jnp.where(low_half, pair, 0.0)
        stage_ref[2 * c + 1] = jnp.where(low_half, 0.0, pair)
        stage_ref[_DIL_Q_CHUNKS + c] = rope64(dk[:, c * LANES:(c + 1) * LANES])
        stage_ref[_DIL_Q_CHUNKS + _DIL_KV_CHUNKS + c] = dv[:, c * LANES:(c + 1) * LANES]
    for p, (_, r) in enumerate(DIL_PATTERNS):
        for first, chunks, o_ref in ((0, _DIL_Q_CHUNKS, dil_refs[3 * p]),
                                     (_DIL_Q_CHUNKS, _DIL_KV_CHUNKS, dil_refs[3 * p + 1]),
                                     (_DIL_Q_CHUNKS + _DIL_KV_CHUNKS, _DIL_KV_CHUNKS, dil_refs[3 * p + 2])):
            for j in range(r):
                for c in range(chunks):
                    rows = stage_ref[first + c, pl.ds(j, tm // r, stride=r), :]
                    o_ref[0, j, :, c * LANES:(c + 1) * LANES] = rows.astype(BF16)

    nq = proj(_C_NQ, NSA_HEADS * HEAD_DIM)
    for c in range(NSA_HEADS_PER_GROUP):
        pair = nq[:, c * LANES:(c + 1) * LANES]
        plain = pair * scale
        rotated = rope64(pair) * (scale * LOG2_E)
        for h, keep in ((c, low_half), (c + NSA_HEADS_PER_GROUP, ~low_half)):
            sl = slice(h * LANES, (h + 1) * LANES)
            nq_ref[:, sl] = jnp.where(keep, plain, 0.0).astype(BF16)
            nqr_ref[:, sl] = jnp.where(keep, rotated, 0.0).astype(BF16)
    kc_ref[...] = proj(_C_KC, LANES)
    vc_ref[...] = proj(_C_VC, LANES)
    ksl_ref[...] = rope64(proj(_C_KSL, LANES)).astype(BF16)
    vsl_ref[0, 0] = proj(_C_VSL, LANES).T.astype(BF16)
    kw_ref[...] = rope64(proj(_C_KW, LANES)).astype(BF16)
    vw_ref[...] = proj(_C_VW, LANES).astype(BF16)
    gl_ref[...] = jax.nn.sigmoid(proj(_C_GL, LANES))


def _proj_source_columns():
    src = -np.ones(_PROJ_WIDTH, np.int64)

    def put(dst, s0, n):
        src[dst:dst + n] = s0 + np.arange(n)

    put(_C_CQ, _SRC_CQ, MLA_Q_LORA)
    put(_C_CKV, _SRC_CKV, MLA_KV_LORA)
    put(_C_KPE + MLA_NOPE, _SRC_KPE, MLA_ROPE)
    put(_C_DQ, _SRC_DQ, DIL_HEADS * HEAD_DIM)
    put(_C_DK, _SRC_DK, DIL_HEADS * HEAD_DIM)
    put(_C_DV, _SRC_DV, DIL_HEADS * HEAD_DIM)
    for c in range(NSA_HEADS_PER_GROUP):
        for half, h in enumerate((c, c + NSA_HEADS_PER_GROUP)):
            put(_C_NQ + c * LANES + half * HEAD_DIM, _SRC_NQ + h * HEAD_DIM, HEAD_DIM)
    for dst, s0 in ((_C_KC, _SRC_KC), (_C_VC, _SRC_VC), (_C_KSL, _SRC_KSL), (_C_VSL, _SRC_VSL),
                    (_C_KW, _SRC_KW), (_C_VW, _SRC_VW)):
        put(dst, s0, NSA_KV_GROUPS * HEAD_DIM)
    put(_C_GL, _SRC_GL, NSA_HEADS * 3)
    return src


def _gather_columns(w, src, axis):
    valid = jnp.asarray(src >= 0)
    taken = jnp.take(w, jnp.asarray(np.maximum(src, 0)), axis=axis)
    shape = [1, 1]
    shape[axis] = -1
    return jnp.where(valid.reshape(shape), taken, 0.0)


def _mla_weight_layout(w_uq, w_ukv):
    dq = MLA_NOPE + MLA_ROPE
    src_q = -np.ones(MLA_HEADS * LANES, np.int64)
    src_kn = -np.ones(MLA_HEADS * LANES, np.int64)
    src_v = np.zeros(MLA_HEADS * MLA_V, np.int64)
    for h in range(MLA_HEADS):
        src_q[h * LANES:h * LANES + dq] = h * dq + np.arange(dq)
        src_kn[h * LANES:h * LANES + MLA_NOPE] = h * (MLA_NOPE + MLA_V) + np.arange(MLA_NOPE)
        src_v[h * MLA_V:(h + 1) * MLA_V] = h * (MLA_NOPE + MLA_V) + MLA_NOPE + np.arange(MLA_V)
    return (_gather_columns(w_uq, src_q, 1).astype(BF16),
            _gather_columns(w_ukv, src_kn, 1).astype(BF16),
            _gather_columns(w_ukv, src_v, 1).astype(BF16))


def _rope_tables(S):
    def cos_sin(dim):
        inv_freq = ROPE_THETA ** (-jnp.arange(0, dim, 2, dtype=F32) / dim)
        ang = jnp.arange(S, dtype=F32)[:, None] * inv_freq[None, :]
        return jnp.cos(ang), jnp.sin(ang)

    cos, sin = cos_sin(HEAD_DIM)
    zero = jnp.zeros_like(sin)
    c64 = jnp.concatenate([cos, cos] * 2, axis=1)
    sa64 = jnp.concatenate([-sin, zero] * 2, axis=1)
    sb64 = jnp.concatenate([zero, sin] * 2, axis=1)
    cos_m, sin_m = cos_sin(MLA_ROPE)
    ones = jnp.ones((S, MLA_NOPE), F32)
    z64 = jnp.zeros((S, MLA_NOPE), F32)
    z16 = jnp.zeros_like(sin_m)
    z32 = jnp.zeros((S, LANES - MLA_NOPE - MLA_ROPE), F32)
    cm = jnp.concatenate([ones, cos_m, cos_m, z32], axis=1)
    sam = jnp.concatenate([z64, -sin_m, z16, z32], axis=1)
    sbm = jnp.concatenate([z64, z16, sin_m, z32], axis=1)
    return c64, sa64, sb64, cm, sam, sbm


def _input_projection(h, w_big, q_norm, kv_norm, wuq, wkn, wv, tables, S, *, tm=KV_TILE):
    T = h.shape[0]
    n_pos = S // tm
    row = lambda w: pl.BlockSpec((tm, w), lambda i: (i, 0))
    full = lambda a: pl.BlockSpec(a.shape, lambda i: (0, 0))
    tab = pl.BlockSpec((tm, LANES), lambda i: (i % n_pos, 0))
    B = T // S
    specs, shapes = [], []

    def rows_out(w, dt):
        specs.append(row(w))
        shapes.append(jax.ShapeDtypeStruct((T, w), dt))

    def tile_t_out(c):
        specs.append(pl.BlockSpec((1, 1, c, tm), lambda i: (i // n_pos, i % n_pos, 0, 0)))
        shapes.append(jax.ShapeDtypeStruct((B, n_pos, c, tm), BF16))

    def residue_out(r, w):
        specs.append(pl.BlockSpec((1, r, tm // r, w), lambda i: (i // n_pos, 0, i % n_pos, 0)))
        shapes.append(jax.ShapeDtypeStruct((B, r, S // r, w), BF16))

    rows_out(MLA_HEADS * LANES, BF16)
    rows_out(MLA_HEADS * LANES, BF16)
    tile_t_out(MLA_HEADS * MLA_V)
    for _, r in DIL_PATTERNS:
        residue_out(r, DIL_HEADS * LANES)
        residue_out(r, DIL_HEADS * HEAD_DIM)
        residue_out(r, DIL_HEADS * HEAD_DIM)
    rows_out(NSA_HEADS * LANES, BF16)
    rows_out(NSA_HEADS * LANES, BF16)
    rows_out(LANES, F32)
    rows_out(LANES, F32)
    rows_out(LANES, BF16)
    tile_t_out(LANES)
    rows_out(LANES, BF16)
    rows_out(LANES, BF16)
    rows_out(LANES, F32)
    qn = q_norm.reshape(1, -1)
    kvn = kv_norm.reshape(1, -1)
    return pl.pallas_call(
        _proj_kernel,
        grid=(T // tm,),
        in_specs=[row(D_MODEL), full(w_big), full(qn), full(kvn), full(wuq), full(wkn), full(wv)]
                 + [tab] * 6,
        out_specs=specs,
        out_shape=shapes,
        scratch_shapes=[pltpu.VMEM((_DIL_Q_CHUNKS + 2 * _DIL_KV_CHUNKS, tm, LANES), F32)],
        compiler_params=_cparams("parallel"),
        name="in_proj",
    )(h, w_big, qn, kvn, wuq, wkn, wv, *tables)


ONES_ROWS = 16


def _flash_step_t(s, vt1, m_ref, acc_ref, idx):
    m_prev = m_ref[idx]
    m_new = jnp.maximum(m_prev, jnp.max(s, axis=0, keepdims=True))
    alpha = jnp.exp2(m_prev - m_new)
    p = jnp.exp2(s - m_new).astype(BF16)
    acc_ref[idx] = alpha * acc_ref[idx] + _dot(vt1, p)
    m_ref[idx] = m_new


def _init_softmax_state(m_ref, acc_ref):
    m_ref[...] = jnp.full(m_ref.shape, NEG_INF, F32)
    acc_ref[...] = jnp.zeros_like(acc_ref)


def _flash_state(slots, channels, queries):
    return [pltpu.VMEM((slots, 1, queries), F32), pltpu.VMEM((slots, channels + ONES_ROWS, queries), F32)]


def _with_ones_rows(vt):
    return jnp.concatenate([vt, jnp.ones((ONES_ROWS, vt.shape[1]), vt.dtype)], axis=0)


def _normalised(acc, channels):
    return acc[:channels] / acc[channels:channels + 1]


def _mla_kernel(q_ref, k_ref, vt_ref, o_ref, qt_ref, m_ref, acc_ref, *, tq, tk):
    q0 = pl.program_id(1) * tq
    _init_softmax_state(m_ref, acc_ref)
    for h in range(MLA_HEADS):
        qt_ref[h] = q_ref[0, :, h * LANES:(h + 1) * LANES].T
    kofs = lax.broadcasted_iota(jnp.int32, (tk, tq), 0)
    qpos = q0 + lax.broadcasted_iota(jnp.int32, (tk, tq), 1)

    def tile(kt, masked):
        start = pl.multiple_of(kt * tk, tk)

        def scores(h):
            s = _dot(k_ref[0, pl.ds(start, tk), h * LANES:(h + 1) * LANES], qt_ref[h])
            return jnp.where((start + kofs) <= qpos, s, NEG_INF) if masked else s

        s_next = scores(0)
        for h in range(MLA_HEADS):
            s = s_next
            if h + 1 < MLA_HEADS:
                s_next = scores(h + 1)
            vt = vt_ref[0, kt, (h // 2) * LANES:(h // 2 + 1) * LANES, :]
            _flash_step_t(s, _with_ones_rows(vt), m_ref, acc_ref, h)

    def body(kt, carry):
        tile(kt, False)
        return carry

    n_full = q0 // tk
    lax.fori_loop(0, n_full, body, 0)
    for d in range(tq // tk):
        tile(n_full + d, True)
    even_rows = lax.broadcasted_iota(jnp.int32, (LANES, tq), 0) < MLA_V
    for pair in range(MLA_HEADS // 2):
        h0, h1 = 2 * pair, 2 * pair + 1
        out_t = jnp.where(even_rows, _normalised(acc_ref[h0], LANES), _normalised(acc_ref[h1], LANES))
        o_ref[0, :, pair * LANES:(pair + 1) * LANES] = out_t.T.astype(o_ref.dtype)


def _mla_attention(q, k, vt, *, tq=1024, tk=KV_TILE):
    B, S, _ = q.shape
    return pl.pallas_call(
        functools.partial(_mla_kernel, tq=tq, tk=tk),
        grid=(B, S // tq),
        in_specs=[
            pl.BlockSpec((1, tq, MLA_HEADS * LANES), lambda b, i: (b, i, 0)),
            pl.BlockSpec((1, S, MLA_HEADS * LANES), lambda b, i: (b, 0, 0)),
            pl.BlockSpec((1, S // tk, MLA_HEADS * MLA_V, tk), lambda b, i: (b, 0, 0, 0)),
        ],
        out_specs=pl.BlockSpec((1, tq, MLA_HEADS * MLA_V), lambda b, i: (b, i, 0)),
        out_shape=jax.ShapeDtypeStruct((B, S, MLA_HEADS * MLA_V), BF16),
        scratch_shapes=[pltpu.VMEM((MLA_HEADS, LANES, tq), BF16)] + _flash_state(MLA_HEADS, LANES, tq),
        compiler_params=_cparams("parallel", "arbitrary"),
        name="mla_attn",
    )(q, k, vt)


def _dot_tn(a, b):
    return lax.dot_general(a, b, (((0,), (0,)), ((), ())), preferred_element_type=F32)


def _band_mask_t(q0, kstart, tq, span, window):
    kpos = kstart + lax.broadcasted_iota(jnp.int32, (span, tq), 0)
    qpos = q0 + lax.broadcasted_iota(jnp.int32, (span, tq), 1)
    dist = qpos - kpos
    return (dist >= 0) & (dist <= window)


def _dil_kernel(q_ref, k_ref, v_ref, o_ref, lse_ref, qt_ref, *, tq, sub, wpad, window):
    q0 = pl.program_id(2) * tq
    n = k_ref.shape[2]
    span = min(sub + wpad, n)
    even_rows = lax.broadcasted_iota(jnp.int32, (LANES, sub), 0) < HEAD_DIM
    windows = []
    for a in range(tq // sub):
        qs = q0 + a * sub
        kstart = pl.multiple_of(jnp.clip(qs - wpad, 0, n - span), LANES)
        windows.append((kstart, _band_mask_t(qs, kstart, sub, span, window)))
    chains = [(a, c, hh) for a in range(tq // sub) for c in range(DIL_HEADS // 2) for hh in range(2)]
    for a in range(tq // sub):
        for h in range(DIL_HEADS):
            qt_ref[a, h] = q_ref[0, 0, a * sub:(a + 1) * sub, h * LANES:(h + 1) * LANES].T

    def scores(a, c, hh):
        kstart, mask = windows[a]
        k = k_ref[0, 0, pl.ds(kstart, span), c * LANES:(c + 1) * LANES]
        return jnp.where(mask, _dot(k, qt_ref[a, 2 * c + hh]), NEG_INF)

    s_next = scores(*chains[0])
    outs, lses = [], []
    for i, (a, c, hh) in enumerate(chains):
        s = s_next
        if i + 1 < len(chains):
            s_next = scores(*chains[i + 1])
        m = jnp.max(s, axis=0, keepdims=True)
        e = jnp.exp2(s - m)
        l = jnp.sum(e, axis=0, keepdims=True)
        v = v_ref[0, 0, pl.ds(windows[a][0], span), c * LANES:(c + 1) * LANES]
        outs.append(_dot_tn(v, e.astype(BF16)) * (1.0 / l))
        lses.append(jnp.broadcast_to(m * LN_2 + jnp.log(l), (LANES, sub)))
        if hh == 1:
            rows, csl = slice(a * sub, (a + 1) * sub), slice(c * LANES, (c + 1) * LANES)
            o_ref[0, 0, rows, csl] = jnp.where(even_rows, outs[-2], outs[-1]).T
            lse_ref[0, 0, rows, csl] = jnp.where(even_rows, lses[-2], lses[-1]).T


def _dilated_pattern(q, k, v, window, dil, *, tq=512, sub=256):
    B, _, n, wq = q.shape
    wk = k.shape[-1]
    wsub = window // dil
    wpad = -(-wsub // LANES) * LANES
    tq = min(tq, n)
    sub = min(sub, tq)
    return pl.pallas_call(
        functools.partial(_dil_kernel, tq=tq, sub=sub, wpad=wpad, window=wsub),
        grid=(B, dil, n // tq),
        in_specs=[
            pl.BlockSpec((1, 1, tq, wq), lambda b, j, i: (b, j, i, 0)),
            pl.BlockSpec((1, 1, n, wk), lambda b, j, i: (b, j, 0, 0)),
            pl.BlockSpec((1, 1, n, wk), lambda b, j, i: (b, j, 0, 0)),
        ],
        out_specs=[pl.BlockSpec((1, 1, tq, wk), lambda b, j, i: (b, j, i, 0))] * 2,
        out_shape=[jax.ShapeDtypeStruct((B, dil, n, wk), F32)] * 2,
        scratch_shapes=[pltpu.VMEM((tq // sub, DIL_HEADS, LANES, sub), BF16)],
        compiler_params=_cparams("parallel", "parallel", "arbitrary"),
        name=f"dilated_r{dil}",
    )(q, k, v)


def _compress_kernel(xk_ref, xv_ref, pos_ref, w1_ref, w2_ref, kc_ref, vct_ref):
    st = NSA_CMP_STRIDE
    n = xk_ref.shape[1] // st
    for t, x_ref in enumerate((xk_ref, xv_ref)):
        out = jnp.zeros((n, LANES), F32)
        for g in range(NSA_KV_GROUPS):
            first = jnp.zeros((n, NSA_CMP_HIDDEN), F32)
            second = jnp.zeros((n, NSA_CMP_HIDDEN), F32)
            for l in range(st):
                x = x_ref[0, pl.ds(l, n, stride=st), :]
                first = first + _dot((x + pos_ref[t, l]).astype(BF16), w1_ref[t, g, l])
                second = second + _dot((x + pos_ref[t, st + l]).astype(BF16), w1_ref[t, g, st + l])
            hid = first + pltpu.roll(second, n - 1, 0)
            hid = (hid * jax.nn.sigmoid(hid)).astype(BF16)
            out = out + _dot(hid, w2_ref[t, g])
        if t == 0:
            kc_ref[0] = out.astype(kc_ref.dtype)
        else:
            vct_ref[0] = out.T.astype(vct_ref.dtype)


def _nsa_compress(kc, vc, cmp_pos, cmp_w1, cmp_w2):
    B, S, _ = kc.shape
    G, d = NSA_KV_GROUPS, HEAD_DIM
    n = S // NSA_CMP_STRIDE
    w1 = cmp_w1.reshape(2, NSA_CMP_BLOCK, d, NSA_CMP_HIDDEN)
    zero1 = jnp.zeros_like(w1)
    w1g = jnp.stack([jnp.concatenate([w1, zero1], axis=2), jnp.concatenate([zero1, w1], axis=2)], axis=1)
    zero2 = jnp.zeros_like(cmp_w2)
    w2g = jnp.stack([jnp.concatenate([cmp_w2, zero2], axis=2), jnp.concatenate([zero2, cmp_w2], axis=2)], axis=1)
    pos = jnp.concatenate([cmp_pos] * G, axis=-1).reshape(2, NSA_CMP_BLOCK, 1, G * d)
    w1g, w2g = w1g.astype(BF16), w2g.astype(BF16)
    xspec = pl.BlockSpec((1, S, G * d), lambda b: (b, 0, 0))
    full = lambda a: pl.BlockSpec(a.shape, lambda b: (0,) * a.ndim)
    return pl.pallas_call(
        _compress_kernel,
        grid=(B,),
        in_specs=[xspec, xspec, full(pos), full(w1g), full(w2g)],
        out_specs=[pl.BlockSpec((1, n, G * d), lambda b: (b, 0, 0)),
                   pl.BlockSpec((1, G * d, n), lambda b: (b, 0, 0))],
        out_shape=[jax.ShapeDtypeStruct((B, n, G * d), BF16), jax.ShapeDtypeStruct((B, G * d, n), BF16)],
        compiler_params=_cparams("parallel"),
        name="nsa_compress",
    )(kc, vc, pos, w1g, w2g)


def _dot_01_by_f32(o01, p):
    p1 = p.astype(BF16)
    r1 = p - p1.astype(F32)
    p2 = r1.astype(BF16)
    p3 = (r1 - p2.astype(F32)).astype(BF16)
    return _dot(o01, p1) + _dot(o01, p2) + _dot(o01, p3)


SUBLANES = 8


def _count_outranking(sc_ref, cnt_ref, n_sources, tq):
    groups = SEL_LANES // SUBLANES
    sub = lax.broadcasted_iota(jnp.int32, (SUBLANES, tq), 0)
    cnt_ref[...] = jnp.zeros_like(cnt_ref)
    for bi in range(groups):
        @pl.when(bi * SUBLANES < n_sources)
        def _():
            src = sc_ref[bi * SUBLANES:(bi + 1) * SUBLANES, :]
            rows = [jnp.broadcast_to(src[ii:ii + 1, :], (SUBLANES, tq)) for ii in range(SUBLANES)]
            for r in range(groups):
                rsl = slice(r * SUBLANES, (r + 1) * SUBLANES)
                tgt = sc_ref[rsl, :]
                cnt = cnt_ref[rsl, :]
                for ii, row in enumerate(rows):
                    if r > bi:
                        ahead = row >= tgt
                    elif r < bi:
                        ahead = row > tgt
                    else:
                        ahead = (row > tgt) | ((row == tgt) & (sub > ii))
                    cnt = cnt + jnp.where(ahead, 1.0, 0.0)
                cnt_ref[rsl, :] = cnt


def _cmp_kernel(q_ref, kc_ref, vct_ref, ovt_ref, o_ref, bias_ref, qt_ref, sc_ref, cnt_ref, *, tq):
    g = pl.program_id(1)
    q0 = pl.program_id(2) * tq
    n = kc_ref.shape[1]
    kc = kc_ref[0]
    vct = vct_ref[0]
    qpos = q0 + lax.broadcasted_iota(jnp.int32, (n, tq), 1)
    cmp_end = lax.broadcasted_iota(jnp.int32, (n, tq), 0) * NSA_CMP_STRIDE + (NSA_CMP_BLOCK - 1)
    visible = cmp_end <= qpos
    in_group = _lane_half((tq, LANES)) == g
    psum = jnp.zeros((n, tq), F32)

    for hh in range(NSA_HEADS_PER_GROUP):
        qt_ref[hh] = q_ref[0, :, hh * LANES:(hh + 1) * LANES].T

    def scores(hh):
        return jnp.where(visible, _dot(kc, qt_ref[hh]), NEG_INF)

    s_next = scores(0)
    for hh in range(NSA_HEADS_PER_GROUP):
        s = s_next
        if hh + 1 < NSA_HEADS_PER_GROUP:
            s_next = scores(hh + 1)
        m = jnp.max(s, axis=0, keepdims=True)
        e = jnp.where(visible, jnp.exp(s - m), 0.0)
        l = jnp.maximum(jnp.sum(e, axis=0, keepdims=True), 1e-30)
        p = e * (1.0 / l)
        o = _dot(vct, p.astype(BF16)).T
        o_ref[0, :, hh * LANES:(hh + 1) * LANES] = jnp.where(in_group, o, 0.0).astype(o_ref.dtype)
        psum = psum + p

    imp_t = _dot_01_by_f32(ovt_ref[...], psum)
    blk = lax.broadcasted_iota(jnp.int32, (SEL_LANES, tq), 0)
    cur = lax.shift_right_logical(q0 + lax.broadcasted_iota(jnp.int32, (SEL_LANES, tq), 1), 6)
    forced = (blk == 0) | (blk == cur) | (blk == cur - 1)
    sc_ref[...] = jnp.where(forced, FORCE_SCORE, jnp.where(blk <= cur, imp_t, -FORCE_SCORE))
    n_causal = jnp.minimum(lax.shift_right_logical(q0 + tq - 1, 6) + 1, SEL_LANES)
    _count_outranking(sc_ref, cnt_ref, n_causal, tq)
    bias_t = jnp.where(cnt_ref[...] < float(NSA_TOP_N), 0.0, UNSELECTED_BIAS)
    bias_ref[0, 0] = bias_t.astype(bias_ref.dtype)


def _nsa_compressed_branch(nq, kcc, vcct, overlap, *, tq=256):
    B, S, _ = nq.shape
    G = NSA_KV_GROUPS
    n = kcc.shape[1]
    wq = NSA_HEADS_PER_GROUP * LANES
    return pl.pallas_call(
        functools.partial(_cmp_kernel, tq=tq),
        grid=(B, G, S // tq),
        in_specs=[
            pl.BlockSpec((1, tq, wq), lambda b, g, i: (b, i, g)),
            pl.BlockSpec((1, n, LANES), lambda b, g, i: (b, 0, 0)),
            pl.BlockSpec((1, LANES, n), lambda b, g, i: (b, 0, 0)),
            pl.BlockSpec((SEL_LANES, n), lambda b, g, i: (0, 0)),
        ],
        out_specs=[
            pl.BlockSpec((1, tq, wq), lambda b, g, i: (b, i, g)),
            pl.BlockSpec((1, 1, SEL_LANES, tq), lambda b, g, i: (b, g, 0, i)),
        ],
        out_shape=[jax.ShapeDtypeStruct((B, S, NSA_HEADS * LANES), BF16),
                   jax.ShapeDtypeStruct((B, G, SEL_LANES, S), BF16)],
        scratch_shapes=[pltpu.VMEM((NSA_HEADS_PER_GROUP, LANES, tq), BF16),
                        pltpu.VMEM((SEL_LANES, tq), F32), pltpu.VMEM((SEL_LANES, tq), F32)],
        compiler_params=_cparams("parallel", "parallel", "arbitrary"),
        name="nsa_cmp_topk",
    )(nq, kcc, vcct, overlap.T)


def _sel_kernel(q_ref, bias_ref, k_ref, vt_ref, oh_ref, o_ref, qa_ref, m_ref, acc_ref, *, tq, tk):
    q0 = pl.program_id(1) * tq
    for h in range(NSA_HEADS):
        qa_ref[h, :LANES, :] = q_ref[0, :, h * LANES:(h + 1) * LANES].T
        qa_ref[h, LANES:, :] = bias_ref[0, h // NSA_HEADS_PER_GROUP]
    kofs = lax.broadcasted_iota(jnp.int32, (tk, tq), 0)
    qpos = q0 + lax.broadcasted_iota(jnp.int32, (tk, tq), 1)
    _init_softmax_state(m_ref, acc_ref)

    def tile(kt, masked):
        start = pl.multiple_of(kt * tk, tk)
        ka = jnp.concatenate([k_ref[0, pl.ds(start, tk), :], oh_ref[pl.ds(start, tk), :]], axis=1)
        vt1 = _with_ones_rows(vt_ref[0, kt])

        def scores(h):
            s = _dot(ka, qa_ref[h])
            return jnp.where((start + kofs) <= qpos, s, NEG_INF) if masked else s

        s_next = scores(0)
        for h in range(NSA_HEADS):
            s = s_next
            if h + 1 < NSA_HEADS:
                s_next = scores(h + 1)
            _flash_step_t(s, vt1, m_ref, acc_ref, h)

    def body(kt, carry):
        tile(kt, False)
        return carry

    n_full = q0 // tk
    lax.fori_loop(0, n_full, body, 0)
    tile(n_full, True)
    half = _lane_half((tq, LANES))
    for h in range(NSA_HEADS):
        out = _normalised(acc_ref[h], LANES).T
        o_ref[0, :, h * LANES:(h + 1) * LANES] = jnp.where(
            half == h // NSA_HEADS_PER_GROUP, out, 0.0).astype(o_ref.dtype)


def _nsa_selected_branch(nqr, bias, ksl, vslt, onehot, *, tq=512, tk=KV_TILE):
    B, S, wq = nqr.shape
    G = NSA_KV_GROUPS
    return pl.pallas_call(
        functools.partial(_sel_kernel, tq=tq, tk=tk),
        grid=(B, S // tq),
        in_specs=[
            pl.BlockSpec((1, tq, wq), lambda b, i: (b, i, 0)),
            pl.BlockSpec((1, G, SEL_LANES, tq), lambda b, i: (b, 0, 0, i)),
            pl.BlockSpec((1, S, LANES), lambda b, i: (b, 0, 0)),
            pl.BlockSpec((1, S // tk, LANES, tk), lambda b, i: (b, 0, 0, 0)),
            pl.BlockSpec((S, SEL_LANES), lambda b, i: (0, 0)),
        ],
        out_specs=pl.BlockSpec((1, tq, wq), lambda b, i: (b, i, 0)),
        out_shape=jax.ShapeDtypeStruct((B, S, wq), BF16),
        scratch_shapes=[pltpu.VMEM((NSA_HEADS, LANES + SEL_LANES, tq), BF16)]
                       + _flash_state(NSA_HEADS, LANES, tq),
        compiler_params=_cparams("parallel", "arbitrary"),
        name="nsa_selected",
    )(nqr, bias, ksl, vslt, onehot)


def _win_kernel(q_ref, k_ref, v_ref, o_ref, qt_ref, *, tq, wpad, window):
    q0 = pl.program_id(1) * tq
    span = tq + wpad
    kstart = pl.multiple_of(jnp.maximum(q0 - wpad, 0), LANES)
    mask = _band_mask_t(q0, kstart, tq, span, window)
    k = k_ref[0, pl.ds(kstart, span), :]
    v = v_ref[0, pl.ds(kstart, span), :]
    half = _lane_half((tq, LANES))
    for h in range(NSA_HEADS):
        qt_ref[h] = q_ref[0, :, h * LANES:(h + 1) * LANES].T

    def scores(h):
        return jnp.where(mask, _dot(k, qt_ref[h]), NEG_INF)

    s_next = scores(0)
    for h in range(NSA_HEADS):
        s = s_next
        if h + 1 < NSA_HEADS:
            s_next = scores(h + 1)
        m = jnp.max(s, axis=0, keepdims=True)
        e = jnp.exp2(s - m)
        l = jnp.sum(e, axis=0, keepdims=True)
        o = (_dot_tn(v, e.astype(BF16)) * (1.0 / l)).T
        o_ref[0, :, h * LANES:(h + 1) * LANES] = jnp.where(
            half == h // NSA_HEADS_PER_GROUP, o, 0.0).astype(o_ref.dtype)


def _nsa_window_branch(nqr, kw, vw, *, tq=256):
    B, S, wq = nqr.shape
    wpad = -(-NSA_WINDOW // LANES) * LANES
    return pl.pallas_call(
        functools.partial(_win_kernel, tq=tq, wpad=wpad, window=NSA_WINDOW),
        grid=(B, S // tq),
        in_specs=[
            pl.BlockSpec((1, tq, wq), lambda b, i: (b, i, 0)),
            pl.BlockSpec((1, S, LANES), lambda b, i: (b, 0, 0)),
            pl.BlockSpec((1, S, LANES), lambda b, i: (b, 0, 0)),
        ],
        out_specs=pl.BlockSpec((1, tq, wq), lambda b, i: (b, i, 0)),
        out_shape=jax.ShapeDtypeStruct((B, S, wq), BF16),
        scratch_shapes=[pltpu.VMEM((NSA_HEADS, LANES, tq), BF16)],
        compiler_params=_cparams("parallel", "arbitrary"),
        name="nsa_window",
    )(nqr, kw, vw)


def _selection_constants(S):
    n_cmp_rows = S // NSA_CMP_STRIDE
    c = np.arange(n_cmp_rows)
    cmp_start = c * NSA_CMP_STRIDE
    cmp_end = cmp_start + NSA_CMP_BLOCK - 1
    sel_start = np.arange(SEL_LANES) * NSA_SEL_BLOCK
    overlap = ((cmp_start[:, None] < sel_start[None, :] + NSA_SEL_BLOCK) &
               (cmp_end[:, None] >= sel_start[None, :]))
    n_cmp = (S - NSA_CMP_BLOCK) // NSA_CMP_STRIDE + 1
    overlap &= (c < n_cmp)[:, None]
    onehot = (np.arange(S)[:, None] // NSA_SEL_BLOCK) == np.arange(SEL_LANES)[None, :]
    return jnp.asarray(overlap, BF16), jnp.asarray(onehot, BF16)


def _out_kernel(*refs):
    n_pat = len(DIL_PATTERNS)
    x_ref, oa_ref = refs[:2]
    dil_refs = refs[2:2 + 2 * n_pat]
    (oc_ref, os_ref, ow_ref, gl_ref, wa_ref, wb_ref, wc_ref, g_ref, b_ref, o_ref,
     nat_ref) = refs[2 + 2 * n_pat:]
    tm = x_ref.shape[0]
    mix = _dot(oa_ref[...], wa_ref[...])

    for p, (_, r) in enumerate(DIL_PATTERNS):
        for a in range(2):
            for j in range(r):
                for c in range(_DIL_KV_CHUNKS):
                    nat_ref[2 * p + a, c, pl.ds(j, tm // r, stride=r), :] = (
                        dil_refs[2 * p + a][0, j, :, c * LANES:(c + 1) * LANES])
    for c in range(_DIL_KV_CHUNKS):
        outs = [nat_ref[2 * p, c] for p in range(n_pat)]
        lses = [nat_ref[2 * p + 1, c] for p in range(n_pat)]
        mx = functools.reduce(jnp.maximum, lses)
        es = [jnp.exp(l - mx) for l in lses]
        inv = 1.0 / functools.reduce(jnp.add, es)
        ob = functools.reduce(jnp.add, [(e * inv) * o for e, o in zip(es, outs)])
        mix = mix + _dot(ob.astype(BF16), wb_ref[c * LANES:(c + 1) * LANES, :])

    gl = gl_ref[...]
    for h in range(NSA_HEADS):
        sl = slice(h * LANES, (h + 1) * LANES)
        oc = (gl[:, 3 * h:3 * h + 1] * oc_ref[:, sl].astype(F32)
              + gl[:, 3 * h + 1:3 * h + 2] * os_ref[:, sl].astype(F32)
              + gl[:, 3 * h + 2:3 * h + 3] * ow_ref[:, sl].astype(F32))
        mix = mix + _dot(oc.astype(BF16), wc_ref[sl, :])

    y = ALPHA * x_ref[...] + mix
    o_ref[...] = _layer_norm(y, g_ref[...], b_ref[...])


def _out_weight_layout(w_out):
    na = MLA_HEADS * MLA_V
    nb = DIL_HEADS * HEAD_DIM
    src = -np.ones(NSA_HEADS * LANES, np.int64)
    for h in range(NSA_HEADS):
        d0 = h * LANES + HEAD_DIM * (h // NSA_HEADS_PER_GROUP)
        src[d0:d0 + HEAD_DIM] = na + nb + h * HEAD_DIM + np.arange(HEAD_DIM)
    return (w_out[:na].astype(BF16), w_out[na:na + nb].astype(BF16),
            _gather_columns(w_out, src, 0).astype(BF16))


def _output_projection(x, oa, dil, oc, osl, ow, gl, wa, wb, wc, gain, bias, S, *, tm=512):
    T = x.shape[0]
    n_pos = S // tm
    row = lambda a: pl.BlockSpec((tm, a.shape[1]), lambda i: (i, 0))
    full = lambda a: pl.BlockSpec(a.shape, lambda i: (0, 0))
    residue = lambda a: pl.BlockSpec((1, a.shape[1], tm // a.shape[1], a.shape[3]),
                                     lambda i: (i // n_pos, 0, i % n_pos, 0))
    g2, b2 = gain.reshape(1, -1), bias.reshape(1, -1)
    dil_flat = [a for pair in dil for a in pair]
    rows = [oc, osl, ow, gl]
    consts = [wa, wb, wc, g2, b2]
    return pl.pallas_call(
        _out_kernel,
        grid=(T // tm,),
        in_specs=[row(x), row(oa)] + [residue(a) for a in dil_flat] + [row(a) for a in rows]
                 + [full(a) for a in consts],
        out_specs=pl.BlockSpec((tm, D_MODEL), lambda i: (i, 0)),
        out_shape=jax.ShapeDtypeStruct((T, D_MODEL), F32),
        scratch_shapes=[pltpu.VMEM((len(dil_flat), _DIL_KV_CHUNKS, tm, LANES), F32)],
        compiler_params=_cparams("parallel"),
        name="out_proj_ln",
    )(x, oa, *dil_flat, *rows, *consts)


def kernel(x, ffn_w_in, ffn_w_out, ln_gain, ln_bias, w_in, w_out, mla_q_norm, mla_kv_norm,
           mla_w_uq, mla_w_ukv, nsa_cmp_pos, nsa_cmp_w1, nsa_cmp_w2):
    B, S, D = x.shape
    assert D == D_MODEL and S % 2048 == 0 and S // NSA_SEL_BLOCK <= SEL_LANES
    T = B * S
    tables = _rope_tables(S)
    overlap, onehot = _selection_constants(S)
    proj_src = _proj_source_columns()
    bs = lambda t: t.reshape(B, S, t.shape[-1])

    xf = x.reshape(T, D)
    for l in range(DEPTH):
        xf = _ffn_ln(xf, ffn_w_in[l, 0].astype(BF16), ffn_w_out[l, 0].astype(BF16),
                     ln_gain[l, 0], ln_bias[l, 0])

        w_big = _gather_columns(w_in[l], proj_src, 1).astype(BF16)
        wuq, wkn, wv = _mla_weight_layout(mla_w_uq[l], mla_w_ukv[l])
        outs = _input_projection(xf, w_big, mla_q_norm[l], mla_kv_norm[l], wuq, wkn, wv, tables, S)
        qa, ka, va = outs[:3]
        n_dil = 3 * len(DIL_PATTERNS)
        dqkv = outs[3:3 + n_dil]
        nq, nqr, kc, vc, ksl, vsl, kw, vw, gl = outs[3 + n_dil:]

        oa = _mla_attention(bs(qa), bs(ka), va).reshape(T, -1)
        dil = [_dilated_pattern(*dqkv[3 * p:3 * p + 3], window, r)
               for p, (window, r) in enumerate(DIL_PATTERNS)]
        kcc, vcc = _nsa_compress(bs(kc), bs(vc), nsa_cmp_pos[l], nsa_cmp_w1[l], nsa_cmp_w2[l])
        oc, sel_bias = _nsa_compressed_branch(bs(nq), kcc, vcc, overlap)
        osl = _nsa_selected_branch(bs(nqr), sel_bias, bs(ksl), vsl, onehot)
        ow = _nsa_window_branch(bs(nqr), bs(kw), bs(vw))

        wa, wb, wc = _out_weight_layout(w_out[l])
        xf = _output_projection(xf, oa, dil, oc.reshape(T, -1), osl.reshape(T, -1), ow.reshape(T, -1),
                                gl, wa, wb, wc, ln_gain[l, 1], ln_bias[l, 1], S)

        xf = _ffn_ln(xf, ffn_w_in[l, 1].astype(BF16), ffn_w_out[l, 1].astype(BF16),
                     ln_gain[l, 2], ln_bias[l, 2])
    return xf.reshape(B, S, D)
```

```python
import functools

import numpy as np
import jax
import jax.numpy as jnp
from jax import lax
from jax.experimental import pallas as pl
from jax.experimental.pallas import tpu as pltpu

F32 = jnp.float32
BF16 = jnp.bfloat16

D_MODEL = 1024
DEPTH = 2
HEAD_DIM = 64
MLA_HEADS = 4
MLA_Q_LORA = 256
MLA_KV_LORA = 128
MLA_NOPE = 64
MLA_ROPE = 32
MLA_V = 64
DIL_HEADS = 6
DIL_PATTERNS = ((128, 1), (512, 4), (2048, 16))
NSA_HEADS = 6
NSA_KV_GROUPS = 2
NSA_HEADS_PER_GROUP = NSA_HEADS // NSA_KV_GROUPS
NSA_CMP_BLOCK = 32
NSA_CMP_STRIDE = 16
NSA_CMP_HIDDEN = 256
NSA_SEL_BLOCK = 64
NSA_TOP_N = 16
NSA_WINDOW = 512
D_FF = ((8 * D_MODEL // 3 + 255) // 256) * 256
ROPE_THETA = 10000.0
LN_EPS = 1e-5
RMS_EPS = 1e-6
NEG_INF = -1e30
FORCE_SCORE = 1e4
ALPHA = (2 * DEPTH) ** 0.25
LOG2_E = 1.4426950408889634
LN_2 = 0.6931471805599453

LANES = 128
SEL_LANES = 128
UNSELECTED_BIAS = -1e9
VMEM_LIMIT = 56 * 1024 * 1024
KV_TILE = 512

_SRC_CQ, _SRC_CKV, _SRC_KPE = 0, 256, 384
_SRC_DQ, _SRC_DK, _SRC_DV = 416, 800, 1184
_SRC_NQ = 1568
_SRC_KC, _SRC_VC, _SRC_KSL, _SRC_VSL, _SRC_KW, _SRC_VW = 1952, 2080, 2208, 2336, 2464, 2592
_SRC_GL = 2720
_C_CQ, _C_CKV, _C_KPE, _C_DQ, _C_DK, _C_DV, _C_NQ = 0, 256, 384, 512, 896, 1280, 1664
_C_KC, _C_VC, _C_KSL, _C_VSL, _C_KW, _C_VW, _C_GL = 2048, 2176, 2304, 2432, 2560, 2688, 2816
_PROJ_WIDTH = 2944


def _cparams(*sem):
    return pltpu.CompilerParams(dimension_semantics=sem, vmem_limit_bytes=VMEM_LIMIT)


def _layer_norm(y, g, b):
    mu = jnp.mean(y, axis=-1, keepdims=True)
    d = y - mu
    var = jnp.mean(d * d, axis=-1, keepdims=True)
    return d * lax.rsqrt(var + LN_EPS) * g + b


def _dot(a, b):
    return jnp.dot(a, b, preferred_element_type=F32)


def _dot_nt(a, b):
    return lax.dot_general(a, b, (((1,), (1,)), ((), ())), preferred_element_type=F32)


def _lane_half(shape):
    return lax.shift_right_logical(lax.broadcasted_iota(jnp.int32, shape, len(shape) - 1), 6)


def _ffn_kernel(x_ref, wg_ref, wu_ref, wo_ref, g_ref, b_ref, o_ref):
    x = x_ref[...]
    xb = x.astype(BF16)
    gate = _dot(xb, wg_ref[...])
    up = _dot(xb, wu_ref[...])
    h = (gate * jax.nn.sigmoid(gate) * up).astype(BF16)
    y = ALPHA * x + 0.5 * _dot(h, wo_ref[...])
    o_ref[...] = _layer_norm(y, g_ref[...], b_ref[...])


def _ffn_ln(x, w_in, w_out, gain, bias, *, tm=512):
    T = x.shape[0]
    resident = lambda shape, idx: pl.BlockSpec(shape, idx, pipeline_mode=pl.Buffered(1))
    return pl.pallas_call(
        _ffn_kernel,
        grid=(T // tm,),
        in_specs=[
            pl.BlockSpec((tm, D_MODEL), lambda i: (i, 0)),
            resident((D_MODEL, D_FF), lambda i: (0, 0)),
            resident((D_MODEL, D_FF), lambda i: (0, 1)),
            resident((D_FF, D_MODEL), lambda i: (0, 0)),
            pl.BlockSpec((1, D_MODEL), lambda i: (0, 0)),
            pl.BlockSpec((1, D_MODEL), lambda i: (0, 0)),
        ],
        out_specs=pl.BlockSpec((tm, D_MODEL), lambda i: (i, 0)),
        out_shape=jax.ShapeDtypeStruct((T, D_MODEL), F32),
        compiler_params=_cparams("parallel"),
        name="ffn_ln",
    )(x, w_in, w_in, w_out, gain.reshape(1, -1), bias.reshape(1, -1))


def _rope_chunk(x, c, sa, sb, half):
    return x * c + pltpu.roll(x, LANES - half, 1) * sa + pltpu.roll(x, half, 1) * sb


def _rms_norm(x, g):
    return x * lax.rsqrt(jnp.mean(x * x, axis=-1, keepdims=True) + RMS_EPS) * g


_DIL_KV_CHUNKS = DIL_HEADS * HEAD_DIM // LANES
_NSA_Q_CHUNKS = NSA_HEADS * HEAD_DIM // LANES


def _proj_kernel(*refs):
    (x_ref, w_ref, qn_ref, kvn_ref, wuq_ref, wkn_ref, wv_ref,
     c64_ref, sa64_ref, sb64_ref, cm_ref, sam_ref, sbm_ref, qa_ref, ka_ref, va_ref) = refs[:16]
    dil_refs = refs[16:16 + 3 * len(DIL_PATTERNS)]
    (nq_ref, nqr_ref, kc_ref, vc_ref, ksl_ref, vsl_ref, kw_ref, vw_ref, gl_ref,
     stage_ref) = refs[16 + 3 * len(DIL_PATTERNS):]
    tm = x_ref.shape[0]
    xb = x_ref[...].astype(BF16)
    c64, sa64, sb64 = c64_ref[...], sa64_ref[...], sb64_ref[...]
    cm, sam, sbm = cm_ref[...], sam_ref[...], sbm_ref[...]

    y = _dot(xb, w_ref[...])

    def proj(off, width):
        return y[:, off:off + width]

    def rope64(v):
        return _rope_chunk(v, c64, sa64, sb64, HEAD_DIM // 2)

    def rope_mla(v):
        return _rope_chunk(v, cm, sam, sbm, MLA_ROPE // 2)

    cq = _rms_norm(proj(_C_CQ, MLA_Q_LORA), qn_ref[...]).astype(BF16)
    q_raw = _dot(cq, wuq_ref[...])
    ckv = _rms_norm(proj(_C_CKV, MLA_KV_LORA), kvn_ref[...]).astype(BF16)
    k_nope = _dot(ckv, wkn_ref[...])
    va_ref[0, 0] = _dot(ckv, wv_ref[...]).T.astype(BF16)
    kpe = rope_mla(proj(_C_KPE, LANES))
    q_scale = (MLA_NOPE + MLA_ROPE) ** -0.5 * LOG2_E
    for h in range(MLA_HEADS):
        sl = slice(h * LANES, (h + 1) * LANES)
        qa_ref[:, sl] = (rope_mla(q_raw[:, sl]) * q_scale).astype(BF16)
        ka_ref[:, sl] = (k_nope[:, sl] + kpe).astype(BF16)

    scale = HEAD_DIM ** -0.5
    dq = proj(_C_DQ, DIL_HEADS * HEAD_DIM)
    dk = proj(_C_DK, DIL_HEADS * HEAD_DIM)
    dv = proj(_C_DV, DIL_HEADS * HEAD_DIM)
    for c in range(_DIL_KV_CHUNKS):
        stage_ref[c] = rope64(dq[:, c * LANES:(c + 1) * LANES]) * (scale * LOG2_E)
        stage_ref[_DIL_KV_CHUNKS + c] = rope64(dk[:, c * LANES:(c + 1) * LANES])
        stage_ref[2 * _DIL_KV_CHUNKS + c] = dv[:, c * LANES:(c + 1) * LANES]
    for p, (_, r) in enumerate(DIL_PATTERNS):
        for t in range(3):
            o_ref = dil_refs[3 * p + t]
            for j in range(r):
                for c in range(_DIL_KV_CHUNKS):
                    rows = stage_ref[t * _DIL_KV_CHUNKS + c, pl.ds(j, tm // r, stride=r), :]
                    o_ref[0, j, :, c * LANES:(c + 1) * LANES] = rows.astype(BF16)

    nq = proj(_C_NQ, NSA_HEADS * HEAD_DIM)
    nq_ref[...] = (nq * scale).astype(BF16)
    for c in range(NSA_HEADS_PER_GROUP):
        sl = slice(c * LANES, (c + 1) * LANES)
        nqr_ref[:, sl] = (rope64(nq[:, sl]) * (scale * LOG2_E)).astype(BF16)
    kc_ref[...] = proj(_C_KC, LANES)
    vc_ref[...] = proj(_C_VC, LANES)
    ksl_ref[...] = rope64(proj(_C_KSL, LANES)).astype(BF16)
    vsl_ref[0, 0] = proj(_C_VSL, LANES).T.astype(BF16)
    kw_ref[...] = rope64(proj(_C_KW, LANES)).astype(BF16)
    vw_ref[...] = proj(_C_VW, LANES).astype(BF16)
    gl_ref[...] = jax.nn.sigmoid(proj(_C_GL, LANES))


def _proj_source_columns():
    src = -np.ones(_PROJ_WIDTH, np.int64)

    def put(dst, s0, n):
        src[dst:dst + n] = s0 + np.arange(n)

    put(_C_CQ, _SRC_CQ, MLA_Q_LORA)
    put(_C_CKV, _SRC_CKV, MLA_KV_LORA)
    put(_C_KPE + MLA_NOPE, _SRC_KPE, MLA_ROPE)
    put(_C_DQ, _SRC_DQ, DIL_HEADS * HEAD_DIM)
    put(_C_DK, _SRC_DK, DIL_HEADS * HEAD_DIM)
    put(_C_DV, _SRC_DV, DIL_HEADS * HEAD_DIM)
    for c in range(NSA_HEADS_PER_GROUP):
        for half, h in enumerate((c, c + NSA_HEADS_PER_GROUP)):
            put(_C_NQ + c * LANES + half * HEAD_DIM, _SRC_NQ + h * HEAD_DIM, HEAD_DIM)
    for dst, s0 in ((_C_KC, _SRC_KC), (_C_VC, _SRC_VC), (_C_KSL, _SRC_KSL), (_C_VSL, _SRC_VSL),
                    (_C_KW, _SRC_KW), (_C_VW, _SRC_VW)):
        put(dst, s0, NSA_KV_GROUPS * HEAD_DIM)
    put(_C_GL, _SRC_GL, NSA_HEADS * 3)
    return src


def _gather_columns(w, src, axis):
    pieces, i, n = [], 0, len(src)
    while i < n:
        j = i + 1
        if src[i] < 0:
            while j < n and src[j] < 0:
                j += 1
            shape = list(w.shape)
            shape[axis] = j - i
            pieces.append(jnp.zeros(shape, w.dtype))
        else:
            while j < n and src[j] == src[j - 1] + 1:
                j += 1
            pieces.append(lax.slice_in_dim(w, int(src[i]), int(src[i]) + j - i, axis=axis))
        i = j
    return jnp.concatenate(pieces, axis=axis)


def _mla_weight_layout(w_uq, w_ukv):
    dq = MLA_NOPE + MLA_ROPE
    src_q = -np.ones(MLA_HEADS * LANES, np.int64)
    src_kn = -np.ones(MLA_HEADS * LANES, np.int64)
    src_v = np.zeros(MLA_HEADS * MLA_V, np.int64)
    for h in range(MLA_HEADS):
        src_q[h * LANES:h * LANES + dq] = h * dq + np.arange(dq)
        src_kn[h * LANES:h * LANES + MLA_NOPE] = h * (MLA_NOPE + MLA_V) + np.arange(MLA_NOPE)
        src_v[h * MLA_V:(h + 1) * MLA_V] = h * (MLA_NOPE + MLA_V) + MLA_NOPE + np.arange(MLA_V)
    return (_gather_columns(w_uq, src_q, 1).astype(BF16),
            _gather_columns(w_ukv, src_kn, 1).astype(BF16),
            _gather_columns(w_ukv, src_v, 1).astype(BF16))


def _rope_tables(S):
    def cos_sin(dim):
        inv_freq = ROPE_THETA ** (-jnp.arange(0, dim, 2, dtype=F32) / dim)
        ang = jnp.arange(S, dtype=F32)[:, None] * inv_freq[None, :]
        return jnp.cos(ang), jnp.sin(ang)

    cos, sin = cos_sin(HEAD_DIM)
    zero = jnp.zeros_like(sin)
    c64 = jnp.concatenate([cos, cos] * 2, axis=1)
    sa64 = jnp.concatenate([-sin, zero] * 2, axis=1)
    sb64 = jnp.concatenate([zero, sin] * 2, axis=1)
    cos_m, sin_m = cos_sin(MLA_ROPE)
    ones = jnp.ones((S, MLA_NOPE), F32)
    z64 = jnp.zeros((S, MLA_NOPE), F32)
    z16 = jnp.zeros_like(sin_m)
    z32 = jnp.zeros((S, LANES - MLA_NOPE - MLA_ROPE), F32)
    cm = jnp.concatenate([ones, cos_m, cos_m, z32], axis=1)
    sam = jnp.concatenate([z64, -sin_m, z16, z32], axis=1)
    sbm = jnp.concatenate([z64, z16, sin_m, z32], axis=1)
    return c64, sa64, sb64, cm, sam, sbm


def _input_projection(h, w_big, q_norm, kv_norm, wuq, wkn, wv, tables, S, *, tm=KV_TILE):
    T = h.shape[0]
    n_pos = S // tm
    row = lambda w: pl.BlockSpec((tm, w), lambda i: (i, 0))
    full = lambda a: pl.BlockSpec(a.shape, lambda i: (0, 0))
    tab = pl.BlockSpec((tm, LANES), lambda i: (i % n_pos, 0))
    B = T // S
    specs, shapes = [], []

    def rows_out(w, dt):
        specs.append(row(w))
        shapes.append(jax.ShapeDtypeStruct((T, w), dt))

    def tile_t_out(c):
        specs.append(pl.BlockSpec((1, 1, c, tm), lambda i: (i // n_pos, i % n_pos, 0, 0)))
        shapes.append(jax.ShapeDtypeStruct((B, n_pos, c, tm), BF16))

    def residue_out(r, w):
        specs.append(pl.BlockSpec((1, r, tm // r, w), lambda i: (i // n_pos, 0, i % n_pos, 0)))
        shapes.append(jax.ShapeDtypeStruct((B, r, S // r, w), BF16))

    rows_out(MLA_HEADS * LANES, BF16)
    rows_out(MLA_HEADS * LANES, BF16)
    tile_t_out(MLA_HEADS * MLA_V)
    for _, r in DIL_PATTERNS:
        residue_out(r, DIL_HEADS * HEAD_DIM)
        residue_out(r, DIL_HEADS * HEAD_DIM)
        residue_out(r, DIL_HEADS * HEAD_DIM)
    rows_out(NSA_HEADS * HEAD_DIM, BF16)
    rows_out(NSA_HEADS * HEAD_DIM, BF16)
    rows_out(LANES, F32)
    rows_out(LANES, F32)
    rows_out(LANES, BF16)
    tile_t_out(LANES)
    rows_out(LANES, BF16)
    rows_out(LANES, BF16)
    rows_out(LANES, F32)
    qn = q_norm.reshape(1, -1)
    kvn = kv_norm.reshape(1, -1)
    return pl.pallas_call(
        _proj_kernel,
        grid=(T // tm,),
        in_specs=[row(D_MODEL), full(w_big), full(qn), full(kvn), full(wuq), full(wkn), full(wv)]
                 + [tab] * 6,
        out_specs=specs,
        out_shape=shapes,
        scratch_shapes=[pltpu.VMEM((3 * _DIL_KV_CHUNKS, tm, LANES), F32)],
        compiler_params=_cparams("parallel"),
        name="in_proj",
    )(h, w_big, qn, kvn, wuq, wkn, wv, *tables)


ONES_ROWS = 16


def _flash_step_t(s, vt1, m_ref, acc_ref, idx, cols=slice(None)):
    m_prev = m_ref[idx, :, cols]
    m_new = jnp.maximum(m_prev, jnp.max(s, axis=0, keepdims=True))
    alpha = jnp.exp2(m_prev - m_new)
    p = jnp.exp2(s - m_new).astype(BF16)
    acc_ref[idx, :, cols] = alpha * acc_ref[idx, :, cols] + _dot(vt1, p)
    m_ref[idx, :, cols] = m_new


def _init_softmax_state(m_ref, acc_ref):
    m_ref[...] = jnp.full(m_ref.shape, NEG_INF, F32)
    acc_ref[...] = jnp.zeros_like(acc_ref)


def _flash_state(slots, channels, queries):
    return [pltpu.VMEM((slots, 1, queries), F32), pltpu.VMEM((slots, channels + ONES_ROWS, queries), F32)]


def _split_pair_t(pair):
    t = pair.T
    top = lax.broadcasted_iota(jnp.int32, t.shape, 0) < HEAD_DIM
    zero = jnp.zeros_like(t)
    return jnp.where(top, t, zero), jnp.where(top, zero, t)


def _with_ones_rows(vt):
    return jnp.concatenate([vt, jnp.ones((ONES_ROWS, vt.shape[1]), vt.dtype)], axis=0)


def _normalised(acc, channels):
    return acc[:channels] / acc[channels:channels + 1]


def _mla_kernel(q_ref, k_ref, vt_ref, o_ref, qt_ref, m_ref, acc_ref, *, tq, tk):
    q0 = pl.program_id(1) * tq
    _init_softmax_state(m_ref, acc_ref)
    for h in range(MLA_HEADS):
        qt_ref[h] = q_ref[0, :, h * LANES:(h + 1) * LANES].T
    def tile(kt, masked, q_lo=0):
        start = pl.multiple_of(kt * tk, tk)
        cols = slice(q_lo, tq)
        if masked:
            kpos = start + lax.broadcasted_iota(jnp.int32, (tk, tq - q_lo), 0)
            qpos = q0 + q_lo + lax.broadcasted_iota(jnp.int32, (tk, tq - q_lo), 1)

        def scores(h):
            s = _dot(k_ref[0, pl.ds(start, tk), h * LANES:(h + 1) * LANES], qt_ref[h, :, cols])
            return jnp.where(kpos <= qpos, s, NEG_INF) if masked else s

        s_next = scores(0)
        for h in range(MLA_HEADS):
            s = s_next
            if h + 1 < MLA_HEADS:
                s_next = scores(h + 1)
            vt = vt_ref[0, kt, (h // 2) * LANES:(h // 2 + 1) * LANES, :]
            _flash_step_t(s, _with_ones_rows(vt), m_ref, acc_ref, h, cols)

    def body(kt, carry):
        tile(kt, False)
        return carry

    n_full = q0 // tk
    lax.fori_loop(0, n_full, body, 0)
    for d in range(tq // tk):
        tile(n_full + d, True, q_lo=d * tk)
    even_rows = lax.broadcasted_iota(jnp.int32, (LANES, tq), 0) < MLA_V
    for pair in range(MLA_HEADS // 2):
        h0, h1 = 2 * pair, 2 * pair + 1
        out_t = jnp.where(even_rows, _normalised(acc_ref[h0], LANES), _normalised(acc_ref[h1], LANES))
        o_ref[0, :, pair * LANES:(pair + 1) * LANES] = out_t.T.astype(o_ref.dtype)


def _mla_attention(q, k, vt, *, tq=1024, tk=KV_TILE):
    B, S, _ = q.shape
    return pl.pallas_call(
        functools.partial(_mla_kernel, tq=tq, tk=tk),
        grid=(B, S // tq),
        in_specs=[
            pl.BlockSpec((1, tq, MLA_HEADS * LANES), lambda b, i: (b, i, 0)),
            pl.BlockSpec((1, S, MLA_HEADS * LANES), lambda b, i: (b, 0, 0)),
            pl.BlockSpec((1, S // tk, MLA_HEADS * MLA_V, tk), lambda b, i: (b, 0, 0, 0)),
        ],
        out_specs=pl.BlockSpec((1, tq, MLA_HEADS * MLA_V), lambda b, i: (b, i, 0)),
        out_shape=jax.ShapeDtypeStruct((B, S, MLA_HEADS * MLA_V), BF16),
        scratch_shapes=[pltpu.VMEM((MLA_HEADS, LANES, tq), BF16)] + _flash_state(MLA_HEADS, LANES, tq),
        compiler_params=_cparams("parallel", "arbitrary"),
        name="mla_attn",
    )(q, k, vt)


def _dot_tn(a, b):
    return lax.dot_general(a, b, (((0,), (0,)), ((), ())), preferred_element_type=F32)


def _band_mask_t(q0, kstart, tq, span, window):
    kpos = kstart + lax.broadcasted_iota(jnp.int32, (span, tq), 0)
    qpos = q0 + lax.broadcasted_iota(jnp.int32, (span, tq), 1)
    dist = qpos - kpos
    return (dist >= 0) & (dist <= window)


def _dil_kernel(q_ref, k_ref, v_ref, o_ref, lse_ref, qt_ref, *, tq, sub, wpad, window):
    q0 = pl.program_id(2) * tq
    n = k_ref.shape[2]
    span = min(sub + wpad, n)
    even_rows = lax.broadcasted_iota(jnp.int32, (LANES, sub), 0) < HEAD_DIM
    windows = []
    for a in range(tq // sub):
        qs = q0 + a * sub
        kstart = pl.multiple_of(jnp.clip(qs - wpad, 0, n - span), LANES)
        windows.append((kstart, _band_mask_t(qs, kstart, sub, span, window)))
    chains = [(a, c, hh) for a in range(tq // sub) for c in range(DIL_HEADS // 2) for hh in range(2)]
    for a in range(tq // sub):
        for c in range(DIL_HEADS // 2):
            qt_ref[a, 2 * c], qt_ref[a, 2 * c + 1] = _split_pair_t(
                q_ref[0, 0, a * sub:(a + 1) * sub, c * LANES:(c + 1) * LANES])

    def scores(a, c, hh):
        kstart, mask = windows[a]
        k = k_ref[0, 0, pl.ds(kstart, span), c * LANES:(c + 1) * LANES]
        return jnp.where(mask, _dot(k, qt_ref[a, 2 * c + hh]), NEG_INF)

    s_next = scores(*chains[0])
    outs = []
    head_row = lax.broadcasted_iota(jnp.int32, (SUBLANES, sub), 0)
    lse_rows = jnp.zeros((SUBLANES, sub), F32)
    for i, (a, c, hh) in enumerate(chains):
        s = s_next
        if i + 1 < len(chains):
            s_next = scores(*chains[i + 1])
        m = jnp.max(s, axis=0, keepdims=True)
        e = jnp.exp2(s - m)
        l = jnp.sum(e, axis=0, keepdims=True)
        v = v_ref[0, 0, pl.ds(windows[a][0], span), c * LANES:(c + 1) * LANES]
        outs.append(_dot_tn(v, e.astype(BF16)) * (1.0 / l))
        lse_rows = jnp.where(head_row == 2 * c + hh, m * LN_2 + jnp.log(l), lse_rows)
        rows = slice(a * sub, (a + 1) * sub)
        if hh == 1:
            o_ref[0, 0, rows, c * LANES:(c + 1) * LANES] = (
                jnp.where(even_rows, outs[-2], outs[-1]).T.astype(o_ref.dtype))
        if (c, hh) == (DIL_HEADS // 2 - 1, 1):
            padded = jnp.concatenate([lse_rows, jnp.zeros((LANES - SUBLANES, sub), F32)], axis=0)
            lse_ref[0, 0, rows, :] = padded.T


def _dilated_pattern(q, k, v, window, dil, *, tq=512, sub=256):
    B, _, n, wq = q.shape
    wk = k.shape[-1]
    wsub = window // dil
    wpad = -(-wsub // LANES) * LANES
    tq = min(tq, n)
    sub = min(sub, tq)
    return pl.pallas_call(
        functools.partial(_dil_kernel, tq=tq, sub=sub, wpad=wpad, window=wsub),
        grid=(B, dil, n // tq),
        in_specs=[
            pl.BlockSpec((1, 1, tq, wq), lambda b, j, i: (b, j, i, 0)),
            pl.BlockSpec((1, 1, n, wk), lambda b, j, i: (b, j, 0, 0)),
            pl.BlockSpec((1, 1, n, wk), lambda b, j, i: (b, j, 0, 0)),
        ],
        out_specs=[pl.BlockSpec((1, 1, tq, wk), lambda b, j, i: (b, j, i, 0)),
                   pl.BlockSpec((1, 1, tq, LANES), lambda b, j, i: (b, j, i, 0))],
        out_shape=[jax.ShapeDtypeStruct((B, dil, n, wk), BF16), jax.ShapeDtypeStruct((B, dil, n, LANES), F32)],
        scratch_shapes=[pltpu.VMEM((tq // sub, DIL_HEADS, LANES, sub), BF16)],
        compiler_params=_cparams("parallel", "parallel", "arbitrary"),
        name=f"dilated_r{dil}",
    )(q, k, v)


def _compress_kernel(xk_ref, xv_ref, pos_ref, w1_ref, w2_ref, kc_ref, vct_ref):
    st = NSA_CMP_STRIDE
    n = xk_ref.shape[1] // st
    for t, x_ref in enumerate((xk_ref, xv_ref)):
        out = jnp.zeros((n, LANES), F32)
        for g in range(NSA_KV_GROUPS):
            first = jnp.zeros((n, NSA_CMP_HIDDEN), F32)
            second = jnp.zeros((n, NSA_CMP_HIDDEN), F32)
            for l in range(st):
                x = x_ref[0, pl.ds(l, n, stride=st), :]
                first = first + _dot((x + pos_ref[t, l]).astype(BF16), w1_ref[t, g, l])
                second = second + _dot((x + pos_ref[t, st + l]).astype(BF16), w1_ref[t, g, st + l])
            hid = first + pltpu.roll(second, n - 1, 0)
            hid = (hid * jax.nn.sigmoid(hid)).astype(BF16)
            out = out + _dot(hid, w2_ref[t, g])
        if t == 0:
            kc_ref[0] = out.astype(kc_ref.dtype)
        else:
            vct_ref[0] = out.T.astype(vct_ref.dtype)


def _nsa_compress(kc, vc, cmp_pos, cmp_w1, cmp_w2):
    B, S, _ = kc.shape
    G, d = NSA_KV_GROUPS, HEAD_DIM
    n = S // NSA_CMP_STRIDE
    w1 = cmp_w1.reshape(2, NSA_CMP_BLOCK, d, NSA_CMP_HIDDEN)
    zero1 = jnp.zeros_like(w1)
    w1g = jnp.stack([jnp.concatenate([w1, zero1], axis=2), jnp.concatenate([zero1, w1], axis=2)], axis=1)
    zero2 = jnp.zeros_like(cmp_w2)
    w2g = jnp.stack([jnp.concatenate([cmp_w2, zero2], axis=2), jnp.concatenate([zero2, cmp_w2], axis=2)], axis=1)
    pos = jnp.concatenate([cmp_pos] * G, axis=-1).reshape(2, NSA_CMP_BLOCK, 1, G * d)
    w1g, w2g = w1g.astype(BF16), w2g.astype(BF16)
    xspec = pl.BlockSpec((1, S, G * d), lambda b: (b, 0, 0))
    full = lambda a: pl.BlockSpec(a.shape, lambda b: (0,) * a.ndim)
    return pl.pallas_call(
        _compress_kernel,
        grid=(B,),
        in_specs=[xspec, xspec, full(pos), full(w1g), full(w2g)],
        out_specs=[pl.BlockSpec((1, n, G * d), lambda b: (b, 0, 0)),
                   pl.BlockSpec((1, G * d, n), lambda b: (b, 0, 0))],
        out_shape=[jax.ShapeDtypeStruct((B, n, G * d), BF16), jax.ShapeDtypeStruct((B, G * d, n), BF16)],
        compiler_params=_cparams("parallel"),
        name="nsa_compress",
    )(kc, vc, pos, w1g, w2g)


def _dot_01_by_f32(o01, p):
    p1 = p.astype(BF16)
    r1 = p - p1.astype(F32)
    p2 = r1.astype(BF16)
    p3 = (r1 - p2.astype(F32)).astype(BF16)
    return _dot(o01, p1) + _dot(o01, p2) + _dot(o01, p3)


SUBLANES = 8


def _count_outranking(sc_ref, cnt_ref, n_sources, tq):
    groups = SEL_LANES // SUBLANES
    sub = lax.broadcasted_iota(jnp.int32, (SUBLANES, tq), 0)
    cnt_ref[...] = jnp.zeros_like(cnt_ref)
    for bi in range(groups):
        @pl.when(bi * SUBLANES < n_sources)
        def _():
            src = sc_ref[bi * SUBLANES:(bi + 1) * SUBLANES, :]
            rows = [jnp.broadcast_to(src[ii:ii + 1, :], (SUBLANES, tq)) for ii in range(SUBLANES)]
            for r in range(groups):
                rsl = slice(r * SUBLANES, (r + 1) * SUBLANES)
                tgt = sc_ref[rsl, :]
                cnt = cnt_ref[rsl, :]
                for ii, row in enumerate(rows):
                    if r > bi:
                        ahead = row >= tgt
                    elif r < bi:
                        ahead = row > tgt
                    else:
                        ahead = (row > tgt) | ((row == tgt) & (sub > ii))
                    cnt = cnt + jnp.where(ahead, 1.0, 0.0)
                cnt_ref[rsl, :] = cnt


def _nsa_head_of_chunk(c, g):
    return c + g * NSA_HEADS_PER_GROUP


def _cmp_kernel(q_ref, kc_ref, vct_ref, ovt_ref, o_ref, bias_ref, qt_ref, sc_ref, cnt_ref, *, tq):
    q0 = pl.program_id(1) * tq
    n = kc_ref.shape[1]
    kc = kc_ref[0]
    vct = vct_ref[0]
    qpos = q0 + lax.broadcasted_iota(jnp.int32, (n, tq), 1)
    cmp_end = lax.broadcasted_iota(jnp.int32, (n, tq), 0) * NSA_CMP_STRIDE + (NSA_CMP_BLOCK - 1)
    visible = cmp_end <= qpos
    top = lax.broadcasted_iota(jnp.int32, (LANES, tq), 0) < HEAD_DIM
    for c in range(_NSA_Q_CHUNKS):
        qt_ref[_nsa_head_of_chunk(c, 0)], qt_ref[_nsa_head_of_chunk(c, 1)] = _split_pair_t(
            q_ref[0, :, c * LANES:(c + 1) * LANES])

    def scores(h):
        return jnp.where(visible, _dot(kc, qt_ref[h]), NEG_INF)

    psum = [jnp.zeros((n, tq), F32) for _ in range(NSA_KV_GROUPS)]
    outs = [None] * NSA_HEADS
    s_next = scores(0)
    for h in range(NSA_HEADS):
        s = s_next
        if h + 1 < NSA_HEADS:
            s_next = scores(h + 1)
        m = jnp.max(s, axis=0, keepdims=True)
        e = jnp.where(visible, jnp.exp(s - m), 0.0)
        l = jnp.maximum(jnp.sum(e, axis=0, keepdims=True), 1e-30)
        p = e * (1.0 / l)
        outs[h] = _dot(vct, p.astype(BF16))
        psum[h // NSA_HEADS_PER_GROUP] = psum[h // NSA_HEADS_PER_GROUP] + p
    for c in range(_NSA_Q_CHUNKS):
        pair_t = jnp.where(top, outs[_nsa_head_of_chunk(c, 0)], outs[_nsa_head_of_chunk(c, 1)])
        o_ref[0, :, c * LANES:(c + 1) * LANES] = pair_t.T.astype(o_ref.dtype)

    blk = lax.broadcasted_iota(jnp.int32, (SEL_LANES, tq), 0)
    cur = lax.shift_right_logical(q0 + lax.broadcasted_iota(jnp.int32, (SEL_LANES, tq), 1), 6)
    forced = (blk == 0) | (blk == cur) | (blk == cur - 1)
    n_causal = jnp.minimum(lax.shift_right_logical(q0 + tq - 1, 6) + 1, SEL_LANES)
    for g in range(NSA_KV_GROUPS):
        imp_t = _dot_01_by_f32(ovt_ref[...], psum[g])
        sc_ref[g] = jnp.where(forced, FORCE_SCORE, jnp.where(blk <= cur, imp_t, -FORCE_SCORE))
        _count_outranking(sc_ref.at[g], cnt_ref.at[g], n_causal, tq)
        bias_t = jnp.where(cnt_ref[g] < float(NSA_TOP_N), 0.0, UNSELECTED_BIAS)
        bias_ref[0, g] = bias_t.astype(bias_ref.dtype)


def _nsa_compressed_branch(nq, kcc, vcct, overlap, *, tq=256):
    B, S, wq = nq.shape
    G = NSA_KV_GROUPS
    n = kcc.shape[1]
    return pl.pallas_call(
        functools.partial(_cmp_kernel, tq=tq),
        grid=(B, S // tq),
        in_specs=[
            pl.BlockSpec((1, tq, wq), lambda b, i: (b, i, 0)),
            pl.BlockSpec((1, n, LANES), lambda b, i: (b, 0, 0)),
            pl.BlockSpec((1, LANES, n), lambda b, i: (b, 0, 0)),
            pl.BlockSpec((SEL_LANES, n), lambda b, i: (0, 0)),
        ],
        out_specs=[
            pl.BlockSpec((1, tq, wq), lambda b, i: (b, i, 0)),
            pl.BlockSpec((1, G, SEL_LANES, tq), lambda b, i: (b, 0, 0, i)),
        ],
        out_shape=[jax.ShapeDtypeStruct((B, S, wq), BF16),
                   jax.ShapeDtypeStruct((B, G, SEL_LANES, S), BF16)],
        scratch_shapes=[pltpu.VMEM((NSA_HEADS, LANES, tq), BF16),
                        pltpu.VMEM((G, SEL_LANES, tq), F32), pltpu.VMEM((G, SEL_LANES, tq), F32)],
        compiler_params=_cparams("parallel", "arbitrary"),
        name="nsa_cmp_topk",
    )(nq, kcc, vcct, overlap.T)


def _sel_kernel(q_ref, bias_ref, k_ref, vt_ref, oh_ref, o_ref, qa_ref, m_ref, acc_ref, *, tq, tk):
    q0 = pl.program_id(1) * tq
    for c in range(_NSA_Q_CHUNKS):
        h0, h1 = _nsa_head_of_chunk(c, 0), _nsa_head_of_chunk(c, 1)
        qa_ref[h0, :LANES, :], qa_ref[h1, :LANES, :] = _split_pair_t(q_ref[0, :, c * LANES:(c + 1) * LANES])
    for h in range(NSA_HEADS):
        qa_ref[h, LANES:, :] = bias_ref[0, h // NSA_HEADS_PER_GROUP]
    kofs = lax.broadcasted_iota(jnp.int32, (tk, tq), 0)
    qpos = q0 + lax.broadcasted_iota(jnp.int32, (tk, tq), 1)
    _init_softmax_state(m_ref, acc_ref)

    def tile(kt, masked):
        start = pl.multiple_of(kt * tk, tk)
        ka = jnp.concatenate([k_ref[0, pl.ds(start, tk), :], oh_ref[pl.ds(start, tk), :]], axis=1)
        vt1 = _with_ones_rows(vt_ref[0, kt])

        def scores(h):
            s = _dot(ka, qa_ref[h])
            return jnp.where((start + kofs) <= qpos, s, NEG_INF) if masked else s

        s_next = scores(0)
        for h in range(NSA_HEADS):
            s = s_next
            if h + 1 < NSA_HEADS:
                s_next = scores(h + 1)
            _flash_step_t(s, vt1, m_ref, acc_ref, h)

    def body(kt, carry):
        tile(kt, False)
        return carry

    n_full = q0 // tk
    lax.fori_loop(0, n_full, body, 0)
    tile(n_full, True)
    top = lax.broadcasted_iota(jnp.int32, (LANES, tq), 0) < HEAD_DIM
    for c in range(_NSA_Q_CHUNKS):
        pair_t = jnp.where(top, _normalised(acc_ref[_nsa_head_of_chunk(c, 0)], LANES),
                           _normalised(acc_ref[_nsa_head_of_chunk(c, 1)], LANES))
        o_ref[0, :, c * LANES:(c + 1) * LANES] = pair_t.T.astype(o_ref.dtype)


def _nsa_selected_branch(nqr, bias, ksl, vslt, onehot, *, tq=512, tk=KV_TILE):
    B, S, wq = nqr.shape
    G = NSA_KV_GROUPS
    return pl.pallas_call(
        functools.partial(_sel_kernel, tq=tq, tk=tk),
        grid=(B, S // tq),
        in_specs=[
            pl.BlockSpec((1, tq, wq), lambda b, i: (b, i, 0)),
            pl.BlockSpec((1, G, SEL_LANES, tq), lambda b, i: (b, 0, 0, i)),
            pl.BlockSpec((1, S, LANES), lambda b, i: (b, 0, 0)),
            pl.BlockSpec((1, S // tk, LANES, tk), lambda b, i: (b, 0, 0, 0)),
            pl.BlockSpec((S, SEL_LANES), lambda b, i: (0, 0)),
        ],
        out_specs=pl.BlockSpec((1, tq, wq), lambda b, i: (b, i, 0)),
        out_shape=jax.ShapeDtypeStruct((B, S, wq), BF16),
        scratch_shapes=[pltpu.VMEM((NSA_HEADS, LANES + SEL_LANES, tq), BF16)]
                       + _flash_state(NSA_HEADS, LANES, tq),
        compiler_params=_cparams("parallel", "arbitrary"),
        name="nsa_selected",
    )(nqr, bias, ksl, vslt, onehot)


def _win_kernel(q_ref, k_ref, v_ref, o_ref, qt_ref, *, tq, wpad, window):
    q0 = pl.program_id(1) * tq
    span = tq + wpad
    kstart = pl.multiple_of(jnp.maximum(q0 - wpad, 0), LANES)
    mask = _band_mask_t(q0, kstart, tq, span, window)
    k = k_ref[0, pl.ds(kstart, span), :]
    v = v_ref[0, pl.ds(kstart, span), :]
    top = lax.broadcasted_iota(jnp.int32, (LANES, tq), 0) < HEAD_DIM
    for c in range(_NSA_Q_CHUNKS):
        qt_ref[_nsa_head_of_chunk(c, 0)], qt_ref[_nsa_head_of_chunk(c, 1)] = _split_pair_t(
            q_ref[0, :, c * LANES:(c + 1) * LANES])

    def scores(h):
        return jnp.where(mask, _dot(k, qt_ref[h]), NEG_INF)

    outs = [None] * NSA_HEADS
    s_next = scores(0)
    for h in range(NSA_HEADS):
        s = s_next
        if h + 1 < NSA_HEADS:
            s_next = scores(h + 1)
        m = jnp.max(s, axis=0, keepdims=True)
        e = jnp.exp2(s - m)
        l = jnp.sum(e, axis=0, keepdims=True)
        outs[h] = _dot_tn(v, e.astype(BF16)) * (1.0 / l)
    for c in range(_NSA_Q_CHUNKS):
        pair_t = jnp.where(top, outs[_nsa_head_of_chunk(c, 0)], outs[_nsa_head_of_chunk(c, 1)])
        o_ref[0, :, c * LANES:(c + 1) * LANES] = pair_t.T.astype(o_ref.dtype)


def _nsa_window_branch(nqr, kw, vw, *, tq=256):
    B, S, wq = nqr.shape
    wpad = -(-NSA_WINDOW // LANES) * LANES
    return pl.pallas_call(
        functools.partial(_win_kernel, tq=tq, wpad=wpad, window=NSA_WINDOW),
        grid=(B, S // tq),
        in_specs=[
            pl.BlockSpec((1, tq, wq), lambda b, i: (b, i, 0)),
            pl.BlockSpec((1, S, LANES), lambda b, i: (b, 0, 0)),
            pl.BlockSpec((1, S, LANES), lambda b, i: (b, 0, 0)),
        ],
        out_specs=pl.BlockSpec((1, tq, wq), lambda b, i: (b, i, 0)),
        out_shape=jax.ShapeDtypeStruct((B, S, wq), BF16),
        scratch_shapes=[pltpu.VMEM((NSA_HEADS, LANES, tq), BF16)],
        compiler_params=_cparams("parallel", "arbitrary"),
        name="nsa_window",
    )(nqr, kw, vw)


def _selection_constants(S):
    n_cmp_rows = S // NSA_CMP_STRIDE
    c = np.arange(n_cmp_rows)
    cmp_start = c * NSA_CMP_STRIDE
    cmp_end = cmp_start + NSA_CMP_BLOCK - 1
    sel_start = np.arange(SEL_LANES) * NSA_SEL_BLOCK
    overlap = ((cmp_start[:, None] < sel_start[None, :] + NSA_SEL_BLOCK) &
               (cmp_end[:, None] >= sel_start[None, :]))
    n_cmp = (S - NSA_CMP_BLOCK) // NSA_CMP_STRIDE + 1
    overlap &= (c < n_cmp)[:, None]
    onehot = (np.arange(S)[:, None] // NSA_SEL_BLOCK) == np.arange(SEL_LANES)[None, :]
    return jnp.asarray(overlap, BF16), jnp.asarray(onehot, BF16)


def _out_kernel(*refs):
    n_pat = len(DIL_PATTERNS)
    x_ref, oa_ref = refs[:2]
    dil_refs = refs[2:2 + 2 * n_pat]
    (oc_ref, os_ref, ow_ref, gl_ref, wa_ref, wb_ref, wc_ref, g_ref, b_ref, o_ref,
     nat_ref) = refs[2 + 2 * n_pat:]
    tm = x_ref.shape[0]
    mix = _dot(oa_ref[...], wa_ref[...])

    lse_chunk = _DIL_KV_CHUNKS
    for p, (_, r) in enumerate(DIL_PATTERNS):
        o_ref_p, lse_ref_p = dil_refs[2 * p], dil_refs[2 * p + 1]
        for j in range(r):
            rows = pl.ds(j, tm // r, stride=r)
            for c in range(_DIL_KV_CHUNKS):
                nat_ref[p, c, rows, :] = o_ref_p[0, j, :, c * LANES:(c + 1) * LANES].astype(F32)
            nat_ref[p, lse_chunk, rows, :] = lse_ref_p[0, j]
    low_half = _lane_half((tm, LANES)) == 0
    for c in range(_DIL_KV_CHUNKS):
        outs = [nat_ref[p, c] for p in range(n_pat)]
        lses = [jnp.where(low_half, nat_ref[p, lse_chunk, :, 2 * c:2 * c + 1],
                          nat_ref[p, lse_chunk, :, 2 * c + 1:2 * c + 2]) for p in range(n_pat)]
        mx = functools.reduce(jnp.maximum, lses)
        es = [jnp.exp(l - mx) for l in lses]
        inv = 1.0 / functools.reduce(jnp.add, es)
        ob = functools.reduce(jnp.add, [(e * inv) * o for e, o in zip(es, outs)])
        mix = mix + _dot(ob.astype(BF16), wb_ref[c * LANES:(c + 1) * LANES, :])

    gl = gl_ref[...]
    for c in range(_NSA_Q_CHUNKS):
        sl = slice(c * LANES, (c + 1) * LANES)
        h0, h1 = _nsa_head_of_chunk(c, 0), _nsa_head_of_chunk(c, 1)
        oc = jnp.zeros((tm, LANES), F32)
        for k, branch_ref in enumerate((oc_ref, os_ref, ow_ref)):
            gate = jnp.where(low_half, gl[:, 3 * h0 + k:3 * h0 + k + 1], gl[:, 3 * h1 + k:3 * h1 + k + 1])
            oc = oc + gate * branch_ref[:, sl].astype(F32)
        mix = mix + _dot(oc.astype(BF16), wc_ref[sl, :])

    y = ALPHA * x_ref[...] + mix
    o_ref[...] = _layer_norm(y, g_ref[...], b_ref[...])


def _out_weight_layout(w_out):
    na = MLA_HEADS * MLA_V
    nb = DIL_HEADS * HEAD_DIM
    src = -np.ones(NSA_HEADS * HEAD_DIM, np.int64)
    for c in range(_NSA_Q_CHUNKS):
        for g in range(NSA_KV_GROUPS):
            d0 = c * LANES + g * HEAD_DIM
            src[d0:d0 + HEAD_DIM] = na + nb + _nsa_head_of_chunk(c, g) * HEAD_DIM + np.arange(HEAD_DIM)
    return (w_out[:na].astype(BF16), w_out[na:na + nb].astype(BF16),
            _gather_columns(w_out, src, 0).astype(BF16))


def _output_projection(x, oa, dil, oc, osl, ow, gl, wa, wb, wc, gain, bias, S, *, tm=512):
    T = x.shape[0]
    n_pos = S // tm
    row = lambda a: pl.BlockSpec((tm, a.shape[1]), lambda i: (i, 0))
    full = lambda a: pl.BlockSpec(a.shape, lambda i: (0, 0))
    residue = lambda a: pl.BlockSpec((1, a.shape[1], tm // a.shape[1], a.shape[3]),
                                     lambda i: (i // n_pos, 0, i % n_pos, 0))
    g2, b2 = gain.reshape(1, -1), bias.reshape(1, -1)
    dil_flat = [a for pair in dil for a in pair]
    rows = [oc, osl, ow, gl]
    consts = [wa, wb, wc, g2, b2]
    return pl.pallas_call(
        _out_kernel,
        grid=(T // tm,),
        in_specs=[row(x), row(oa)] + [residue(a) for a in dil_flat] + [row(a) for a in rows]
                 + [full(a) for a in consts],
        out_specs=pl.BlockSpec((tm, D_MODEL), lambda i: (i, 0)),
        out_shape=jax.ShapeDtypeStruct((T, D_MODEL), F32),
        scratch_shapes=[pltpu.VMEM((len(dil), _DIL_KV_CHUNKS + 1, tm, LANES), F32)],
        compiler_params=_cparams("parallel"),
        name="out_proj_ln",
    )(x, oa, *dil_flat, *rows, *consts)


def kernel(x, ffn_w_in, ffn_w_out, ln_gain, ln_bias, w_in, w_out, mla_q_norm, mla_kv_norm,
           mla_w_uq, mla_w_ukv, nsa_cmp_pos, nsa_cmp_w1, nsa_cmp_w2):
    B, S, D = x.shape
    assert D == D_MODEL and S % 2048 == 0 and S // NSA_SEL_BLOCK <= SEL_LANES
    T = B * S
    tables = _rope_tables(S)
    overlap, onehot = _selection_constants(S)
    proj_src = _proj_source_columns()
    bs = lambda t: t.reshape(B, S, t.shape[-1])

    xf = x.reshape(T, D)
    for l in range(DEPTH):
        xf = _ffn_ln(xf, ffn_w_in[l, 0].astype(BF16), ffn_w_out[l, 0].astype(BF16),
                     ln_gain[l, 0], ln_bias[l, 0])

        w_big = _gather_columns(w_in[l], proj_src, 1).astype(BF16)
        wuq, wkn, wv = _mla_weight_layout(mla_w_uq[l], mla_w_ukv[l])
        outs = _input_projection(xf, w_big, mla_q_norm[l], mla_kv_norm[l], wuq, wkn, wv, tables, S)
        qa, ka, va = outs[:3]
        n_dil = 3 * len(DIL_PATTERNS)
        dqkv = outs[3:3 + n_dil]
        nq, nqr, kc, vc, ksl, vsl, kw, vw, gl = outs[3 + n_dil:]

        oa = _mla_attention(bs(qa), bs(ka), va).reshape(T, -1)
        dil = [_dilated_pattern(*dqkv[3 * p:3 * p + 3], window, r)
               for p, (window, r) in enumerate(DIL_PATTERNS)]
        kcc, vcc = _nsa_compress(bs(kc), bs(vc), nsa_cmp_pos[l], nsa_cmp_w1[l], nsa_cmp_w2[l])
        oc, sel_bias = _nsa_compressed_branch(bs(nq), kcc, vcc, overlap)
        osl = _nsa_selected_branch(bs(nqr), sel_bias, bs(ksl), vsl, onehot)
        ow = _nsa_window_branch(bs(nqr), bs(kw), bs(vw))

        wa, wb, wc = _out_weight_layout(w_out[l])
        xf = _output_projection(xf, oa, dil, oc.reshape(T, -1), osl.reshape(T, -1), ow.reshape(T, -1),
                                gl, wa, wb, wc, ln_gain[l, 1], ln_bias[l, 1], S)

        xf = _ffn_ln(xf, ffn_w_in[l, 1].astype(BF16), ffn_w_out[l, 1].astype(BF16),
                     ln_gain[l, 2], ln_bias[l, 2])
    return xf.reshape(B, S, D)
```

```python
import functools

import numpy as np
import jax
import jax.numpy as jnp
from jax import lax
from jax.experimental import pallas as pl
from jax.experimental.pallas import tpu as pltpu

F32 = jnp.float32
BF16 = jnp.bfloat16

D_MODEL = 1024
DEPTH = 2
HEAD_DIM = 64
MLA_HEADS = 4
MLA_Q_LORA = 256
MLA_KV_LORA = 128
MLA_NOPE = 64
MLA_ROPE = 32
MLA_V = 64
DIL_HEADS = 6
DIL_PATTERNS = ((128, 1), (512, 4), (2048, 16))
NSA_HEADS = 6
NSA_KV_GROUPS = 2
NSA_HEADS_PER_GROUP = NSA_HEADS // NSA_KV_GROUPS
NSA_CMP_BLOCK = 32
NSA_CMP_STRIDE = 16
NSA_CMP_HIDDEN = 256
NSA_SEL_BLOCK = 64
NSA_TOP_N = 16
NSA_WINDOW = 512
D_FF = ((8 * D_MODEL // 3 + 255) // 256) * 256
ROPE_THETA = 10000.0
LN_EPS = 1e-5
RMS_EPS = 1e-6
NEG_INF = -1e30
FORCE_SCORE = 1e4
ALPHA = (2 * DEPTH) ** 0.25
LOG2_E = 1.4426950408889634
LN_2 = 0.6931471805599453

LANES = 128
SEL_LANES = 128
UNSELECTED_BIAS = -1e9
VMEM_LIMIT = 56 * 1024 * 1024
KV_TILE = 512

_SRC_CQ, _SRC_CKV, _SRC_KPE = 0, 256, 384
_SRC_DQ, _SRC_DK, _SRC_DV = 416, 800, 1184
_SRC_NQ = 1568
_SRC_KC, _SRC_VC, _SRC_KSL, _SRC_VSL, _SRC_KW, _SRC_VW = 1952, 2080, 2208, 2336, 2464, 2592
_SRC_GL = 2720
_C_CQ, _C_CKV, _C_KPE, _C_DQ, _C_DK, _C_DV, _C_NQ = 0, 256, 384, 512, 896, 1280, 1664
_C_KC, _C_VC, _C_KSL, _C_VSL, _C_KW, _C_VW, _C_GL = 2048, 2176, 2304, 2432, 2560, 2688, 2816
_PROJ_WIDTH = 2944


def _cparams(*sem):
    return pltpu.CompilerParams(dimension_semantics=sem, vmem_limit_bytes=VMEM_LIMIT)


def _layer_norm(y, g, b):
    mu = jnp.mean(y, axis=-1, keepdims=True)
    d = y - mu
    var = jnp.mean(d * d, axis=-1, keepdims=True)
    return d * lax.rsqrt(var + LN_EPS) * g + b


def _dot(a, b):
    return jnp.dot(a, b, preferred_element_type=F32)


def _dot_nt(a, b):
    return lax.dot_general(a, b, (((1,), (1,)), ((), ())), preferred_element_type=F32)


def _lane_half(shape):
    return lax.shift_right_logical(lax.broadcasted_iota(jnp.int32, shape, len(shape) - 1), 6)


def _ffn_kernel(x_ref, wg_ref, wu_ref, wo_ref, g_ref, b_ref, o_ref):
    x = x_ref[...]
    xb = x.astype(BF16)
    gate = _dot(xb, wg_ref[...])
    up = _dot(xb, wu_ref[...])
    h = (gate * jax.nn.sigmoid(gate) * up).astype(BF16)
    y = ALPHA * x + 0.5 * _dot(h, wo_ref[...])
    o_ref[...] = _layer_norm(y, g_ref[...], b_ref[...])


def _ffn_ln(x, w_in, w_out, gain, bias, *, tm=512):
    T = x.shape[0]
    resident = lambda shape, idx: pl.BlockSpec(shape, idx, pipeline_mode=pl.Buffered(1))
    return pl.pallas_call(
        _ffn_kernel,
        grid=(T // tm,),
        in_specs=[
            pl.BlockSpec((tm, D_MODEL), lambda i: (i, 0)),
            resident((D_MODEL, D_FF), lambda i: (0, 0)),
            resident((D_MODEL, D_FF), lambda i: (0, 1)),
            resident((D_FF, D_MODEL), lambda i: (0, 0)),
            pl.BlockSpec((1, D_MODEL), lambda i: (0, 0)),
            pl.BlockSpec((1, D_MODEL), lambda i: (0, 0)),
        ],
        out_specs=pl.BlockSpec((tm, D_MODEL), lambda i: (i, 0)),
        out_shape=jax.ShapeDtypeStruct((T, D_MODEL), F32),
        compiler_params=_cparams("parallel"),
        name="ffn_ln",
    )(x, w_in, w_in, w_out, gain.reshape(1, -1), bias.reshape(1, -1))


def _rope_chunk(x, c, sa, sb, half):
    return x * c + pltpu.roll(x, LANES - half, 1) * sa + pltpu.roll(x, half, 1) * sb


def _rms_norm(x, g):
    return x * lax.rsqrt(jnp.mean(x * x, axis=-1, keepdims=True) + RMS_EPS) * g


_DIL_KV_CHUNKS = DIL_HEADS * HEAD_DIM // LANES
_NSA_Q_CHUNKS = NSA_HEADS * HEAD_DIM // LANES


def _proj_kernel(*refs):
    (x_ref, w_ref, qn_ref, kvn_ref, wuq_ref, wkn_ref, wv_ref,
     c64_ref, sa64_ref, sb64_ref, cm_ref, sam_ref, sbm_ref, qa_ref, ka_ref, va_ref) = refs[:16]
    dil_refs = refs[16:16 + 3 * len(DIL_PATTERNS)]
    (nq_ref, nqr_ref, kc_ref, vc_ref, ksl_ref, vsl_ref, kw_ref, vw_ref, gl_ref,
     stage_ref) = refs[16 + 3 * len(DIL_PATTERNS):]
    tm = x_ref.shape[0]
    xb = x_ref[...].astype(BF16)
    c64, sa64, sb64 = c64_ref[...], sa64_ref[...], sb64_ref[...]
    cm, sam, sbm = cm_ref[...], sam_ref[...], sbm_ref[...]

    y = _dot(xb, w_ref[...])

    def proj(off, width):
        return y[:, off:off + width]

    def rope64(v):
        return _rope_chunk(v, c64, sa64, sb64, HEAD_DIM // 2)

    def rope_mla(v):
        return _rope_chunk(v, cm, sam, sbm, MLA_ROPE // 2)

    cq = _rms_norm(proj(_C_CQ, MLA_Q_LORA), qn_ref[...]).astype(BF16)
    q_raw = _dot(cq, wuq_ref[...])
    ckv = _rms_norm(proj(_C_CKV, MLA_KV_LORA), kvn_ref[...]).astype(BF16)
    k_nope = _dot(ckv, wkn_ref[...])
    va_ref[0, 0] = _dot(ckv, wv_ref[...]).T.astype(BF16)
    kpe = rope_mla(proj(_C_KPE, LANES))
    q_scale = (MLA_NOPE + MLA_ROPE) ** -0.5 * LOG2_E
    for h in range(MLA_HEADS):
        sl = slice(h * LANES, (h + 1) * LANES)
        qa_ref[:, sl] = (rope_mla(q_raw[:, sl]) * q_scale).astype(BF16)
        ka_ref[:, sl] = (k_nope[:, sl] + kpe).astype(BF16)

    scale = HEAD_DIM ** -0.5
    dq = proj(_C_DQ, DIL_HEADS * HEAD_DIM)
    dk = proj(_C_DK, DIL_HEADS * HEAD_DIM)
    dv = proj(_C_DV, DIL_HEADS * HEAD_DIM)
    for c in range(_DIL_KV_CHUNKS):
        stage_ref[c] = rope64(dq[:, c * LANES:(c + 1) * LANES]) * (scale * LOG2_E)
        stage_ref[_DIL_KV_CHUNKS + c] = rope64(dk[:, c * LANES:(c + 1) * LANES])
        stage_ref[2 * _DIL_KV_CHUNKS + c] = dv[:, c * LANES:(c + 1) * LANES]
    for p, (_, r) in enumerate(DIL_PATTERNS):
        for t in range(3):
            o_ref = dil_refs[3 * p + t]
            for j in range(r):
                for c in range(_DIL_KV_CHUNKS):
                    rows = stage_ref[t * _DIL_KV_CHUNKS + c, pl.ds(j, tm // r, stride=r), :]
                    o_ref[0, j, :, c * LANES:(c + 1) * LANES] = rows.astype(BF16)

    nq = proj(_C_NQ, NSA_HEADS * HEAD_DIM)
    nq_ref[...] = (nq * scale).astype(BF16)
    for c in range(NSA_HEADS_PER_GROUP):
        sl = slice(c * LANES, (c + 1) * LANES)
        nqr_ref[:, sl] = (rope64(nq[:, sl]) * (scale * LOG2_E)).astype(BF16)
    kc_ref[...] = proj(_C_KC, LANES)
    vc_ref[...] = proj(_C_VC, LANES)
    ksl_ref[...] = rope64(proj(_C_KSL, LANES)).astype(BF16)
    vsl_ref[0, 0] = proj(_C_VSL, LANES).T.astype(BF16)
    kw_ref[...] = rope64(proj(_C_KW, LANES)).astype(BF16)
    vw_ref[...] = proj(_C_VW, LANES).astype(BF16)
    gl_ref[...] = jax.nn.sigmoid(proj(_C_GL, LANES))


def _proj_source_columns():
    src = -np.ones(_PROJ_WIDTH, np.int64)

    def put(dst, s0, n):
        src[dst:dst + n] = s0 + np.arange(n)

    put(_C_CQ, _SRC_CQ, MLA_Q_LORA)
    put(_C_CKV, _SRC_CKV, MLA_KV_LORA)
    put(_C_KPE + MLA_NOPE, _SRC_KPE, MLA_ROPE)
    put(_C_DQ, _SRC_DQ, DIL_HEADS * HEAD_DIM)
    put(_C_DK, _SRC_DK, DIL_HEADS * HEAD_DIM)
    put(_C_DV, _SRC_DV, DIL_HEADS * HEAD_DIM)
    for c in range(NSA_HEADS_PER_GROUP):
        for half, h in enumerate((c, c + NSA_HEADS_PER_GROUP)):
            put(_C_NQ + c * LANES + half * HEAD_DIM, _SRC_NQ + h * HEAD_DIM, HEAD_DIM)
    for dst, s0 in ((_C_KC, _SRC_KC), (_C_VC, _SRC_VC), (_C_KSL, _SRC_KSL), (_C_VSL, _SRC_VSL),
                    (_C_KW, _SRC_KW), (_C_VW, _SRC_VW)):
        put(dst, s0, NSA_KV_GROUPS * HEAD_DIM)
    put(_C_GL, _SRC_GL, NSA_HEADS * 3)
    return src


def _gather_columns(w, src, axis):
    pieces, i, n = [], 0, len(src)
    while i < n:
        j = i + 1
        if src[i] < 0:
            while j < n and src[j] < 0:
                j += 1
            shape = list(w.shape)
            shape[axis] = j - i
            pieces.append(jnp.zeros(shape, w.dtype))
        else:
            while j < n and src[j] == src[j - 1] + 1:
                j += 1
            pieces.append(lax.slice_in_dim(w, int(src[i]), int(src[i]) + j - i, axis=axis))
        i = j
    return jnp.concatenate(pieces, axis=axis)


def _mla_weight_layout(w_uq, w_ukv):
    dq = MLA_NOPE + MLA_ROPE
    src_q = -np.ones(MLA_HEADS * LANES, np.int64)
    src_kn = -np.ones(MLA_HEADS * LANES, np.int64)
    src_v = np.zeros(MLA_HEADS * MLA_V, np.int64)
    for h in range(MLA_HEADS):
        src_q[h * LANES:h * LANES + dq] = h * dq + np.arange(dq)
        src_kn[h * LANES:h * LANES + MLA_NOPE] = h * (MLA_NOPE + MLA_V) + np.arange(MLA_NOPE)
        src_v[h * MLA_V:(h + 1) * MLA_V] = h * (MLA_NOPE + MLA_V) + MLA_NOPE + np.arange(MLA_V)
    return (_gather_columns(w_uq, src_q, 1).astype(BF16),
            _gather_columns(w_ukv, src_kn, 1).astype(BF16),
            _gather_columns(w_ukv, src_v, 1).astype(BF16))


def _rope_tables(S):
    def cos_sin(dim):
        inv_freq = ROPE_THETA ** (-jnp.arange(0, dim, 2, dtype=F32) / dim)
        ang = jnp.arange(S, dtype=F32)[:, None] * inv_freq[None, :]
        return jnp.cos(ang), jnp.sin(ang)

    cos, sin = cos_sin(HEAD_DIM)
    zero = jnp.zeros_like(sin)
    c64 = jnp.concatenate([cos, cos] * 2, axis=1)
    sa64 = jnp.concatenate([-sin, zero] * 2, axis=1)
    sb64 = jnp.concatenate([zero, sin] * 2, axis=1)
    cos_m, sin_m = cos_sin(MLA_ROPE)
    ones = jnp.ones((S, MLA_NOPE), F32)
    z64 = jnp.zeros((S, MLA_NOPE), F32)
    z16 = jnp.zeros_like(sin_m)
    z32 = jnp.zeros((S, LANES - MLA_NOPE - MLA_ROPE), F32)
    cm = jnp.concatenate([ones, cos_m, cos_m, z32], axis=1)
    sam = jnp.concatenate([z64, -sin_m, z16, z32], axis=1)
    sbm = jnp.concatenate([z64, z16, sin_m, z32], axis=1)
    return c64, sa64, sb64, cm, sam, sbm


def _input_projection(h, w_big, q_norm, kv_norm, wuq, wkn, wv, tables, S, *, tm=KV_TILE):
    T = h.shape[0]
    n_pos = S // tm
    row = lambda w: pl.BlockSpec((tm, w), lambda i: (i, 0))
    full = lambda a: pl.BlockSpec(a.shape, lambda i: (0, 0))
    tab = pl.BlockSpec((tm, LANES), lambda i: (i % n_pos, 0))
    B = T // S
    specs, shapes = [], []

    def rows_out(w, dt):
        specs.append(row(w))
        shapes.append(jax.ShapeDtypeStruct((T, w), dt))

    def tile_t_out(c):
        specs.append(pl.BlockSpec((1, 1, c, tm), lambda i: (i // n_pos, i % n_pos, 0, 0)))
        shapes.append(jax.ShapeDtypeStruct((B, n_pos, c, tm), BF16))

    def residue_out(r, w):
        specs.append(pl.BlockSpec((1, r, tm // r, w), lambda i: (i // n_pos, 0, i % n_pos, 0)))
        shapes.append(jax.ShapeDtypeStruct((B, r, S // r, w), BF16))

    rows_out(MLA_HEADS * LANES, BF16)
    rows_out(MLA_HEADS * LANES, BF16)
    tile_t_out(MLA_HEADS * MLA_V)
    for _, r in DIL_PATTERNS:
        residue_out(r, DIL_HEADS * HEAD_DIM)
        residue_out(r, DIL_HEADS * HEAD_DIM)
        residue_out(r, DIL_HEADS * HEAD_DIM)
    rows_out(NSA_HEADS * HEAD_DIM, BF16)
    rows_out(NSA_HEADS * HEAD_DIM, BF16)
    rows_out(LANES, F32)
    rows_out(LANES, F32)
    rows_out(LANES, BF16)
    tile_t_out(LANES)
    rows_out(LANES, BF16)
    rows_out(LANES, BF16)
    rows_out(LANES, F32)
    qn = q_norm.reshape(1, -1)
    kvn = kv_norm.reshape(1, -1)
    return pl.pallas_call(
        _proj_kernel,
        grid=(T // tm,),
        in_specs=[row(D_MODEL), full(w_big), full(qn), full(kvn), full(wuq), full(wkn), full(wv)]
                 + [tab] * 6,
        out_specs=specs,
        out_shape=shapes,
        scratch_shapes=[pltpu.VMEM((3 * _DIL_KV_CHUNKS, tm, LANES), F32)],
        compiler_params=_cparams("parallel"),
        name="in_proj",
    )(h, w_big, qn, kvn, wuq, wkn, wv, *tables)


ONES_ROWS = 16


def _flash_step_t(s, vt1s, m_ref, acc_ref, idx, cols=slice(None)):
    m_prev = m_ref[idx, :, cols]
    m_new = jnp.maximum(m_prev, jnp.max(s, axis=0, keepdims=True))
    alpha = jnp.exp2(m_prev - m_new)
    p = jnp.exp2(s - m_new).astype(BF16)
    pv = functools.reduce(jnp.add, [_dot(vt1, p[j * KV_TILE:(j + 1) * KV_TILE]) for j, vt1 in enumerate(vt1s)])
    acc_ref[idx, :, cols] = alpha * acc_ref[idx, :, cols] + pv
    m_ref[idx, :, cols] = m_new


def _causal_key_loop(q0, tq, tk, tile, *, pairs):
    n_full = q0 // tk
    if pairs:
        def pair(i, carry):
            tile(2 * i, 2, False, 0)
            return carry

        lax.fori_loop(0, lax.shift_right_logical(n_full, 1), pair, 0)

        @pl.when(lax.rem(n_full, 2) == 1)
        def _():
            tile(n_full - 1, 1, False, 0)
    else:
        def single(i, carry):
            tile(i, 1, False, 0)
            return carry

        lax.fori_loop(0, n_full, single, 0)

    for d in range(tq // tk):
        tile(n_full + d, 1, True, d * tk)


def _init_softmax_state(m_ref, acc_ref):
    m_ref[...] = jnp.full(m_ref.shape, NEG_INF, F32)
    acc_ref[...] = jnp.zeros_like(acc_ref)


def _flash_state(slots, channels, queries):
    return [pltpu.VMEM((slots, 1, queries), F32), pltpu.VMEM((slots, channels + ONES_ROWS, queries), F32)]


def _split_pair_t(pair):
    t = pair.T
    top = lax.broadcasted_iota(jnp.int32, t.shape, 0) < HEAD_DIM
    zero = jnp.zeros_like(t)
    return jnp.where(top, t, zero), jnp.where(top, zero, t)


def _with_ones_rows(vt):
    return jnp.concatenate([vt, jnp.ones((ONES_ROWS, vt.shape[1]), vt.dtype)], axis=0)


def _normalised(acc, channels):
    return acc[:channels] / acc[channels:channels + 1]


def _mla_kernel(q_ref, k_ref, vt_ref, o_ref, qt_ref, m_ref, acc_ref, *, tq, tk):
    q0 = pl.program_id(1) * tq
    _init_softmax_state(m_ref, acc_ref)
    for h in range(MLA_HEADS):
        qt_ref[h] = q_ref[0, :, h * LANES:(h + 1) * LANES].T

    def tile(kt, ntiles, masked, q_lo):
        start = pl.multiple_of(kt * tk, tk)
        keys = ntiles * tk
        cols = slice(q_lo, tq)
        if masked:
            kpos = start + lax.broadcasted_iota(jnp.int32, (keys, tq - q_lo), 0)
            qpos = q0 + q_lo + lax.broadcasted_iota(jnp.int32, (keys, tq - q_lo), 1)

        def scores(h):
            s = _dot(k_ref[0, pl.ds(start, keys), h * LANES:(h + 1) * LANES], qt_ref[h, :, cols])
            return jnp.where(kpos <= qpos, s, NEG_INF) if masked else s

        s_next = scores(0)
        for h in range(MLA_HEADS):
            s = s_next
            if h + 1 < MLA_HEADS:
                s_next = scores(h + 1)
            vt1s = [_with_ones_rows(vt_ref[0, kt + j, (h // 2) * LANES:(h // 2 + 1) * LANES, :])
                    for j in range(ntiles)]
            _flash_step_t(s, vt1s, m_ref, acc_ref, h, cols)

    _causal_key_loop(q0, tq, tk, tile, pairs=True)
    even_rows = lax.broadcasted_iota(jnp.int32, (LANES, tq), 0) < MLA_V
    for pair in range(MLA_HEADS // 2):
        h0, h1 = 2 * pair, 2 * pair + 1
        out_t = jnp.where(even_rows, _normalised(acc_ref[h0], LANES), _normalised(acc_ref[h1], LANES))
        o_ref[0, :, pair * LANES:(pair + 1) * LANES] = out_t.T.astype(o_ref.dtype)


def _mla_attention(q, k, vt, *, tq=1024, tk=KV_TILE):
    B, S, _ = q.shape
    return pl.pallas_call(
        functools.partial(_mla_kernel, tq=tq, tk=tk),
        grid=(B, S // tq),
        in_specs=[
            pl.BlockSpec((1, tq, MLA_HEADS * LANES), lambda b, i: (b, i, 0)),
            pl.BlockSpec((1, S, MLA_HEADS * LANES), lambda b, i: (b, 0, 0)),
            pl.BlockSpec((1, S // tk, MLA_HEADS * MLA_V, tk), lambda b, i: (b, 0, 0, 0)),
        ],
        out_specs=pl.BlockSpec((1, tq, MLA_HEADS * MLA_V), lambda b, i: (b, i, 0)),
        out_shape=jax.ShapeDtypeStruct((B, S, MLA_HEADS * MLA_V), BF16),
        scratch_shapes=[pltpu.VMEM((MLA_HEADS, LANES, tq), BF16)] + _flash_state(MLA_HEADS, LANES, tq),
        compiler_params=_cparams("parallel", "arbitrary"),
        name="mla_attn",
    )(q, k, vt)


def _dot_tn(a, b):
    return lax.dot_general(a, b, (((0,), (0,)), ((), ())), preferred_element_type=F32)


def _band_mask_t(q0, kstart, tq, span, window):
    kpos = kstart + lax.broadcasted_iota(jnp.int32, (span, tq), 0)
    qpos = q0 + lax.broadcasted_iota(jnp.int32, (span, tq), 1)
    dist = qpos - kpos
    return (dist >= 0) & (dist <= window)


def _dil_kernel(q_ref, k_ref, v_ref, o_ref, lse_ref, qt_ref, *, tq, sub, wpad, window):
    q0 = pl.program_id(2) * tq
    n = k_ref.shape[2]
    span = min(sub + wpad, n)
    even_rows = lax.broadcasted_iota(jnp.int32, (LANES, sub), 0) < HEAD_DIM
    windows = []
    for a in range(tq // sub):
        qs = q0 + a * sub
        kstart = pl.multiple_of(jnp.clip(qs - wpad, 0, n - span), LANES)
        windows.append((kstart, _band_mask_t(qs, kstart, sub, span, window)))
    chains = [(a, c, hh) for a in range(tq // sub) for c in range(DIL_HEADS // 2) for hh in range(2)]
    for a in range(tq // sub):
        for c in range(DIL_HEADS // 2):
            qt_ref[a, 2 * c], qt_ref[a, 2 * c + 1] = _split_pair_t(
                q_ref[0, 0, a * sub:(a + 1) * sub, c * LANES:(c + 1) * LANES])

    def scores(a, c, hh):
        kstart, mask = windows[a]
        k = k_ref[0, 0, pl.ds(kstart, span), c * LANES:(c + 1) * LANES]
        return jnp.where(mask, _dot(k, qt_ref[a, 2 * c + hh]), NEG_INF)

    s_next = scores(*chains[0])
    outs = []
    head_row = lax.broadcasted_iota(jnp.int32, (SUBLANES, sub), 0)
    lse_rows = jnp.zeros((SUBLANES, sub), F32)
    for i, (a, c, hh) in enumerate(chains):
        s = s_next
        if i + 1 < len(chains):
            s_next = scores(*chains[i + 1])
        m = jnp.max(s, axis=0, keepdims=True)
        e = jnp.exp2(s - m)
        l = jnp.sum(e, axis=0, keepdims=True)
        v = v_ref[0, 0, pl.ds(windows[a][0], span), c * LANES:(c + 1) * LANES]
        outs.append(_dot_tn(v, e.astype(BF16)) * (1.0 / l))
        lse_rows = jnp.where(head_row == 2 * c + hh, m * LN_2 + jnp.log(l), lse_rows)
        rows = slice(a * sub, (a + 1) * sub)
        if hh == 1:
            o_ref[0, 0, rows, c * LANES:(c + 1) * LANES] = (
                jnp.where(even_rows, outs[-2], outs[-1]).T.astype(o_ref.dtype))
        if (c, hh) == (DIL_HEADS // 2 - 1, 1):
            padded = jnp.concatenate([lse_rows, jnp.zeros((LANES - SUBLANES, sub), F32)], axis=0)
            lse_ref[0, 0, rows, :] = padded.T


def _dilated_pattern(q, k, v, window, dil, *, tq=512, sub=256):
    B, _, n, wq = q.shape
    wk = k.shape[-1]
    wsub = window // dil
    wpad = -(-wsub // LANES) * LANES
    tq = min(tq, n)
    sub = min(sub, tq)
    return pl.pallas_call(
        functools.partial(_dil_kernel, tq=tq, sub=sub, wpad=wpad, window=wsub),
        grid=(B, dil, n // tq),
        in_specs=[
            pl.BlockSpec((1, 1, tq, wq), lambda b, j, i: (b, j, i, 0)),
            pl.BlockSpec((1, 1, n, wk), lambda b, j, i: (b, j, 0, 0)),
            pl.BlockSpec((1, 1, n, wk), lambda b, j, i: (b, j, 0, 0)),
        ],
        out_specs=[pl.BlockSpec((1, 1, tq, wk), lambda b, j, i: (b, j, i, 0)),
                   pl.BlockSpec((1, 1, tq, LANES), lambda b, j, i: (b, j, i, 0))],
        out_shape=[jax.ShapeDtypeStruct((B, dil, n, wk), BF16), jax.ShapeDtypeStruct((B, dil, n, LANES), F32)],
        scratch_shapes=[pltpu.VMEM((tq // sub, DIL_HEADS, LANES, sub), BF16)],
        compiler_params=_cparams("parallel", "parallel", "arbitrary"),
        name=f"dilated_r{dil}",
    )(q, k, v)


def _compress_kernel(xk_ref, xv_ref, pos_ref, w1_ref, w2_ref, kc_ref, vct_ref):
    st = NSA_CMP_STRIDE
    n = xk_ref.shape[1] // st
    for t, x_ref in enumerate((xk_ref, xv_ref)):
        out = jnp.zeros((n, LANES), F32)
        for g in range(NSA_KV_GROUPS):
            first = jnp.zeros((n, NSA_CMP_HIDDEN), F32)
            second = jnp.zeros((n, NSA_CMP_HIDDEN), F32)
            for l in range(st):
                x = x_ref[0, pl.ds(l, n, stride=st), :]
                first = first + _dot((x + pos_ref[t, l]).astype(BF16), w1_ref[t, g, l])
                second = second + _dot((x + pos_ref[t, st + l]).astype(BF16), w1_ref[t, g, st + l])
            hid = first + pltpu.roll(second, n - 1, 0)
            hid = (hid * jax.nn.sigmoid(hid)).astype(BF16)
            out = out + _dot(hid, w2_ref[t, g])
        if t == 0:
            kc_ref[0] = out.astype(kc_ref.dtype)
        else:
            vct_ref[0] = out.T.astype(vct_ref.dtype)


def _nsa_compress(kc, vc, cmp_pos, cmp_w1, cmp_w2):
    B, S, _ = kc.shape
    G, d = NSA_KV_GROUPS, HEAD_DIM
    n = S // NSA_CMP_STRIDE
    w1 = cmp_w1.reshape(2, NSA_CMP_BLOCK, d, NSA_CMP_HIDDEN)
    zero1 = jnp.zeros_like(w1)
    w1g = jnp.stack([jnp.concatenate([w1, zero1], axis=2), jnp.concatenate([zero1, w1], axis=2)], axis=1)
    zero2 = jnp.zeros_like(cmp_w2)
    w2g = jnp.stack([jnp.concatenate([cmp_w2, zero2], axis=2), jnp.concatenate([zero2, cmp_w2], axis=2)], axis=1)
    pos = jnp.concatenate([cmp_pos] * G, axis=-1).reshape(2, NSA_CMP_BLOCK, 1, G * d)
    w1g, w2g = w1g.astype(BF16), w2g.astype(BF16)
    xspec = pl.BlockSpec((1, S, G * d), lambda b: (b, 0, 0))
    full = lambda a: pl.BlockSpec(a.shape, lambda b: (0,) * a.ndim)
    return pl.pallas_call(
        _compress_kernel,
        grid=(B,),
        in_specs=[xspec, xspec, full(pos), full(w1g), full(w2g)],
        out_specs=[pl.BlockSpec((1, n, G * d), lambda b: (b, 0, 0)),
                   pl.BlockSpec((1, G * d, n), lambda b: (b, 0, 0))],
        out_shape=[jax.ShapeDtypeStruct((B, n, G * d), BF16), jax.ShapeDtypeStruct((B, G * d, n), BF16)],
        compiler_params=_cparams("parallel"),
        name="nsa_compress",
    )(kc, vc, pos, w1g, w2g)


def _dot_01_by_f32(o01, p):
    p1 = p.astype(BF16)
    r1 = p - p1.astype(F32)
    p2 = r1.astype(BF16)
    p3 = (r1 - p2.astype(F32)).astype(BF16)
    return _dot(o01, p1) + _dot(o01, p2) + _dot(o01, p3)


SUBLANES = 8


def _count_outranking(sc_ref, cnt_ref, n_sources, tq):
    groups = SEL_LANES // SUBLANES
    sub = lax.broadcasted_iota(jnp.int32, (SUBLANES, tq), 0)
    cnt_ref[...] = jnp.zeros_like(cnt_ref)
    for bi in range(groups):
        @pl.when(bi * SUBLANES < n_sources)
        def _():
            src = sc_ref[bi * SUBLANES:(bi + 1) * SUBLANES, :]
            rows = [jnp.broadcast_to(src[ii:ii + 1, :], (SUBLANES, tq)) for ii in range(SUBLANES)]
            for r in range(groups):
                rsl = slice(r * SUBLANES, (r + 1) * SUBLANES)
                tgt = sc_ref[rsl, :]
                cnt = cnt_ref[rsl, :]
                for ii, row in enumerate(rows):
                    if r > bi:
                        ahead = row >= tgt
                    elif r < bi:
                        ahead = row > tgt
                    else:
                        ahead = (row > tgt) | ((row == tgt) & (sub > ii))
                    cnt = cnt + jnp.where(ahead, 1.0, 0.0)
                cnt_ref[rsl, :] = cnt


def _nsa_head_of_chunk(c, g):
    return c + g * NSA_HEADS_PER_GROUP


def _cmp_kernel(q_ref, kc_ref, vct_ref, ovt_ref, o_ref, bias_ref, qt_ref, sc_ref, cnt_ref, *, tq):
    q0 = pl.program_id(1) * tq
    n = kc_ref.shape[1]
    top = lax.broadcasted_iota(jnp.int32, (LANES, tq), 0) < HEAD_DIM
    for c in range(_NSA_Q_CHUNKS):
        qt_ref[_nsa_head_of_chunk(c, 0)], qt_ref[_nsa_head_of_chunk(c, 1)] = _split_pair_t(
            q_ref[0, :, c * LANES:(c + 1) * LANES])
    blk = lax.broadcasted_iota(jnp.int32, (SEL_LANES, tq), 0)
    cur = lax.shift_right_logical(q0 + lax.broadcasted_iota(jnp.int32, (SEL_LANES, tq), 1), 6)
    forced = (blk == 0) | (blk == cur) | (blk == cur - 1)

    def attend(nk):
        kc = kc_ref[0, :nk, :]
        vct = vct_ref[0, :, :nk]
        qpos = q0 + lax.broadcasted_iota(jnp.int32, (nk, tq), 1)
        cmp_end = lax.broadcasted_iota(jnp.int32, (nk, tq), 0) * NSA_CMP_STRIDE + (NSA_CMP_BLOCK - 1)
        visible = cmp_end <= qpos

        def scores(h):
            return jnp.where(visible, _dot(kc, qt_ref[h]), NEG_INF)

        psum = [jnp.zeros((nk, tq), F32) for _ in range(NSA_KV_GROUPS)]
        outs = [None] * NSA_HEADS
        s_next = scores(0)
        for h in range(NSA_HEADS):
            s = s_next
            if h + 1 < NSA_HEADS:
                s_next = scores(h + 1)
            m = jnp.max(s, axis=0, keepdims=True)
            e = jnp.where(visible, jnp.exp(s - m), 0.0)
            l = jnp.maximum(jnp.sum(e, axis=0, keepdims=True), 1e-30)
            p = e * (1.0 / l)
            outs[h] = _dot(vct, p.astype(BF16))
            psum[h // NSA_HEADS_PER_GROUP] = psum[h // NSA_HEADS_PER_GROUP] + p
        for c in range(_NSA_Q_CHUNKS):
            pair_t = jnp.where(top, outs[_nsa_head_of_chunk(c, 0)], outs[_nsa_head_of_chunk(c, 1)])
            o_ref[0, :, c * LANES:(c + 1) * LANES] = pair_t.T.astype(o_ref.dtype)
        for g in range(NSA_KV_GROUPS):
            imp_t = _dot_01_by_f32(ovt_ref[:, :nk], psum[g])
            sc_ref[g] = jnp.where(forced, FORCE_SCORE, jnp.where(blk <= cur, imp_t, -FORCE_SCORE))

    step = LANES if n % LANES == 0 else n
    needed = lax.div(q0 + tq + NSA_CMP_STRIDE * step - 1, NSA_CMP_STRIDE * step)
    for v in range(1, n // step + 1):
        @pl.when(jnp.minimum(needed, n // step) == v)
        def _(nk=v * step):
            attend(nk)

    n_causal = jnp.minimum(lax.shift_right_logical(q0 + tq - 1, 6) + 1, SEL_LANES)
    for g in range(NSA_KV_GROUPS):
        _count_outranking(sc_ref.at[g], cnt_ref.at[g], n_causal, tq)
        bias_t = jnp.where(cnt_ref[g] < float(NSA_TOP_N), 0.0, UNSELECTED_BIAS)
        bias_ref[0, g] = bias_t.astype(bias_ref.dtype)


def _nsa_compressed_branch(nq, kcc, vcct, overlap, *, tq=256):
    B, S, wq = nq.shape
    G = NSA_KV_GROUPS
    n = kcc.shape[1]
    return pl.pallas_call(
        functools.partial(_cmp_kernel, tq=tq),
        grid=(B, S // tq),
        in_specs=[
            pl.BlockSpec((1, tq, wq), lambda b, i: (b, i, 0)),
            pl.BlockSpec((1, n, LANES), lambda b, i: (b, 0, 0)),
            pl.BlockSpec((1, LANES, n), lambda b, i: (b, 0, 0)),
            pl.BlockSpec((SEL_LANES, n), lambda b, i: (0, 0)),
        ],
        out_specs=[
            pl.BlockSpec((1, tq, wq), lambda b, i: (b, i, 0)),
            pl.BlockSpec((1, G, SEL_LANES, tq), lambda b, i: (b, 0, 0, i)),
        ],
        out_shape=[jax.ShapeDtypeStruct((B, S, wq), BF16),
                   jax.ShapeDtypeStruct((B, G, SEL_LANES, S), BF16)],
        scratch_shapes=[pltpu.VMEM((NSA_HEADS, LANES, tq), BF16),
                        pltpu.VMEM((G, SEL_LANES, tq), F32), pltpu.VMEM((G, SEL_LANES, tq), F32)],
        compiler_params=_cparams("parallel", "arbitrary"),
        name="nsa_cmp_topk",
    )(nq, kcc, vcct, overlap.T)


def _sel_kernel(q_ref, bias_ref, k_ref, vt_ref, oh_ref, o_ref, qa_ref, m_ref, acc_ref, *, tq, tk):
    q0 = pl.program_id(1) * tq
    for c in range(_NSA_Q_CHUNKS):
        h0, h1 = _nsa_head_of_chunk(c, 0), _nsa_head_of_chunk(c, 1)
        qa_ref[h0, :LANES, :], qa_ref[h1, :LANES, :] = _split_pair_t(q_ref[0, :, c * LANES:(c + 1) * LANES])
    for h in range(NSA_HEADS):
        qa_ref[h, LANES:, :] = bias_ref[0, h // NSA_HEADS_PER_GROUP]
    _init_softmax_state(m_ref, acc_ref)

    def tile(kt, ntiles, masked, q_lo):
        start = pl.multiple_of(kt * tk, tk)
        keys = ntiles * tk
        cols = slice(q_lo, tq)
        if masked:
            kpos = start + lax.broadcasted_iota(jnp.int32, (keys, tq - q_lo), 0)
            qpos = q0 + q_lo + lax.broadcasted_iota(jnp.int32, (keys, tq - q_lo), 1)
        ka = jnp.concatenate([k_ref[0, pl.ds(start, keys), :], oh_ref[pl.ds(start, keys), :]], axis=1)
        vt1s = [_with_ones_rows(vt_ref[0, kt + j]) for j in range(ntiles)]

        def scores(h):
            s = _dot(ka, qa_ref[h, :, cols])
            return jnp.where(kpos <= qpos, s, NEG_INF) if masked else s

        s_next = scores(0)
        for h in range(NSA_HEADS):
            s = s_next
            if h + 1 < NSA_HEADS:
                s_next = scores(h + 1)
            _flash_step_t(s, vt1s, m_ref, acc_ref, h, cols)

    _causal_key_loop(q0, tq, tk, tile, pairs=False)
    top = lax.broadcasted_iota(jnp.int32, (LANES, tq), 0) < HEAD_DIM
    for c in range(_NSA_Q_CHUNKS):
        pair_t = jnp.where(top, _normalised(acc_ref[_nsa_head_of_chunk(c, 0)], LANES),
                           _normalised(acc_ref[_nsa_head_of_chunk(c, 1)], LANES))
        o_ref[0, :, c * LANES:(c + 1) * LANES] = pair_t.T.astype(o_ref.dtype)


def _nsa_selected_branch(nqr, bias, ksl, vslt, onehot, *, tq=512, tk=KV_TILE):
    B, S, wq = nqr.shape
    G = NSA_KV_GROUPS
    return pl.pallas_call(
        functools.partial(_sel_kernel, tq=tq, tk=tk),
        grid=(B, S // tq),
        in_specs=[
            pl.BlockSpec((1, tq, wq), lambda b, i: (b, i, 0)),
            pl.BlockSpec((1, G, SEL_LANES, tq), lambda b, i: (b, 0, 0, i)),
            pl.BlockSpec((1, S, LANES), lambda b, i: (b, 0, 0)),
            pl.BlockSpec((1, S // tk, LANES, tk), lambda b, i: (b, 0, 0, 0)),
            pl.BlockSpec((S, SEL_LANES), lambda b, i: (0, 0)),
        ],
        out_specs=pl.BlockSpec((1, tq, wq), lambda b, i: (b, i, 0)),
        out_shape=jax.ShapeDtypeStruct((B, S, wq), BF16),
        scratch_shapes=[pltpu.VMEM((NSA_HEADS, LANES + SEL_LANES, tq), BF16)]
                       + _flash_state(NSA_HEADS, LANES, tq),
        compiler_params=_cparams("parallel", "arbitrary"),
        name="nsa_selected",
    )(nqr, bias, ksl, vslt, onehot)


def _win_kernel(q_ref, k_ref, v_ref, o_ref, qt_ref, *, tq, wpad, window):
    q0 = pl.program_id(1) * tq
    span = tq + wpad
    kstart = pl.multiple_of(jnp.maximum(q0 - wpad, 0), LANES)
    mask = _band_mask_t(q0, kstart, tq, span, window)
    k = k_ref[0, pl.ds(kstart, span), :]
    v = v_ref[0, pl.ds(kstart, span), :]
    top = lax.broadcasted_iota(jnp.int32, (LANES, tq), 0) < HEAD_DIM
    for c in range(_NSA_Q_CHUNKS):
        qt_ref[_nsa_head_of_chunk(c, 0)], qt_ref[_nsa_head_of_chunk(c, 1)] = _split_pair_t(
            q_ref[0, :, c * LANES:(c + 1) * LANES])

    def scores(h):
        return jnp.where(mask, _dot(k, qt_ref[h]), NEG_INF)

    outs = [None] * NSA_HEADS
    s_next = scores(0)
    for h in range(NSA_HEADS):
        s = s_next
        if h + 1 < NSA_HEADS:
            s_next = scores(h + 1)
        m = jnp.max(s, axis=0, keepdims=True)
        e = jnp.exp2(s - m)
        l = jnp.sum(e, axis=0, keepdims=True)
        outs[h] = _dot_tn(v, e.astype(BF16)) * (1.0 / l)
    for c in range(_NSA_Q_CHUNKS):
        pair_t = jnp.where(top, outs[_nsa_head_of_chunk(c, 0)], outs[_nsa_head_of_chunk(c, 1)])
        o_ref[0, :, c * LANES:(c + 1) * LANES] = pair_t.T.astype(o_ref.dtype)


def _nsa_window_branch(nqr, kw, vw, *, tq=256):
    B, S, wq = nqr.shape
    wpad = -(-NSA_WINDOW // LANES) * LANES
    return pl.pallas_call(
        functools.partial(_win_kernel, tq=tq, wpad=wpad, window=NSA_WINDOW),
        grid=(B, S // tq),
        in_specs=[
            pl.BlockSpec((1, tq, wq), lambda b, i: (b, i, 0)),
            pl.BlockSpec((1, S, LANES), lambda b, i: (b, 0, 0)),
            pl.BlockSpec((1, S, LANES), lambda b, i: (b, 0, 0)),
        ],
        out_specs=pl.BlockSpec((1, tq, wq), lambda b, i: (b, i, 0)),
        out_shape=jax.ShapeDtypeStruct((B, S, wq), BF16),
        scratch_shapes=[pltpu.VMEM((NSA_HEADS, LANES, tq), BF16)],
        compiler_params=_cparams("parallel", "arbitrary"),
        name="nsa_window",
    )(nqr, kw, vw)


def _selection_constants(S):
    n_cmp_rows = S // NSA_CMP_STRIDE
    c = np.arange(n_cmp_rows)
    cmp_start = c * NSA_CMP_STRIDE
    cmp_end = cmp_start + NSA_CMP_BLOCK - 1
    sel_start = np.arange(SEL_LANES) * NSA_SEL_BLOCK
    overlap = ((cmp_start[:, None] < sel_start[None, :] + NSA_SEL_BLOCK) &
               (cmp_end[:, None] >= sel_start[None, :]))
    n_cmp = (S - NSA_CMP_BLOCK) // NSA_CMP_STRIDE + 1
    overlap &= (c < n_cmp)[:, None]
    onehot = (np.arange(S)[:, None] // NSA_SEL_BLOCK) == np.arange(SEL_LANES)[None, :]
    return jnp.asarray(overlap, BF16), jnp.asarray(onehot, BF16)


def _out_kernel(*refs):
    n_pat = len(DIL_PATTERNS)
    x_ref, oa_ref = refs[:2]
    dil_refs = refs[2:2 + 2 * n_pat]
    (oc_ref, os_ref, ow_ref, gl_ref, wa_ref, wb_ref, wc_ref, g_ref, b_ref, o_ref,
     nat_ref) = refs[2 + 2 * n_pat:]
    tm = x_ref.shape[0]
    mix = _dot(oa_ref[...], wa_ref[...])

    lse_chunk = _DIL_KV_CHUNKS
    for p, (_, r) in enumerate(DIL_PATTERNS):
        o_ref_p, lse_ref_p = dil_refs[2 * p], dil_refs[2 * p + 1]
        for j in range(r):
            rows = pl.ds(j, tm // r, stride=r)
            for c in range(_DIL_KV_CHUNKS):
                nat_ref[p, c, rows, :] = o_ref_p[0, j, :, c * LANES:(c + 1) * LANES].astype(F32)
            nat_ref[p, lse_chunk, rows, :] = lse_ref_p[0, j]
    low_half = _lane_half((tm, LANES)) == 0
    for c in range(_DIL_KV_CHUNKS):
        outs = [nat_ref[p, c] for p in range(n_pat)]
        lses = [jnp.where(low_half, nat_ref[p, lse_chunk, :, 2 * c:2 * c + 1],
                          nat_ref[p, lse_chunk, :, 2 * c + 1:2 * c + 2]) for p in range(n_pat)]
        mx = functools.reduce(jnp.maximum, lses)
        es = [jnp.exp(l - mx) for l in lses]
        inv = 1.0 / functools.reduce(jnp.add, es)
        ob = functools.reduce(jnp.add, [(e * inv) * o for e, o in zip(es, outs)])
        mix = mix + _dot(ob.astype(BF16), wb_ref[c * LANES:(c + 1) * LANES, :])

    gl = gl_ref[...]
    for c in range(_NSA_Q_CHUNKS):
        sl = slice(c * LANES, (c + 1) * LANES)
        h0, h1 = _nsa_head_of_chunk(c, 0), _nsa_head_of_chunk(c, 1)
        oc = jnp.zeros((tm, LANES), F32)
        for k, branch_ref in enumerate((oc_ref, os_ref, ow_ref)):
            gate = jnp.where(low_half, gl[:, 3 * h0 + k:3 * h0 + k + 1], gl[:, 3 * h1 + k:3 * h1 + k + 1])
            oc = oc + gate * branch_ref[:, sl].astype(F32)
        mix = mix + _dot(oc.astype(BF16), wc_ref[sl, :])

    y = ALPHA * x_ref[...] + mix
    o_ref[...] = _layer_norm(y, g_ref[...], b_ref[...])


def _out_weight_layout(w_out):
    na = MLA_HEADS * MLA_V
    nb = DIL_HEADS * HEAD_DIM
    src = -np.ones(NSA_HEADS * HEAD_DIM, np.int64)
    for c in range(_NSA_Q_CHUNKS):
        for g in range(NSA_KV_GROUPS):
            d0 = c * LANES + g * HEAD_DIM
            src[d0:d0 + HEAD_DIM] = na + nb + _nsa_head_of_chunk(c, g) * HEAD_DIM + np.arange(HEAD_DIM)
    return (w_out[:na].astype(BF16), w_out[na:na + nb].astype(BF16),
            _gather_columns(w_out, src, 0).astype(BF16))


def _output_projection(x, oa, dil, oc, osl, ow, gl, wa, wb, wc, gain, bias, S, *, tm=512):
    T = x.shape[0]
    n_pos = S // tm
    row = lambda a: pl.BlockSpec((tm, a.shape[1]), lambda i: (i, 0))
    full = lambda a: pl.BlockSpec(a.shape, lambda i: (0, 0))
    residue = lambda a: pl.BlockSpec((1, a.shape[1], tm // a.shape[1], a.shape[3]),
                                     lambda i: (i // n_pos, 0, i % n_pos, 0))
    g2, b2 = gain.reshape(1, -1), bias.reshape(1, -1)
    dil_flat = [a for pair in dil for a in pair]
    rows = [oc, osl, ow, gl]
    consts = [wa, wb, wc, g2, b2]
    return pl.pallas_call(
        _out_kernel,
        grid=(T // tm,),
        in_specs=[row(x), row(oa)] + [residue(a) for a in dil_flat] + [row(a) for a in rows]
                 + [full(a) for a in consts],
        out_specs=pl.BlockSpec((tm, D_MODEL), lambda i: (i, 0)),
        out_shape=jax.ShapeDtypeStruct((T, D_MODEL), F32),
        scratch_shapes=[pltpu.VMEM((len(dil), _DIL_KV_CHUNKS + 1, tm, LANES), F32)],
        compiler_params=_cparams("parallel"),
        name="out_proj_ln",
    )(x, oa, *dil_flat, *rows, *consts)


def kernel(x, ffn_w_in, ffn_w_out, ln_gain, ln_bias, w_in, w_out, mla_q_norm, mla_kv_norm,
           mla_w_uq, mla_w_ukv, nsa_cmp_pos, nsa_cmp_w1, nsa_cmp_w2):
    B, S, D = x.shape
    assert D == D_MODEL and S % 2048 == 0 and S // NSA_SEL_BLOCK <= SEL_LANES
    T = B * S
    tables = _rope_tables(S)
    overlap, onehot = _selection_constants(S)
    proj_src = _proj_source_columns()
    bs = lambda t: t.reshape(B, S, t.shape[-1])

    xf = x.reshape(T, D)
    for l in range(DEPTH):
        xf = _ffn_ln(xf, ffn_w_in[l, 0].astype(BF16), ffn_w_out[l, 0].astype(BF16),
                     ln_gain[l, 0], ln_bias[l, 0])

        w_big = _gather_columns(w_in[l], proj_src, 1).astype(BF16)
        wuq, wkn, wv = _mla_weight_layout(mla_w_uq[l], mla_w_ukv[l])
        outs = _input_projection(xf, w_big, mla_q_norm[l], mla_kv_norm[l], wuq, wkn, wv, tables, S)
        qa, ka, va = outs[:3]
        n_dil = 3 * len(DIL_PATTERNS)
        dqkv = outs[3:3 + n_dil]
        nq, nqr, kc, vc, ksl, vsl, kw, vw, gl = outs[3 + n_dil:]

        oa = _mla_attention(bs(qa), bs(ka), va).reshape(T, -1)
        dil = [_dilated_pattern(*dqkv[3 * p:3 * p + 3], window, r)
               for p, (window, r) in enumerate(DIL_PATTERNS)]
        kcc, vcc = _nsa_compress(bs(kc), bs(vc), nsa_cmp_pos[l], nsa_cmp_w1[l], nsa_cmp_w2[l])
        oc, sel_bias = _nsa_compressed_branch(bs(nq), kcc, vcc, overlap)
        osl = _nsa_selected_branch(bs(nqr), sel_bias, bs(ksl), vsl, onehot)
        ow = _nsa_window_branch(bs(nqr), bs(kw), bs(vw))

        wa, wb, wc = _out_weight_layout(w_out[l])
        xf = _output_projection(xf, oa, dil, oc.reshape(T, -1), osl.reshape(T, -1), ow.reshape(T, -1),
                                gl, wa, wb, wc, ln_gain[l, 1], ln_bias[l, 1], S)

        xf = _ffn_ln(xf, ffn_w_in[l, 1].astype(BF16), ffn_w_out[l, 1].astype(BF16),
                     ln_gain[l, 2], ln_bias[l, 2])
    return xf.reshape(B, S, D)
```

```python
import functools

import numpy as np
import jax
import jax.numpy as jnp
from jax import lax
from jax.experimental import pallas as pl
from jax.experimental.pallas import tpu as pltpu

F32 = jnp.float32
BF16 = jnp.bfloat16

D_MODEL = 1024
DEPTH = 2
HEAD_DIM = 64
MLA_HEADS = 4
MLA_Q_LORA = 256
MLA_KV_LORA = 128
MLA_NOPE = 64
MLA_ROPE = 32
MLA_V = 64
DIL_HEADS = 6
DIL_PATTERNS = ((128, 1), (512, 4), (2048, 16))
NSA_HEADS = 6
NSA_KV_GROUPS = 2
NSA_HEADS_PER_GROUP = NSA_HEADS // NSA_KV_GROUPS
NSA_CMP_BLOCK = 32
NSA_CMP_STRIDE = 16
NSA_CMP_HIDDEN = 256
NSA_SEL_BLOCK = 64
NSA_TOP_N = 16
NSA_WINDOW = 512
D_FF = ((8 * D_MODEL // 3 + 255) // 256) * 256
ROPE_THETA = 10000.0
LN_EPS = 1e-5
RMS_EPS = 1e-6
NEG_INF = -1e30
FORCE_SCORE = 1e4
ALPHA = (2 * DEPTH) ** 0.25
LOG2_E = 1.4426950408889634
LN_2 = 0.6931471805599453

LANES = 128
SEL_LANES = 128
UNSELECTED_BIAS = -1e9
VMEM_LIMIT = 56 * 1024 * 1024
KV_TILE = 512

_SRC_CQ, _SRC_CKV, _SRC_KPE = 0, 256, 384
_SRC_DQ, _SRC_DK, _SRC_DV = 416, 800, 1184
_SRC_NQ = 1568
_SRC_KC, _SRC_VC, _SRC_KSL, _SRC_VSL, _SRC_KW, _SRC_VW = 1952, 2080, 2208, 2336, 2464, 2592
_SRC_GL = 2720
_C_CQ, _C_CKV, _C_KPE, _C_DQ, _C_DK, _C_DV, _C_NQ = 0, 256, 384, 512, 896, 1280, 1664
_C_KC, _C_VC, _C_KSL, _C_VSL, _C_KW, _C_VW, _C_GL = 2048, 2176, 2304, 2432, 2560, 2688, 2816
_PROJ_WIDTH = 2944


def _cparams(*sem):
    return pltpu.CompilerParams(dimension_semantics=sem, vmem_limit_bytes=VMEM_LIMIT)


def _layer_norm(y, g, b):
    mu = jnp.mean(y, axis=-1, keepdims=True)
    d = y - mu
    var = jnp.mean(d * d, axis=-1, keepdims=True)
    return d * lax.rsqrt(var + LN_EPS) * g + b


def _dot(a, b):
    return jnp.dot(a, b, preferred_element_type=F32)


def _dot_nt(a, b):
    return lax.dot_general(a, b, (((1,), (1,)), ((), ())), preferred_element_type=F32)


def _lane_half(shape):
    return lax.shift_right_logical(lax.broadcasted_iota(jnp.int32, shape, len(shape) - 1), 6)


def _ffn_kernel(x_ref, wg_ref, wu_ref, wo_ref, g_ref, b_ref, o_ref):
    x = x_ref[...]
    xb = x.astype(BF16)
    gate = _dot(xb, wg_ref[...])
    up = _dot(xb, wu_ref[...])
    h = (gate * jax.nn.sigmoid(gate) * up).astype(BF16)
    y = ALPHA * x + 0.5 * _dot(h, wo_ref[...])
    o_ref[...] = _layer_norm(y, g_ref[...], b_ref[...])


def _ffn_ln(x, w_in, w_out, gain, bias, *, tm=512):
    T = x.shape[0]
    resident = lambda shape, idx: pl.BlockSpec(shape, idx, pipeline_mode=pl.Buffered(1))
    return pl.pallas_call(
        _ffn_kernel,
        grid=(T // tm,),
        in_specs=[
            pl.BlockSpec((tm, D_MODEL), lambda i: (i, 0)),
            resident((D_MODEL, D_FF), lambda i: (0, 0)),
            resident((D_MODEL, D_FF), lambda i: (0, 1)),
            resident((D_FF, D_MODEL), lambda i: (0, 0)),
            pl.BlockSpec((1, D_MODEL), lambda i: (0, 0)),
            pl.BlockSpec((1, D_MODEL), lambda i: (0, 0)),
        ],
        out_specs=pl.BlockSpec((tm, D_MODEL), lambda i: (i, 0)),
        out_shape=jax.ShapeDtypeStruct((T, D_MODEL), F32),
        compiler_params=_cparams("parallel"),
        name="ffn_ln",
    )(x, w_in, w_in, w_out, gain.reshape(1, -1), bias.reshape(1, -1))


def _rope_chunk(x, c, sa, sb, half):
    return x * c + pltpu.roll(x, LANES - half, 1) * sa + pltpu.roll(x, half, 1) * sb


def _rms_norm(x, g):
    return x * lax.rsqrt(jnp.mean(x * x, axis=-1, keepdims=True) + RMS_EPS) * g


_DIL_KV_CHUNKS = DIL_HEADS * HEAD_DIM // LANES
_NSA_Q_CHUNKS = NSA_HEADS * HEAD_DIM // LANES


def _proj_kernel(*refs):
    (x_ref, w_ref, qn_ref, kvn_ref, wuq_ref, wkn_ref, wv_ref,
     c64_ref, sa64_ref, sb64_ref, cm_ref, sam_ref, sbm_ref, qa_ref, ka_ref, va_ref) = refs[:16]
    dil_refs = refs[16:16 + 3 * len(DIL_PATTERNS)]
    (nq_ref, nqr_ref, kc_ref, vc_ref, ksl_ref, vsl_ref, kw_ref, vw_ref, gl_ref,
     stage_ref) = refs[16 + 3 * len(DIL_PATTERNS):]
    tm = x_ref.shape[0]
    xb = x_ref[...].astype(BF16)
    c64, sa64, sb64 = c64_ref[...], sa64_ref[...], sb64_ref[...]
    cm, sam, sbm = cm_ref[...], sam_ref[...], sbm_ref[...]

    y = _dot(xb, w_ref[...])

    def proj(off, width):
        return y[:, off:off + width]

    def rope64(v):
        return _rope_chunk(v, c64, sa64, sb64, HEAD_DIM // 2)

    def rope_mla(v):
        return _rope_chunk(v, cm, sam, sbm, MLA_ROPE // 2)

    cq = _rms_norm(proj(_C_CQ, MLA_Q_LORA), qn_ref[...]).astype(BF16)
    q_raw = _dot(cq, wuq_ref[...])
    ckv = _rms_norm(proj(_C_CKV, MLA_KV_LORA), kvn_ref[...]).astype(BF16)
    k_nope = _dot(ckv, wkn_ref[...])
    va_ref[0, 0] = _dot(ckv, wv_ref[...]).T.astype(BF16)
    kpe = rope_mla(proj(_C_KPE, LANES))
    q_scale = (MLA_NOPE + MLA_ROPE) ** -0.5 * LOG2_E
    for h in range(MLA_HEADS):
        sl = slice(h * LANES, (h + 1) * LANES)
        qa_ref[:, sl] = (rope_mla(q_raw[:, sl]) * q_scale).astype(BF16)
        ka_ref[:, sl] = (k_nope[:, sl] + kpe).astype(BF16)

    scale = HEAD_DIM ** -0.5
    dq = proj(_C_DQ, DIL_HEADS * HEAD_DIM)
    dk = proj(_C_DK, DIL_HEADS * HEAD_DIM)
    dv = proj(_C_DV, DIL_HEADS * HEAD_DIM)
    for c in range(_DIL_KV_CHUNKS):
        stage_ref[c] = rope64(dq[:, c * LANES:(c + 1) * LANES]) * (scale * LOG2_E)
        stage_ref[_DIL_KV_CHUNKS + c] = rope64(dk[:, c * LANES:(c + 1) * LANES])
        stage_ref[2 * _DIL_KV_CHUNKS + c] = dv[:, c * LANES:(c + 1) * LANES]
    for p, (_, r) in enumerate(DIL_PATTERNS):
        for t in range(3):
            o_ref = dil_refs[3 * p + t]
            for j in range(r):
                for c in range(_DIL_KV_CHUNKS):
                    rows = stage_ref[t * _DIL_KV_CHUNKS + c, pl.ds(j, tm // r, stride=r), :]
                    o_ref[0, j, :, c * LANES:(c + 1) * LANES] = rows.astype(BF16)

    nq = proj(_C_NQ, NSA_HEADS * HEAD_DIM)
    nq_ref[...] = (nq * scale).astype(BF16)
    for c in range(NSA_HEADS_PER_GROUP):
        sl = slice(c * LANES, (c + 1) * LANES)
        nqr_ref[:, sl] = (rope64(nq[:, sl]) * (scale * LOG2_E)).astype(BF16)
    kc_ref[...] = proj(_C_KC, LANES)
    vc_ref[...] = proj(_C_VC, LANES)
    ksl_ref[...] = rope64(proj(_C_KSL, LANES)).astype(BF16)
    vsl_ref[0, 0] = proj(_C_VSL, LANES).T.astype(BF16)
    kw_ref[...] = rope64(proj(_C_KW, LANES)).astype(BF16)
    vw_ref[...] = proj(_C_VW, LANES).astype(BF16)
    gl_ref[...] = jax.nn.sigmoid(proj(_C_GL, LANES))


def _proj_source_columns():
    src = -np.ones(_PROJ_WIDTH, np.int64)

    def put(dst, s0, n):
        src[dst:dst + n] = s0 + np.arange(n)

    put(_C_CQ, _SRC_CQ, MLA_Q_LORA)
    put(_C_CKV, _SRC_CKV, MLA_KV_LORA)
    put(_C_KPE + MLA_NOPE, _SRC_KPE, MLA_ROPE)
    put(_C_DQ, _SRC_DQ, DIL_HEADS * HEAD_DIM)
    put(_C_DK, _SRC_DK, DIL_HEADS * HEAD_DIM)
    put(_C_DV, _SRC_DV, DIL_HEADS * HEAD_DIM)
    for c in range(NSA_HEADS_PER_GROUP):
        for half, h in enumerate((c, c + NSA_HEADS_PER_GROUP)):
            put(_C_NQ + c * LANES + half * HEAD_DIM, _SRC_NQ + h * HEAD_DIM, HEAD_DIM)
    for dst, s0 in ((_C_KC, _SRC_KC), (_C_VC, _SRC_VC), (_C_KSL, _SRC_KSL), (_C_VSL, _SRC_VSL),
                    (_C_KW, _SRC_KW), (_C_VW, _SRC_VW)):
        put(dst, s0, NSA_KV_GROUPS * HEAD_DIM)
    put(_C_GL, _SRC_GL, NSA_HEADS * 3)
    return src


def _gather_columns(w, src, axis):
    pieces, i, n = [], 0, len(src)
    while i < n:
        j = i + 1
        if src[i] < 0:
            while j < n and src[j] < 0:
                j += 1
            shape = list(w.shape)
            shape[axis] = j - i
            pieces.append(jnp.zeros(shape, w.dtype))
        else:
            while j < n and src[j] == src[j - 1] + 1:
                j += 1
            pieces.append(lax.slice_in_dim(w, int(src[i]), int(src[i]) + j - i, axis=axis))
        i = j
    return jnp.concatenate(pieces, axis=axis)


def _mla_weight_layout(w_uq, w_ukv):
    dq = MLA_NOPE + MLA_ROPE
    src_q = -np.ones(MLA_HEADS * LANES, np.int64)
    src_kn = -np.ones(MLA_HEADS * LANES, np.int64)
    src_v = np.zeros(MLA_HEADS * MLA_V, np.int64)
    for h in range(MLA_HEADS):
        src_q[h * LANES:h * LANES + dq] = h * dq + np.arange(dq)
        src_kn[h * LANES:h * LANES + MLA_NOPE] = h * (MLA_NOPE + MLA_V) + np.arange(MLA_NOPE)
        src_v[h * MLA_V:(h + 1) * MLA_V] = h * (MLA_NOPE + MLA_V) + MLA_NOPE + np.arange(MLA_V)
    return (_gather_columns(w_uq, src_q, 1).astype(BF16),
            _gather_columns(w_ukv, src_kn, 1).astype(BF16),
            _gather_columns(w_ukv, src_v, 1).astype(BF16))


def _rope_tables(S):
    def cos_sin(dim):
        inv_freq = ROPE_THETA ** (-jnp.arange(0, dim, 2, dtype=F32) / dim)
        ang = jnp.arange(S, dtype=F32)[:, None] * inv_freq[None, :]
        return jnp.cos(ang), jnp.sin(ang)

    cos, sin = cos_sin(HEAD_DIM)
    zero = jnp.zeros_like(sin)
    c64 = jnp.concatenate([cos, cos] * 2, axis=1)
    sa64 = jnp.concatenate([-sin, zero] * 2, axis=1)
    sb64 = jnp.concatenate([zero, sin] * 2, axis=1)
    cos_m, sin_m = cos_sin(MLA_ROPE)
    ones = jnp.ones((S, MLA_NOPE), F32)
    z64 = jnp.zeros((S, MLA_NOPE), F32)
    z16 = jnp.zeros_like(sin_m)
    z32 = jnp.zeros((S, LANES - MLA_NOPE - MLA_ROPE), F32)
    cm = jnp.concatenate([ones, cos_m, cos_m, z32], axis=1)
    sam = jnp.concatenate([z64, -sin_m, z16, z32], axis=1)
    sbm = jnp.concatenate([z64, z16, sin_m, z32], axis=1)
    return c64, sa64, sb64, cm, sam, sbm


def _input_projection(h, w_big, q_norm, kv_norm, wuq, wkn, wv, tables, S, *, tm=KV_TILE):
    T = h.shape[0]
    n_pos = S // tm
    row = lambda w: pl.BlockSpec((tm, w), lambda i: (i, 0))
    full = lambda a: pl.BlockSpec(a.shape, lambda i: (0, 0))
    tab = pl.BlockSpec((tm, LANES), lambda i: (i % n_pos, 0))
    B = T // S
    specs, shapes = [], []

    def rows_out(w, dt):
        specs.append(row(w))
        shapes.append(jax.ShapeDtypeStruct((T, w), dt))

    def tile_t_out(c):
        specs.append(pl.BlockSpec((1, 1, c, tm), lambda i: (i // n_pos, i % n_pos, 0, 0)))
        shapes.append(jax.ShapeDtypeStruct((B, n_pos, c, tm), BF16))

    def residue_out(r, w):
        specs.append(pl.BlockSpec((1, r, tm // r, w), lambda i: (i // n_pos, 0, i % n_pos, 0)))
        shapes.append(jax.ShapeDtypeStruct((B, r, S // r, w), BF16))

    rows_out(MLA_HEADS * LANES, BF16)
    rows_out(MLA_HEADS * LANES, BF16)
    tile_t_out(MLA_HEADS * MLA_V)
    for _, r in DIL_PATTERNS:
        residue_out(r, DIL_HEADS * HEAD_DIM)
        residue_out(r, DIL_HEADS * HEAD_DIM)
        residue_out(r, DIL_HEADS * HEAD_DIM)
    rows_out(NSA_HEADS * HEAD_DIM, BF16)
    rows_out(NSA_HEADS * HEAD_DIM, BF16)
    rows_out(LANES, F32)
    rows_out(LANES, F32)
    rows_out(LANES, BF16)
    tile_t_out(LANES)
    rows_out(LANES, BF16)
    rows_out(LANES, BF16)
    rows_out(LANES, F32)
    qn = q_norm.reshape(1, -1)
    kvn = kv_norm.reshape(1, -1)
    return pl.pallas_call(
        _proj_kernel,
        grid=(T // tm,),
        in_specs=[row(D_MODEL), full(w_big), full(qn), full(kvn), full(wuq), full(wkn), full(wv)]
                 + [tab] * 6,
        out_specs=specs,
        out_shape=shapes,
        scratch_shapes=[pltpu.VMEM((3 * _DIL_KV_CHUNKS, tm, LANES), F32)],
        compiler_params=_cparams("parallel"),
        name="in_proj",
    )(h, w_big, qn, kvn, wuq, wkn, wv, *tables)


ONES_ROWS = 16


def _flash_step_t(s, vt1s, m_ref, acc_ref, idx, cols=slice(None)):
    m_prev = m_ref[idx, :, cols]
    m_new = jnp.maximum(m_prev, jnp.max(s, axis=0, keepdims=True))
    alpha = jnp.exp2(m_prev - m_new)
    p = jnp.exp2(s - m_new).astype(BF16)
    pv = functools.reduce(jnp.add, [_dot(vt1, p[j * KV_TILE:(j + 1) * KV_TILE]) for j, vt1 in enumerate(vt1s)])
    acc_ref[idx, :, cols] = alpha * acc_ref[idx, :, cols] + pv
    m_ref[idx, :, cols] = m_new


def _causal_key_loop(q0, tq, tk, tile, *, pairs):
    n_full = q0 // tk
    if pairs:
        def pair(i, carry):
            tile(2 * i, 2, False, 0)
            return carry

        lax.fori_loop(0, lax.shift_right_logical(n_full, 1), pair, 0)

        @pl.when(lax.rem(n_full, 2) == 1)
        def _():
            tile(n_full - 1, 1, False, 0)
    else:
        def single(i, carry):
            tile(i, 1, False, 0)
            return carry

        lax.fori_loop(0, n_full, single, 0)

    for d in range(tq // tk):
        tile(n_full + d, 1, True, d * tk)


def _init_softmax_state(m_ref, acc_ref):
    m_ref[...] = jnp.full(m_ref.shape, NEG_INF, F32)
    acc_ref[...] = jnp.zeros_like(acc_ref)


def _flash_state(slots, channels, queries):
    return [pltpu.VMEM((slots, 1, queries), F32), pltpu.VMEM((slots, channels + ONES_ROWS, queries), F32)]


def _split_pair_t(pair):
    t = pair.T
    top = lax.broadcasted_iota(jnp.int32, t.shape, 0) < HEAD_DIM
    zero = jnp.zeros_like(t)
    return jnp.where(top, t, zero), jnp.where(top, zero, t)


def _with_ones_rows(vt):
    return jnp.concatenate([vt, jnp.ones((ONES_ROWS, vt.shape[1]), vt.dtype)], axis=0)


def _normalised(acc, channels):
    return acc[:channels] / acc[channels:channels + 1]


def _mla_kernel(q_ref, k_ref, vt_ref, o_ref, qt_ref, m_ref, acc_ref, *, tq, tk):
    q0 = pl.program_id(1) * tq
    _init_softmax_state(m_ref, acc_ref)
    for h in range(MLA_HEADS):
        qt_ref[h] = q_ref[0, :, h * LANES:(h + 1) * LANES].T

    def tile(kt, ntiles, masked, q_lo):
        start = pl.multiple_of(kt * tk, tk)
        keys = ntiles * tk
        cols = slice(q_lo, tq)
        if masked:
            kpos = start + lax.broadcasted_iota(jnp.int32, (keys, tq - q_lo), 0)
            qpos = q0 + q_lo + lax.broadcasted_iota(jnp.int32, (keys, tq - q_lo), 1)

        def scores(h):
            s = _dot(k_ref[0, pl.ds(start, keys), h * LANES:(h + 1) * LANES], qt_ref[h, :, cols])
            return jnp.where(kpos <= qpos, s, NEG_INF) if masked else s

        s_next = scores(0)
        for h in range(MLA_HEADS):
            s = s_next
            if h + 1 < MLA_HEADS:
                s_next = scores(h + 1)
            vt1s = [_with_ones_rows(vt_ref[0, kt + j, (h // 2) * LANES:(h // 2 + 1) * LANES, :])
                    for j in range(ntiles)]
            _flash_step_t(s, vt1s, m_ref, acc_ref, h, cols)

    _causal_key_loop(q0, tq, tk, tile, pairs=True)
    even_rows = lax.broadcasted_iota(jnp.int32, (LANES, tq), 0) < MLA_V
    for pair in range(MLA_HEADS // 2):
        h0, h1 = 2 * pair, 2 * pair + 1
        out_t = jnp.where(even_rows, _normalised(acc_ref[h0], LANES), _normalised(acc_ref[h1], LANES))
        o_ref[0, :, pair * LANES:(pair + 1) * LANES] = out_t.T.astype(o_ref.dtype)


def _mla_attention(q, k, vt, *, tq=1024, tk=KV_TILE):
    B, S, _ = q.shape
    return pl.pallas_call(
        functools.partial(_mla_kernel, tq=tq, tk=tk),
        grid=(B, S // tq),
        in_specs=[
            pl.BlockSpec((1, tq, MLA_HEADS * LANES), lambda b, i: (b, i, 0)),
            pl.BlockSpec((1, S, MLA_HEADS * LANES), lambda b, i: (b, 0, 0)),
            pl.BlockSpec((1, S // tk, MLA_HEADS * MLA_V, tk), lambda b, i: (b, 0, 0, 0)),
        ],
        out_specs=pl.BlockSpec((1, tq, MLA_HEADS * MLA_V), lambda b, i: (b, i, 0)),
        out_shape=jax.ShapeDtypeStruct((B, S, MLA_HEADS * MLA_V), BF16),
        scratch_shapes=[pltpu.VMEM((MLA_HEADS, LANES, tq), BF16)] + _flash_state(MLA_HEADS, LANES, tq),
        compiler_params=_cparams("parallel", "arbitrary"),
        name="mla_attn",
    )(q, k, vt)


def _dot_tn(a, b):
    return lax.dot_general(a, b, (((0,), (0,)), ((), ())), preferred_element_type=F32)


def _band_mask_t(q0, kstart, tq, span, window):
    kpos = kstart + lax.broadcasted_iota(jnp.int32, (span, tq), 0)
    qpos = q0 + lax.broadcasted_iota(jnp.int32, (span, tq), 1)
    dist = qpos - kpos
    return (dist >= 0) & (dist <= window)


def _dil_kernel(q_ref, k_ref, v_ref, o_ref, lse_ref, qt_ref, *, tq, sub, wpad, window):
    q0 = pl.program_id(2) * tq
    n = k_ref.shape[2]
    span = min(sub + wpad, n)
    even_rows = lax.broadcasted_iota(jnp.int32, (LANES, sub), 0) < HEAD_DIM
    windows = []
    for a in range(tq // sub):
        qs = q0 + a * sub
        kstart = pl.multiple_of(jnp.clip(qs - wpad, 0, n - span), LANES)
        windows.append((kstart, _band_mask_t(qs, kstart, sub, span, window)))
    chains = [(a, c, hh) for a in range(tq // sub) for c in range(DIL_HEADS // 2) for hh in range(2)]
    for a in range(tq // sub):
        for c in range(DIL_HEADS // 2):
            qt_ref[a, 2 * c], qt_ref[a, 2 * c + 1] = _split_pair_t(
                q_ref[0, 0, a * sub:(a + 1) * sub, c * LANES:(c + 1) * LANES])

    def scores(a, c, hh):
        kstart, mask = windows[a]
        k = k_ref[0, 0, pl.ds(kstart, span), c * LANES:(c + 1) * LANES]
        return jnp.where(mask, _dot(k, qt_ref[a, 2 * c + hh]), NEG_INF)

    s_next = scores(*chains[0])
    outs = []
    head_row = lax.broadcasted_iota(jnp.int32, (SUBLANES, sub), 0)
    lse_rows = jnp.zeros((SUBLANES, sub), F32)
    for i, (a, c, hh) in enumerate(chains):
        s = s_next
        if i + 1 < len(chains):
            s_next = scores(*chains[i + 1])
        m = jnp.max(s, axis=0, keepdims=True)
        e = jnp.exp2(s - m)
        l = jnp.sum(e, axis=0, keepdims=True)
        v = v_ref[0, 0, pl.ds(windows[a][0], span), c * LANES:(c + 1) * LANES]
        outs.append(_dot_tn(v, e.astype(BF16)) * (1.0 / l))
        lse_rows = jnp.where(head_row == 2 * c + hh, m * LN_2 + jnp.log(l), lse_rows)
        rows = slice(a * sub, (a + 1) * sub)
        if hh == 1:
            o_ref[0, 0, rows, c * LANES:(c + 1) * LANES] = (
                jnp.where(even_rows, outs[-2], outs[-1]).T.astype(o_ref.dtype))
        if (c, hh) == (DIL_HEADS // 2 - 1, 1):
            padded = jnp.concatenate([lse_rows, jnp.zeros((LANES - SUBLANES, sub), F32)], axis=0)
            lse_ref[0, 0, rows, :] = padded.T


def _dilated_pattern(q, k, v, window, dil, *, tq=512, sub=256):
    B, _, n, wq = q.shape
    wk = k.shape[-1]
    wsub = window // dil
    wpad = -(-wsub // LANES) * LANES
    tq = min(tq, n)
    sub = min(sub, tq)
    return pl.pallas_call(
        functools.partial(_dil_kernel, tq=tq, sub=sub, wpad=wpad, window=wsub),
        grid=(B, dil, n // tq),
        in_specs=[
            pl.BlockSpec((1, 1, tq, wq), lambda b, j, i: (b, j, i, 0)),
            pl.BlockSpec((1, 1, n, wk), lambda b, j, i: (b, j, 0, 0)),
            pl.BlockSpec((1, 1, n, wk), lambda b, j, i: (b, j, 0, 0)),
        ],
        out_specs=[pl.BlockSpec((1, 1, tq, wk), lambda b, j, i: (b, j, i, 0)),
                   pl.BlockSpec((1, 1, tq, LANES), lambda b, j, i: (b, j, i, 0))],
        out_shape=[jax.ShapeDtypeStruct((B, dil, n, wk), BF16), jax.ShapeDtypeStruct((B, dil, n, LANES), F32)],
        scratch_shapes=[pltpu.VMEM((tq // sub, DIL_HEADS, LANES, sub), BF16)],
        compiler_params=_cparams("parallel", "parallel", "arbitrary"),
        name=f"dilated_r{dil}",
    )(q, k, v)


def _compress_kernel(xk_ref, xv_ref, pos_ref, w1_ref, w2_ref, kc_ref, vct_ref):
    st = NSA_CMP_STRIDE
    n = xk_ref.shape[1] // st
    for t, x_ref in enumerate((xk_ref, xv_ref)):
        out = jnp.zeros((n, LANES), F32)
        for g in range(NSA_KV_GROUPS):
            first = jnp.zeros((n, NSA_CMP_HIDDEN), F32)
            second = jnp.zeros((n, NSA_CMP_HIDDEN), F32)
            for l in range(st):
                x = x_ref[0, pl.ds(l, n, stride=st), :]
                first = first + _dot((x + pos_ref[t, l]).astype(BF16), w1_ref[t, g, l])
                second = second + _dot((x + pos_ref[t, st + l]).astype(BF16), w1_ref[t, g, st + l])
            hid = first + pltpu.roll(second, n - 1, 0)
            hid = (hid * jax.nn.sigmoid(hid)).astype(BF16)
            out = out + _dot(hid, w2_ref[t, g])
        if t == 0:
            kc_ref[0] = out.astype(kc_ref.dtype)
        else:
            vct_ref[0] = out.T.astype(vct_ref.dtype)


def _nsa_compress(kc, vc, cmp_pos, cmp_w1, cmp_w2):
    B, S, _ = kc.shape
    G, d = NSA_KV_GROUPS, HEAD_DIM
    n = S // NSA_CMP_STRIDE
    w1 = cmp_w1.reshape(2, NSA_CMP_BLOCK, d, NSA_CMP_HIDDEN)
    zero1 = jnp.zeros_like(w1)
    w1g = jnp.stack([jnp.concatenate([w1, zero1], axis=2), jnp.concatenate([zero1, w1], axis=2)], axis=1)
    zero2 = jnp.zeros_like(cmp_w2)
    w2g = jnp.stack([jnp.concatenate([cmp_w2, zero2], axis=2), jnp.concatenate([zero2, cmp_w2], axis=2)], axis=1)
    pos = jnp.concatenate([cmp_pos] * G, axis=-1).reshape(2, NSA_CMP_BLOCK, 1, G * d)
    w1g, w2g = w1g.astype(BF16), w2g.astype(BF16)
    xspec = pl.BlockSpec((1, S, G * d), lambda b: (b, 0, 0))
    full = lambda a: pl.BlockSpec(a.shape, lambda b: (0,) * a.ndim)
    return pl.pallas_call(
        _compress_kernel,
        grid=(B,),
        in_specs=[xspec, xspec, full(pos), full(w1g), full(w2g)],
        out_specs=[pl.BlockSpec((1, n, G * d), lambda b: (b, 0, 0)),
                   pl.BlockSpec((1, G * d, n), lambda b: (b, 0, 0))],
        out_shape=[jax.ShapeDtypeStruct((B, n, G * d), BF16), jax.ShapeDtypeStruct((B, G * d, n), BF16)],
        compiler_params=_cparams("parallel"),
        name="nsa_compress",
    )(kc, vc, pos, w1g, w2g)


def _dot_01_by_f32(o01, p):
    p1 = p.astype(BF16)
    r1 = p - p1.astype(F32)
    p2 = r1.astype(BF16)
    p3 = (r1 - p2.astype(F32)).astype(BF16)
    return _dot(o01, p1) + _dot(o01, p2) + _dot(o01, p3)


SUBLANES = 8
RANK_TARGET_GROUPS = 4


def _count_outranking(sc_ref, cnt_ref, n_sources, tq):
    groups = SEL_LANES // SUBLANES
    sub = lax.broadcasted_iota(jnp.int32, (SUBLANES, tq), 0)
    cnt_ref[...] = jnp.zeros_like(cnt_ref)
    for bi in range(groups):
        @pl.when(bi * SUBLANES < n_sources)
        def _():
            src = sc_ref[bi * SUBLANES:(bi + 1) * SUBLANES, :]
            rows = [jnp.broadcast_to(src[ii:ii + 1, :], (SUBLANES, tq)) for ii in range(SUBLANES)]

            def count_targets(first, last):
                for r in range(first, last):
                    rsl = slice(r * SUBLANES, (r + 1) * SUBLANES)
                    tgt = sc_ref[rsl, :]
                    cnt = cnt_ref[rsl, :]
                    for ii, row in enumerate(rows):
                        if r > bi:
                            ahead = row >= tgt
                        elif r < bi:
                            ahead = row > tgt
                        else:
                            ahead = (row > tgt) | ((row == tgt) & (sub > ii))
                        cnt = cnt + jnp.where(ahead, 1.0, 0.0)
                    cnt_ref[rsl, :] = cnt

            for first in range(0, groups, RANK_TARGET_GROUPS):
                @pl.when(first * SUBLANES < n_sources)
                def _(first=first):
                    count_targets(first, min(first + RANK_TARGET_GROUPS, groups))


def _nsa_head_of_chunk(c, g):
    return c + g * NSA_HEADS_PER_GROUP


def _cmp_kernel(q_ref, kc_ref, vct_ref, ovt_ref, o_ref, bias_ref, qt_ref, sc_ref, cnt_ref, *, tq):
    q0 = pl.program_id(1) * tq
    n = kc_ref.shape[1]
    top = lax.broadcasted_iota(jnp.int32, (LANES, tq), 0) < HEAD_DIM
    for c in range(_NSA_Q_CHUNKS):
        qt_ref[_nsa_head_of_chunk(c, 0)], qt_ref[_nsa_head_of_chunk(c, 1)] = _split_pair_t(
            q_ref[0, :, c * LANES:(c + 1) * LANES])
    blk = lax.broadcasted_iota(jnp.int32, (SEL_LANES, tq), 0)
    cur = lax.shift_right_logical(q0 + lax.broadcasted_iota(jnp.int32, (SEL_LANES, tq), 1), 6)
    forced = (blk == 0) | (blk == cur) | (blk == cur - 1)

    def attend(nk):
        kc = kc_ref[0, :nk, :]
        vct = vct_ref[0, :, :nk]
        qpos = q0 + lax.broadcasted_iota(jnp.int32, (nk, tq), 1)
        cmp_end = lax.broadcasted_iota(jnp.int32, (nk, tq), 0) * NSA_CMP_STRIDE + (NSA_CMP_BLOCK - 1)
        visible = cmp_end <= qpos

        def scores(h):
            return jnp.where(visible, _dot(kc, qt_ref[h]), NEG_INF)

        psum = [jnp.zeros((nk, tq), F32) for _ in range(NSA_KV_GROUPS)]
        outs = [None] * NSA_HEADS
        s_next = scores(0)
        for h in range(NSA_HEADS):
            s = s_next
            if h + 1 < NSA_HEADS:
                s_next = scores(h + 1)
            m = jnp.max(s, axis=0, keepdims=True)
            e = jnp.where(visible, jnp.exp(s - m), 0.0)
            l = jnp.maximum(jnp.sum(e, axis=0, keepdims=True), 1e-30)
            p = e * (1.0 / l)
            outs[h] = _dot(vct, p.astype(BF16))
            psum[h // NSA_HEADS_PER_GROUP] = psum[h // NSA_HEADS_PER_GROUP] + p
        for c in range(_NSA_Q_CHUNKS):
            pair_t = jnp.where(top, outs[_nsa_head_of_chunk(c, 0)], outs[_nsa_head_of_chunk(c, 1)])
            o_ref[0, :, c * LANES:(c + 1) * LANES] = pair_t.T.astype(o_ref.dtype)
        for g in range(NSA_KV_GROUPS):
            imp_t = _dot_01_by_f32(ovt_ref[:, :nk], psum[g])
            sc_ref[g] = jnp.where(forced, FORCE_SCORE, jnp.where(blk <= cur, imp_t, -FORCE_SCORE))

    step = LANES if n % LANES == 0 else n
    needed = lax.div(q0 + tq + NSA_CMP_STRIDE * step - 1, NSA_CMP_STRIDE * step)
    for v in range(1, n // step + 1):
        @pl.when(jnp.minimum(needed, n // step) == v)
        def _(nk=v * step):
            attend(nk)

    n_causal = jnp.minimum(lax.shift_right_logical(q0 + tq - 1, 6) + 1, SEL_LANES)
    for g in range(NSA_KV_GROUPS):
        _count_outranking(sc_ref.at[g], cnt_ref.at[g], n_causal, tq)
        bias_t = jnp.where(cnt_ref[g] < float(NSA_TOP_N), 0.0, UNSELECTED_BIAS)
        bias_ref[0, g] = bias_t.astype(bias_ref.dtype)


def _nsa_compressed_branch(nq, kcc, vcct, overlap, *, tq=256):
    B, S, wq = nq.shape
    G = NSA_KV_GROUPS
    n = kcc.shape[1]
    return pl.pallas_call(
        functools.partial(_cmp_kernel, tq=tq),
        grid=(B, S // tq),
        in_specs=[
            pl.BlockSpec((1, tq, wq), lambda b, i: (b, i, 0)),
            pl.BlockSpec((1, n, LANES), lambda b, i: (b, 0, 0)),
            pl.BlockSpec((1, LANES, n), lambda b, i: (b, 0, 0)),
            pl.BlockSpec((SEL_LANES, n), lambda b, i: (0, 0)),
        ],
        out_specs=[
            pl.BlockSpec((1, tq, wq), lambda b, i: (b, i, 0)),
            pl.BlockSpec((1, G, SEL_LANES, tq), lambda b, i: (b, 0, 0, i)),
        ],
        out_shape=[jax.ShapeDtypeStruct((B, S, wq), BF16),
                   jax.ShapeDtypeStruct((B, G, SEL_LANES, S), BF16)],
        scratch_shapes=[pltpu.VMEM((NSA_HEADS, LANES, tq), BF16),
                        pltpu.VMEM((G, SEL_LANES, tq), F32), pltpu.VMEM((G, SEL_LANES, tq), F32)],
        compiler_params=_cparams("parallel", "arbitrary"),
        name="nsa_cmp_topk",
    )(nq, kcc, vcct, overlap.T)


def _sel_kernel(q_ref, bias_ref, k_ref, vt_ref, oh_ref, o_ref, qa_ref, m_ref, acc_ref, *, tq, tk):
    q0 = pl.program_id(1) * tq
    for c in range(_NSA_Q_CHUNKS):
        h0, h1 = _nsa_head_of_chunk(c, 0), _nsa_head_of_chunk(c, 1)
        qa_ref[h0, :LANES, :], qa_ref[h1, :LANES, :] = _split_pair_t(q_ref[0, :, c * LANES:(c + 1) * LANES])
    for h in range(NSA_HEADS):
        qa_ref[h, LANES:, :] = bias_ref[0, h // NSA_HEADS_PER_GROUP]
    _init_softmax_state(m_ref, acc_ref)

    def tile(kt, ntiles, masked, q_lo):
        start = pl.multiple_of(kt * tk, tk)
        keys = ntiles * tk
        cols = slice(q_lo, tq)
        if masked:
            kpos = start + lax.broadcasted_iota(jnp.int32, (keys, tq - q_lo), 0)
            qpos = q0 + q_lo + lax.broadcasted_iota(jnp.int32, (keys, tq - q_lo), 1)
        ka = jnp.concatenate([k_ref[0, pl.ds(start, keys), :], oh_ref[pl.ds(start, keys), :]], axis=1)
        vt1s = [_with_ones_rows(vt_ref[0, kt + j]) for j in range(ntiles)]

        def scores(h):
            s = _dot(ka, qa_ref[h, :, cols])
            return jnp.where(kpos <= qpos, s, NEG_INF) if masked else s

        s_next = scores(0)
        for h in range(NSA_HEADS):
            s = s_next
            if h + 1 < NSA_HEADS:
                s_next = scores(h + 1)
            _flash_step_t(s, vt1s, m_ref, acc_ref, h, cols)

    _causal_key_loop(q0, tq, tk, tile, pairs=True)
    top = lax.broadcasted_iota(jnp.int32, (LANES, tq), 0) < HEAD_DIM
    for c in range(_NSA_Q_CHUNKS):
        pair_t = jnp.where(top, _normalised(acc_ref[_nsa_head_of_chunk(c, 0)], LANES),
                           _normalised(acc_ref[_nsa_head_of_chunk(c, 1)], LANES))
        o_ref[0, :, c * LANES:(c + 1) * LANES] = pair_t.T.astype(o_ref.dtype)


def _nsa_selected_branch(nqr, bias, ksl, vslt, onehot, *, tq=1024, tk=KV_TILE):
    B, S, wq = nqr.shape
    G = NSA_KV_GROUPS
    return pl.pallas_call(
        functools.partial(_sel_kernel, tq=tq, tk=tk),
        grid=(B, S // tq),
        in_specs=[
            pl.BlockSpec((1, tq, wq), lambda b, i: (b, i, 0)),
            pl.BlockSpec((1, G, SEL_LANES, tq), lambda b, i: (b, 0, 0, i)),
            pl.BlockSpec((1, S, LANES), lambda b, i: (b, 0, 0)),
            pl.BlockSpec((1, S // tk, LANES, tk), lambda b, i: (b, 0, 0, 0)),
            pl.BlockSpec((S, SEL_LANES), lambda b, i: (0, 0)),
        ],
        out_specs=pl.BlockSpec((1, tq, wq), lambda b, i: (b, i, 0)),
        out_shape=jax.ShapeDtypeStruct((B, S, wq), BF16),
        scratch_shapes=[pltpu.VMEM((NSA_HEADS, LANES + SEL_LANES, tq), BF16)]
                       + _flash_state(NSA_HEADS, LANES, tq),
        compiler_params=_cparams("parallel", "arbitrary"),
        name="nsa_selected",
    )(nqr, bias, ksl, vslt, onehot)


def _win_kernel(q_ref, k_ref, v_ref, o_ref, qt_ref, *, tq, wpad, window):
    q0 = pl.program_id(1) * tq
    span = tq + wpad
    kstart = pl.multiple_of(jnp.maximum(q0 - wpad, 0), LANES)
    mask = _band_mask_t(q0, kstart, tq, span, window)
    k = k_ref[0, pl.ds(kstart, span), :]
    v = v_ref[0, pl.ds(kstart, span), :]
    top = lax.broadcasted_iota(jnp.int32, (LANES, tq), 0) < HEAD_DIM
    for c in range(_NSA_Q_CHUNKS):
        qt_ref[_nsa_head_of_chunk(c, 0)], qt_ref[_nsa_head_of_chunk(c, 1)] = _split_pair_t(
            q_ref[0, :, c * LANES:(c + 1) * LANES])

    def scores(h):
        return jnp.where(mask, _dot(k, qt_ref[h]), NEG_INF)

    outs = [None] * NSA_HEADS
    s_next = scores(0)
    for h in range(NSA_HEADS):
        s = s_next
        if h + 1 < NSA_HEADS:
            s_next = scores(h + 1)
        m = jnp.max(s, axis=0, keepdims=True)
        e = jnp.exp2(s - m)
        l = jnp.sum(e, axis=0, keepdims=True)
        outs[h] = _dot_tn(v, e.astype(BF16)) * (1.0 / l)
    for c in range(_NSA_Q_CHUNKS):
        pair_t = jnp.where(top, outs[_nsa_head_of_chunk(c, 0)], outs[_nsa_head_of_chunk(c, 1)])
        o_ref[0, :, c * LANES:(c + 1) * LANES] = pair_t.T.astype(o_ref.dtype)


def _nsa_window_branch(nqr, kw, vw, *, tq=256):
    B, S, wq = nqr.shape
    wpad = -(-NSA_WINDOW // LANES) * LANES
    return pl.pallas_call(
        functools.partial(_win_kernel, tq=tq, wpad=wpad, window=NSA_WINDOW),
        grid=(B, S // tq),
        in_specs=[
            pl.BlockSpec((1, tq, wq), lambda b, i: (b, i, 0)),
            pl.BlockSpec((1, S, LANES), lambda b, i: (b, 0, 0)),
            pl.BlockSpec((1, S, LANES), lambda b, i: (b, 0, 0)),
        ],
        out_specs=pl.BlockSpec((1, tq, wq), lambda b, i: (b, i, 0)),
        out_shape=jax.ShapeDtypeStruct((B, S, wq), BF16),
        scratch_shapes=[pltpu.VMEM((NSA_HEADS, LANES, tq), BF16)],
        compiler_params=_cparams("parallel", "arbitrary"),
        name="nsa_window",
    )(nqr, kw, vw)


def _selection_constants(S):
    n_cmp_rows = S // NSA_CMP_STRIDE
    c = np.arange(n_cmp_rows)
    cmp_start = c * NSA_CMP_STRIDE
    cmp_end = cmp_start + NSA_CMP_BLOCK - 1
    sel_start = np.arange(SEL_LANES) * NSA_SEL_BLOCK
    overlap = ((cmp_start[:, None] < sel_start[None, :] + NSA_SEL_BLOCK) &
               (cmp_end[:, None] >= sel_start[None, :]))
    n_cmp = (S - NSA_CMP_BLOCK) // NSA_CMP_STRIDE + 1
    overlap &= (c < n_cmp)[:, None]
    onehot = (np.arange(S)[:, None] // NSA_SEL_BLOCK) == np.arange(SEL_LANES)[None, :]
    return jnp.asarray(overlap, BF16), jnp.asarray(onehot, BF16)


def _out_kernel(*refs):
    n_pat = len(DIL_PATTERNS)
    x_ref, oa_ref = refs[:2]
    dil_refs = refs[2:2 + 2 * n_pat]
    (oc_ref, os_ref, ow_ref, gl_ref, wa_ref, wb_ref, wc_ref, g_ref, b_ref, o_ref,
     nat_ref) = refs[2 + 2 * n_pat:]
    tm = x_ref.shape[0]
    mix = _dot(oa_ref[...], wa_ref[...])

    lse_chunk = _DIL_KV_CHUNKS
    for p, (_, r) in enumerate(DIL_PATTERNS):
        o_ref_p, lse_ref_p = dil_refs[2 * p], dil_refs[2 * p + 1]
        for j in range(r):
            rows = pl.ds(j, tm // r, stride=r)
            for c in range(_DIL_KV_CHUNKS):
                nat_ref[p, c, rows, :] = o_ref_p[0, j, :, c * LANES:(c + 1) * LANES].astype(F32)
            nat_ref[p, lse_chunk, rows, :] = lse_ref_p[0, j]
    low_half = _lane_half((tm, LANES)) == 0
    for c in range(_DIL_KV_CHUNKS):
        outs = [nat_ref[p, c] for p in range(n_pat)]
        lses = [jnp.where(low_half, nat_ref[p, lse_chunk, :, 2 * c:2 * c + 1],
                          nat_ref[p, lse_chunk, :, 2 * c + 1:2 * c + 2]) for p in range(n_pat)]
        mx = functools.reduce(jnp.maximum, lses)
        es = [jnp.exp(l - mx) for l in lses]
        inv = 1.0 / functools.reduce(jnp.add, es)
        ob = functools.reduce(jnp.add, [(e * inv) * o for e, o in zip(es, outs)])
        mix = mix + _dot(ob.astype(BF16), wb_ref[c * LANES:(c + 1) * LANES, :])

    gl = gl_ref[...]
    for c in range(_NSA_Q_CHUNKS):
        sl = slice(c * LANES, (c + 1) * LANES)
        h0, h1 = _nsa_head_of_chunk(c, 0), _nsa_head_of_chunk(c, 1)
        oc = jnp.zeros((tm, LANES), F32)
        for k, branch_ref in enumerate((oc_ref, os_ref, ow_ref)):
            gate = jnp.where(low_half, gl[:, 3 * h0 + k:3 * h0 + k + 1], gl[:, 3 * h1 + k:3 * h1 + k + 1])
            oc = oc + gate * branch_ref[:, sl].astype(F32)
        mix = mix + _dot(oc.astype(BF16), wc_ref[sl, :])

    y = ALPHA * x_ref[...] + mix
    o_ref[...] = _layer_norm(y, g_ref[...], b_ref[...])


def _out_weight_layout(w_out):
    na = MLA_HEADS * MLA_V
    nb = DIL_HEADS * HEAD_DIM
    src = -np.ones(NSA_HEADS * HEAD_DIM, np.int64)
    for c in range(_NSA_Q_CHUNKS):
        for g in range(NSA_KV_GROUPS):
            d0 = c * LANES + g * HEAD_DIM
            src[d0:d0 + HEAD_DIM] = na + nb + _nsa_head_of_chunk(c, g) * HEAD_DIM + np.arange(HEAD_DIM)
    return (w_out[:na].astype(BF16), w_out[na:na + nb].astype(BF16),
            _gather_columns(w_out, src, 0).astype(BF16))


def _output_projection(x, oa, dil, oc, osl, ow, gl, wa, wb, wc, gain, bias, S, *, tm=512):
    T = x.shape[0]
    n_pos = S // tm
    row = lambda a: pl.BlockSpec((tm, a.shape[1]), lambda i: (i, 0))
    full = lambda a: pl.BlockSpec(a.shape, lambda i: (0, 0))
    residue = lambda a: pl.BlockSpec((1, a.shape[1], tm // a.shape[1], a.shape[3]),
                                     lambda i: (i // n_pos, 0, i % n_pos, 0))
    g2, b2 = gain.reshape(1, -1), bias.reshape(1, -1)
    dil_flat = [a for pair in dil for a in pair]
    rows = [oc, osl, ow, gl]
    consts = [wa, wb, wc, g2, b2]
    return pl.pallas_call(
        _out_kernel,
        grid=(T // tm,),
        in_specs=[row(x), row(oa)] + [residue(a) for a in dil_flat] + [row(a) for a in rows]
                 + [full(a) for a in consts],
        out_specs=pl.BlockSpec((tm, D_MODEL), lambda i: (i, 0)),
        out_shape=jax.ShapeDtypeStruct((T, D_MODEL), F32),
        scratch_shapes=[pltpu.VMEM((len(dil), _DIL_KV_CHUNKS + 1, tm, LANES), F32)],
        compiler_params=_cparams("parallel"),
        name="out_proj_ln",
    )(x, oa, *dil_flat, *rows, *consts)


def kernel(x, ffn_w_in, ffn_w_out, ln_gain, ln_bias, w_in, w_out, mla_q_norm, mla_kv_norm,
           mla_w_uq, mla_w_ukv, nsa_cmp_pos, nsa_cmp_w1, nsa_cmp_w2):
    B, S, D = x.shape
    assert D == D_MODEL and S % 2048 == 0 and S // NSA_SEL_BLOCK <= SEL_LANES
    T = B * S
    tables = _rope_tables(S)
    overlap, onehot = _selection_constants(S)
    proj_src = _proj_source_columns()
    bs = lambda t: t.reshape(B, S, t.shape[-1])

    xf = x.reshape(T, D)
    for l in range(DEPTH):
        xf = _ffn_ln(xf, ffn_w_in[l, 0].astype(BF16), ffn_w_out[l, 0].astype(BF16),
                     ln_gain[l, 0], ln_bias[l, 0])

        w_big = _gather_columns(w_in[l], proj_src, 1).astype(BF16)
        wuq, wkn, wv = _mla_weight_layout(mla_w_uq[l], mla_w_ukv[l])
        outs = _input_projection(xf, w_big, mla_q_norm[l], mla_kv_norm[l], wuq, wkn, wv, tables, S)
        qa, ka, va = outs[:3]
        n_dil = 3 * len(DIL_PATTERNS)
        dqkv = outs[3:3 + n_dil]
        nq, nqr, kc, vc, ksl, vsl, kw, vw, gl = outs[3 + n_dil:]

        oa = _mla_attention(bs(qa), bs(ka), va).reshape(T, -1)
        dil = [_dilated_pattern(*dqkv[3 * p:3 * p + 3], window, r)
               for p, (window, r) in enumerate(DIL_PATTERNS)]
        kcc, vcc = _nsa_compress(bs(kc), bs(vc), nsa_cmp_pos[l], nsa_cmp_w1[l], nsa_cmp_w2[l])
        oc, sel_bias = _nsa_compressed_branch(bs(nq), kcc, vcc, overlap)
        osl = _nsa_selected_branch(bs(nqr), sel_bias, bs(ksl), vsl, onehot)
        ow = _nsa_window_branch(bs(nqr), bs(kw), bs(vw))

        wa, wb, wc = _out_weight_layout(w_out[l])
        xf = _output_projection(xf, oa, dil, oc.reshape(T, -1), osl.reshape(T, -1), ow.reshape(T, -1),
                                gl, wa, wb, wc, ln_gain[l, 1], ln_bias[l, 1], S)

        xf = _ffn_ln(xf, ffn_w_in[l, 1].astype(BF16), ffn_w_out[l, 1].astype(BF16),
                     ln_gain[l, 2], ln_bias[l, 2])
    return xf.reshape(B, S, D)
```

```python
import functools

import numpy as np
import jax
import jax.numpy as jnp
from jax import lax
from jax.experimental import pallas as pl
from jax.experimental.pallas import tpu as pltpu

F32 = jnp.float32
BF16 = jnp.bfloat16

D_MODEL = 1024
DEPTH = 2
HEAD_DIM = 64
MLA_HEADS = 4
MLA_Q_LORA = 256
MLA_KV_LORA = 128
MLA_NOPE = 64
MLA_ROPE = 32
MLA_V = 64
DIL_HEADS = 6
DIL_PATTERNS = ((128, 1), (512, 4), (2048, 16))
NSA_HEADS = 6
NSA_KV_GROUPS = 2
NSA_HEADS_PER_GROUP = NSA_HEADS // NSA_KV_GROUPS
NSA_CMP_BLOCK = 32
NSA_CMP_STRIDE = 16
NSA_CMP_HIDDEN = 256
NSA_SEL_BLOCK = 64
NSA_TOP_N = 16
NSA_WINDOW = 512
D_FF = ((8 * D_MODEL // 3 + 255) // 256) * 256
ROPE_THETA = 10000.0
LN_EPS = 1e-5
RMS_EPS = 1e-6
NEG_INF = -1e30
FORCE_SCORE = 1e4
ALPHA = (2 * DEPTH) ** 0.25
LOG2_E = 1.4426950408889634
LN_2 = 0.6931471805599453

LANES = 128
SEL_LANES = 128
UNSELECTED_BIAS = -1e9
VMEM_LIMIT = 56 * 1024 * 1024
KV_TILE = 512

_SRC_CQ, _SRC_CKV, _SRC_KPE = 0, 256, 384
_SRC_DQ, _SRC_DK, _SRC_DV = 416, 800, 1184
_SRC_NQ = 1568
_SRC_KC, _SRC_VC, _SRC_KSL, _SRC_VSL, _SRC_KW, _SRC_VW = 1952, 2080, 2208, 2336, 2464, 2592
_SRC_GL = 2720
_C_CQ, _C_CKV, _C_KPE, _C_DQ, _C_DK, _C_DV, _C_NQ = 0, 256, 384, 512, 896, 1280, 1664
_C_KC, _C_VC, _C_KSL, _C_VSL, _C_KW, _C_VW, _C_GL = 2048, 2176, 2304, 2432, 2560, 2688, 2816
_PROJ_WIDTH = 2944


def _cparams(*sem):
    return pltpu.CompilerParams(dimension_semantics=sem, vmem_limit_bytes=VMEM_LIMIT)


def _layer_norm(y, g, b):
    mu = jnp.mean(y, axis=-1, keepdims=True)
    d = y - mu
    var = jnp.mean(d * d, axis=-1, keepdims=True)
    return d * lax.rsqrt(var + LN_EPS) * g + b


def _dot(a, b):
    return jnp.dot(a, b, preferred_element_type=F32)


def _dot_nt(a, b):
    return lax.dot_general(a, b, (((1,), (1,)), ((), ())), preferred_element_type=F32)


def _lane_half(shape):
    return lax.shift_right_logical(lax.broadcasted_iota(jnp.int32, shape, len(shape) - 1), 6)


FFN_LOAD_CHUNK = 256


def _ffn_load_weights(w_in_hbm, w_out_hbm, layer, which, wg_ref, wu_ref, wo_ref, stage_in, stage_out, sems):
    C = FFN_LOAD_CHUNK
    n_chunks = D_FF // C

    def copies(j, slot):
        cols = pl.ds(j * C, C)
        return (
            pltpu.make_async_copy(w_in_hbm.at[layer, which, :, cols], stage_in.at[slot, 0], sems.at[slot, 0]),
            pltpu.make_async_copy(w_in_hbm.at[layer, which, :, pl.ds(D_FF + j * C, C)], stage_in.at[slot, 1],
                                  sems.at[slot, 1]),
            pltpu.make_async_copy(w_out_hbm.at[layer, which, cols, :], stage_out.at[slot], sems.at[slot, 2]),
        )

    for c in copies(0, 0):
        c.start()
    for j in range(n_chunks):
        slot = j % 2
        if j + 1 < n_chunks:
            for c in copies(j + 1, 1 - slot):
                c.start()
        for c in copies(j, slot):
            c.wait()
        wg_ref[:, j * C:(j + 1) * C] = stage_in[slot, 0].astype(BF16)
        wu_ref[:, j * C:(j + 1) * C] = stage_in[slot, 1].astype(BF16)
        wo_ref[j * C:(j + 1) * C, :] = stage_out[slot].astype(BF16)


def _ffn_kernel(x_ref, w_in_hbm, w_out_hbm, g_ref, b_ref, o_ref, wg_ref, wu_ref, wo_ref, stage_in, stage_out,
                sems, *, layer, which):
    @pl.when(pl.program_id(0) == 0)
    def _():
        _ffn_load_weights(w_in_hbm, w_out_hbm, layer, which, wg_ref, wu_ref, wo_ref, stage_in, stage_out, sems)

    x = x_ref[...]
    xb = x.astype(BF16)
    gate = _dot(xb, wg_ref[...])
    up = _dot(xb, wu_ref[...])
    h = (gate * jax.nn.sigmoid(gate) * up).astype(BF16)
    y = ALPHA * x + 0.5 * _dot(h, wo_ref[...])
    o_ref[...] = _layer_norm(y, g_ref[...], b_ref[...])


def _ffn_ln(x, ffn_w_in, ffn_w_out, layer, which, gain, bias, *, tm=512):
    T = x.shape[0]
    C = FFN_LOAD_CHUNK
    return pl.pallas_call(
        functools.partial(_ffn_kernel, layer=layer, which=which),
        grid=(T // tm,),
        in_specs=[
            pl.BlockSpec((tm, D_MODEL), lambda i: (i, 0)),
            pl.BlockSpec(memory_space=pl.ANY),
            pl.BlockSpec(memory_space=pl.ANY),
            pl.BlockSpec((1, D_MODEL), lambda i: (0, 0)),
            pl.BlockSpec((1, D_MODEL), lambda i: (0, 0)),
        ],
        out_specs=pl.BlockSpec((tm, D_MODEL), lambda i: (i, 0)),
        out_shape=jax.ShapeDtypeStruct((T, D_MODEL), F32),
        scratch_shapes=[pltpu.VMEM((D_MODEL, D_FF), BF16), pltpu.VMEM((D_MODEL, D_FF), BF16),
                        pltpu.VMEM((D_FF, D_MODEL), BF16),
                        pltpu.VMEM((2, 2, D_MODEL, C), F32), pltpu.VMEM((2, C, D_MODEL), F32),
                        pltpu.SemaphoreType.DMA((2, 3))],
        compiler_params=_cparams("arbitrary"),
        name="ffn_ln",
    )(x, ffn_w_in, ffn_w_out, gain.reshape(1, -1), bias.reshape(1, -1))


def _rope_chunk(x, c, sa, sb, half):
    return x * c + pltpu.roll(x, LANES - half, 1) * sa + pltpu.roll(x, half, 1) * sb


def _rms_norm(x, g):
    return x * lax.rsqrt(jnp.mean(x * x, axis=-1, keepdims=True) + RMS_EPS) * g


_DIL_KV_CHUNKS = DIL_HEADS * HEAD_DIM // LANES
_NSA_Q_CHUNKS = NSA_HEADS * HEAD_DIM // LANES


def _proj_kernel(*refs):
    (x_ref, w_ref, qn_ref, kvn_ref, wuq_ref, wkn_ref, wv_ref,
     c64_ref, sa64_ref, sb64_ref, cm_ref, sam_ref, sbm_ref, qa_ref, ka_ref, va_ref) = refs[:16]
    dil_refs = refs[16:16 + 3 * len(DIL_PATTERNS)]
    (nq_ref, nqr_ref, kc_ref, vc_ref, ksl_ref, vsl_ref, kw_ref, vw_ref, gl_ref,
     stage_ref) = refs[16 + 3 * len(DIL_PATTERNS):]
    tm = x_ref.shape[0]
    xb = x_ref[...].astype(BF16)
    c64, sa64, sb64 = c64_ref[...], sa64_ref[...], sb64_ref[...]
    cm, sam, sbm = cm_ref[...], sam_ref[...], sbm_ref[...]

    y = _dot(xb, w_ref[...])

    def proj(off, width):
        return y[:, off:off + width]

    def rope64(v):
        return _rope_chunk(v, c64, sa64, sb64, HEAD_DIM // 2)

    def rope_mla(v):
        return _rope_chunk(v, cm, sam, sbm, MLA_ROPE // 2)

    cq = _rms_norm(proj(_C_CQ, MLA_Q_LORA), qn_ref[...]).astype(BF16)
    q_raw = _dot(cq, wuq_ref[...])
    ckv = _rms_norm(proj(_C_CKV, MLA_KV_LORA), kvn_ref[...]).astype(BF16)
    k_nope = _dot(ckv, wkn_ref[...])
    va_ref[0, 0] = _dot(ckv, wv_ref[...]).T.astype(BF16)
    kpe = rope_mla(proj(_C_KPE, LANES))
    q_scale = (MLA_NOPE + MLA_ROPE) ** -0.5 * LOG2_E
    for h in range(MLA_HEADS):
        sl = slice(h * LANES, (h + 1) * LANES)
        qa_ref[:, sl] = (rope_mla(q_raw[:, sl]) * q_scale).astype(BF16)
        ka_ref[:, sl] = (k_nope[:, sl] + kpe).astype(BF16)

    scale = HEAD_DIM ** -0.5
    dq = proj(_C_DQ, DIL_HEADS * HEAD_DIM)
    dk = proj(_C_DK, DIL_HEADS * HEAD_DIM)
    dv = proj(_C_DV, DIL_HEADS * HEAD_DIM)
    for c in range(_DIL_KV_CHUNKS):
        stage_ref[c] = rope64(dq[:, c * LANES:(c + 1) * LANES]) * (scale * LOG2_E)
        stage_ref[_DIL_KV_CHUNKS + c] = rope64(dk[:, c * LANES:(c + 1) * LANES])
        stage_ref[2 * _DIL_KV_CHUNKS + c] = dv[:, c * LANES:(c + 1) * LANES]
    for p, (_, r) in enumerate(DIL_PATTERNS):
        for t in range(3):
            o_ref = dil_refs[3 * p + t]
            for j in range(r):
                for c in range(_DIL_KV_CHUNKS):
                    rows = stage_ref[t * _DIL_KV_CHUNKS + c, pl.ds(j, tm // r, stride=r), :]
                    o_ref[0, j, :, c * LANES:(c + 1) * LANES] = rows.astype(BF16)

    nq = proj(_C_NQ, NSA_HEADS * HEAD_DIM)
    nq_ref[...] = (nq * scale).astype(BF16)
    for c in range(NSA_HEADS_PER_GROUP):
        sl = slice(c * LANES, (c + 1) * LANES)
        nqr_ref[:, sl] = (rope64(nq[:, sl]) * (scale * LOG2_E)).astype(BF16)
    kc_ref[...] = proj(_C_KC, LANES)
    vc_ref[...] = proj(_C_VC, LANES)
    ksl_ref[...] = rope64(proj(_C_KSL, LANES)).astype(BF16)
    vsl_ref[0, 0] = proj(_C_VSL, LANES).T.astype(BF16)
    kw_ref[...] = rope64(proj(_C_KW, LANES)).astype(BF16)
    vw_ref[...] = proj(_C_VW, LANES).astype(BF16)
    gl_ref[...] = jax.nn.sigmoid(proj(_C_GL, LANES))


def _proj_source_columns():
    src = -np.ones(_PROJ_WIDTH, np.int64)

    def put(dst, s0, n):
        src[dst:dst + n] = s0 + np.arange(n)

    put(_C_CQ, _SRC_CQ, MLA_Q_LORA)
    put(_C_CKV, _SRC_CKV, MLA_KV_LORA)
    put(_C_KPE + MLA_NOPE, _SRC_KPE, MLA_ROPE)
    put(_C_DQ, _SRC_DQ, DIL_HEADS * HEAD_DIM)
    put(_C_DK, _SRC_DK, DIL_HEADS * HEAD_DIM)
    put(_C_DV, _SRC_DV, DIL_HEADS * HEAD_DIM)
    for c in range(NSA_HEADS_PER_GROUP):
        for half, h in enumerate((c, c + NSA_HEADS_PER_GROUP)):
            put(_C_NQ + c * LANES + half * HEAD_DIM, _SRC_NQ + h * HEAD_DIM, HEAD_DIM)
    for dst, s0 in ((_C_KC, _SRC_KC), (_C_VC, _SRC_VC), (_C_KSL, _SRC_KSL), (_C_VSL, _SRC_VSL),
                    (_C_KW, _SRC_KW), (_C_VW, _SRC_VW)):
        put(dst, s0, NSA_KV_GROUPS * HEAD_DIM)
    put(_C_GL, _SRC_GL, NSA_HEADS * 3)
    return src


def _gather_columns(w, src, axis):
    pieces, i, n = [], 0, len(src)
    while i < n:
        j = i + 1
        if src[i] < 0:
            while j < n and src[j] < 0:
                j += 1
            shape = list(w.shape)
            shape[axis] = j - i
            pieces.append(jnp.zeros(shape, w.dtype))
        else:
            while j < n and src[j] == src[j - 1] + 1:
                j += 1
            pieces.append(lax.slice_in_dim(w, int(src[i]), int(src[i]) + j - i, axis=axis))
        i = j
    return jnp.concatenate(pieces, axis=axis)


def _mla_weight_layout(w_uq, w_ukv):
    dq = MLA_NOPE + MLA_ROPE
    src_q = -np.ones(MLA_HEADS * LANES, np.int64)
    src_kn = -np.ones(MLA_HEADS * LANES, np.int64)
    src_v = np.zeros(MLA_HEADS * MLA_V, np.int64)
    for h in range(MLA_HEADS):
        src_q[h * LANES:h * LANES + dq] = h * dq + np.arange(dq)
        src_kn[h * LANES:h * LANES + MLA_NOPE] = h * (MLA_NOPE + MLA_V) + np.arange(MLA_NOPE)
        src_v[h * MLA_V:(h + 1) * MLA_V] = h * (MLA_NOPE + MLA_V) + MLA_NOPE + np.arange(MLA_V)
    return (_gather_columns(w_uq, src_q, 1).astype(BF16),
            _gather_columns(w_ukv, src_kn, 1).astype(BF16),
            _gather_columns(w_ukv, src_v, 1).astype(BF16))


def _rope_tables(S):
    def cos_sin(dim):
        inv_freq = ROPE_THETA ** (-jnp.arange(0, dim, 2, dtype=F32) / dim)
        ang = jnp.arange(S, dtype=F32)[:, None] * inv_freq[None, :]
        return jnp.cos(ang), jnp.sin(ang)

    cos, sin = cos_sin(HEAD_DIM)
    zero = jnp.zeros_like(sin)
    c64 = jnp.concatenate([cos, cos] * 2, axis=1)
    sa64 = jnp.concatenate([-sin, zero] * 2, axis=1)
    sb64 = jnp.concatenate([zero, sin] * 2, axis=1)
    cos_m, sin_m = cos_sin(MLA_ROPE)
    ones = jnp.ones((S, MLA_NOPE), F32)
    z64 = jnp.zeros((S, MLA_NOPE), F32)
    z16 = jnp.zeros_like(sin_m)
    z32 = jnp.zeros((S, LANES - MLA_NOPE - MLA_ROPE), F32)
    cm = jnp.concatenate([ones, cos_m, cos_m, z32], axis=1)
    sam = jnp.concatenate([z64, -sin_m, z16, z32], axis=1)
    sbm = jnp.concatenate([z64, z16, sin_m, z32], axis=1)
    return c64, sa64, sb64, cm, sam, sbm


def _input_projection(h, w_big, q_norm, kv_norm, wuq, wkn, wv, tables, S, *, tm=KV_TILE):
    T = h.shape[0]
    n_pos = S // tm
    row = lambda w: pl.BlockSpec((tm, w), lambda i: (i, 0))
    full = lambda a: pl.BlockSpec(a.shape, lambda i: (0, 0))
    tab = pl.BlockSpec((tm, LANES), lambda i: (i % n_pos, 0))
    B = T // S
    specs, shapes = [], []

    def rows_out(w, dt):
        specs.append(row(w))
        shapes.append(jax.ShapeDtypeStruct((T, w), dt))

    def tile_t_out(c):
        specs.append(pl.BlockSpec((1, 1, c, tm), lambda i: (i // n_pos, i % n_pos, 0, 0)))
        shapes.append(jax.ShapeDtypeStruct((B, n_pos, c, tm), BF16))

    def residue_out(r, w):
        specs.append(pl.BlockSpec((1, r, tm // r, w), lambda i: (i // n_pos, 0, i % n_pos, 0)))
        shapes.append(jax.ShapeDtypeStruct((B, r, S // r, w), BF16))

    rows_out(MLA_HEADS * LANES, BF16)
    rows_out(MLA_HEADS * LANES, BF16)
    tile_t_out(MLA_HEADS * MLA_V)
    for _, r in DIL_PATTERNS:
        residue_out(r, DIL_HEADS * HEAD_DIM)
        residue_out(r, DIL_HEADS * HEAD_DIM)
        residue_out(r, DIL_HEADS * HEAD_DIM)
    rows_out(NSA_HEADS * HEAD_DIM, BF16)
    rows_out(NSA_HEADS * HEAD_DIM, BF16)
    rows_out(LANES, F32)
    rows_out(LANES, F32)
    rows_out(LANES, BF16)
    tile_t_out(LANES)
    rows_out(LANES, BF16)
    rows_out(LANES, BF16)
    rows_out(LANES, F32)
    qn = q_norm.reshape(1, -1)
    kvn = kv_norm.reshape(1, -1)
    return pl.pallas_call(
        _proj_kernel,
        grid=(T // tm,),
        in_specs=[row(D_MODEL), full(w_big), full(qn), full(kvn), full(wuq), full(wkn), full(wv)]
                 + [tab] * 6,
        out_specs=specs,
        out_shape=shapes,
        scratch_shapes=[pltpu.VMEM((3 * _DIL_KV_CHUNKS, tm, LANES), F32)],
        compiler_params=_cparams("parallel"),
        name="in_proj",
    )(h, w_big, qn, kvn, wuq, wkn, wv, *tables)


ONES_ROWS = 16


def _flash_step_t(s, vt1s, m_ref, acc_ref, idx, cols=slice(None)):
    m_prev = m_ref[idx, :, cols]
    m_new = jnp.maximum(m_prev, jnp.max(s, axis=0, keepdims=True))
    alpha = jnp.exp2(m_prev - m_new)
    p = jnp.exp2(s - m_new).astype(BF16)
    pv = functools.reduce(jnp.add, [_dot(vt1, p[j * KV_TILE:(j + 1) * KV_TILE]) for j, vt1 in enumerate(vt1s)])
    acc_ref[idx, :, cols] = alpha * acc_ref[idx, :, cols] + pv
    m_ref[idx, :, cols] = m_new


def _causal_key_loop(q0, tq, tk, tile, *, pairs):
    n_full = q0 // tk
    if pairs:
        def pair(i, carry):
            tile(2 * i, 2, False, 0)
            return carry

        lax.fori_loop(0, lax.shift_right_logical(n_full, 1), pair, 0)

        @pl.when(lax.rem(n_full, 2) == 1)
        def _():
            tile(n_full - 1, 1, False, 0)
    else:
        def single(i, carry):
            tile(i, 1, False, 0)
            return carry

        lax.fori_loop(0, n_full, single, 0)

    for d in range(tq // tk):
        tile(n_full + d, 1, True, d * tk)


def _init_softmax_state(m_ref, acc_ref):
    m_ref[...] = jnp.full(m_ref.shape, NEG_INF, F32)
    acc_ref[...] = jnp.zeros_like(acc_ref)


def _flash_state(slots, channels, queries):
    return [pltpu.VMEM((slots, 1, queries), F32), pltpu.VMEM((slots, channels + ONES_ROWS, queries), F32)]


def _split_pair_t(pair):
    t = pair.T
    top = lax.broadcasted_iota(jnp.int32, t.shape, 0) < HEAD_DIM
    zero = jnp.zeros_like(t)
    return jnp.where(top, t, zero), jnp.where(top, zero, t)


def _with_ones_rows(vt):
    return jnp.concatenate([vt, jnp.ones((ONES_ROWS, vt.shape[1]), vt.dtype)], axis=0)


def _normalised(acc, channels):
    return acc[:channels] / acc[channels:channels + 1]


def _mla_kernel(q_ref, k_ref, vt_ref, o_ref, qt_ref, m_ref, acc_ref, *, tq, tk):
    q0 = pl.program_id(1) * tq
    _init_softmax_state(m_ref, acc_ref)
    for h in range(MLA_HEADS):
        qt_ref[h] = q_ref[0, :, h * LANES:(h + 1) * LANES].T

    def tile(kt, ntiles, masked, q_lo):
        start = pl.multiple_of(kt * tk, tk)
        keys = ntiles * tk
        cols = slice(q_lo, tq)
        if masked:
            kpos = start + lax.broadcasted_iota(jnp.int32, (keys, tq - q_lo), 0)
            qpos = q0 + q_lo + lax.broadcasted_iota(jnp.int32, (keys, tq - q_lo), 1)

        def scores(h):
            s = _dot(k_ref[0, pl.ds(start, keys), h * LANES:(h + 1) * LANES], qt_ref[h, :, cols])
            return jnp.where(kpos <= qpos, s, NEG_INF) if masked else s

        s_next = scores(0)
        for h in range(MLA_HEADS):
            s = s_next
            if h + 1 < MLA_HEADS:
                s_next = scores(h + 1)
            vt1s = [_with_ones_rows(vt_ref[0, kt + j, (h // 2) * LANES:(h // 2 + 1) * LANES, :])
                    for j in range(ntiles)]
            _flash_step_t(s, vt1s, m_ref, acc_ref, h, cols)

    _causal_key_loop(q0, tq, tk, tile, pairs=True)
    even_rows = lax.broadcasted_iota(jnp.int32, (LANES, tq), 0) < MLA_V
    for pair in range(MLA_HEADS // 2):
        h0, h1 = 2 * pair, 2 * pair + 1
        out_t = jnp.where(even_rows, _normalised(acc_ref[h0], LANES), _normalised(acc_ref[h1], LANES))
        o_ref[0, :, pair * LANES:(pair + 1) * LANES] = out_t.T.astype(o_ref.dtype)


def _mla_attention(q, k, vt, *, tq=1024, tk=KV_TILE):
    B, S, _ = q.shape
    return pl.pallas_call(
        functools.partial(_mla_kernel, tq=tq, tk=tk),
        grid=(B, S // tq),
        in_specs=[
            pl.BlockSpec((1, tq, MLA_HEADS * LANES), lambda b, i: (b, i, 0)),
            pl.BlockSpec((1, S, MLA_HEADS * LANES), lambda b, i: (b, 0, 0)),
            pl.BlockSpec((1, S // tk, MLA_HEADS * MLA_V, tk), lambda b, i: (b, 0, 0, 0)),
        ],
        out_specs=pl.BlockSpec((1, tq, MLA_HEADS * MLA_V), lambda b, i: (b, i, 0)),
        out_shape=jax.ShapeDtypeStruct((B, S, MLA_HEADS * MLA_V), BF16),
        scratch_shapes=[pltpu.VMEM((MLA_HEADS, LANES, tq), BF16)] + _flash_state(MLA_HEADS, LANES, tq),
        compiler_params=_cparams("parallel", "arbitrary"),
        name="mla_attn",
    )(q, k, vt)


def _dot_tn(a, b):
    return lax.dot_general(a, b, (((0,), (0,)), ((), ())), preferred_element_type=F32)


def _band_mask_t(q0, kstart, tq, span, window):
    kpos = kstart + lax.broadcasted_iota(jnp.int32, (span, tq), 0)
    qpos = q0 + lax.broadcasted_iota(jnp.int32, (span, tq), 1)
    dist = qpos - kpos
    return (dist >= 0) & (dist <= window)


def _dil_kernel(q_ref, k_ref, v_ref, o_ref, lse_ref, qt_ref, *, tq, sub, wpad, window):
    q0 = pl.program_id(2) * tq
    n = k_ref.shape[2]
    span = min(sub + wpad, n)
    even_rows = lax.broadcasted_iota(jnp.int32, (LANES, sub), 0) < HEAD_DIM
    windows = []
    for a in range(tq // sub):
        qs = q0 + a * sub
        kstart = pl.multiple_of(jnp.clip(qs - wpad, 0, n - span), LANES)
        windows.append((kstart, _band_mask_t(qs, kstart, sub, span, window)))
    chains = [(a, c, hh) for a in range(tq // sub) for c in range(DIL_HEADS // 2) for hh in range(2)]
    for a in range(tq // sub):
        for c in range(DIL_HEADS // 2):
            qt_ref[a, 2 * c], qt_ref[a, 2 * c + 1] = _split_pair_t(
                q_ref[0, 0, a * sub:(a + 1) * sub, c * LANES:(c + 1) * LANES])

    def scores(a, c, hh):
        kstart, mask = windows[a]
        k = k_ref[0, 0, pl.ds(kstart, span), c * LANES:(c + 1) * LANES]
        return jnp.where(mask, _dot(k, qt_ref[a, 2 * c + hh]), NEG_INF)

    s_next = scores(*chains[0])
    outs = []
    head_row = lax.broadcasted_iota(jnp.int32, (SUBLANES, sub), 0)
    lse_rows = jnp.zeros((SUBLANES, sub), F32)
    for i, (a, c, hh) in enumerate(chains):
        s = s_next
        if i + 1 < len(chains):
            s_next = scores(*chains[i + 1])
        m = jnp.max(s, axis=0, keepdims=True)
        e = jnp.exp2(s - m)
        l = jnp.sum(e, axis=0, keepdims=True)
        v = v_ref[0, 0, pl.ds(windows[a][0], span), c * LANES:(c + 1) * LANES]
        outs.append(_dot_tn(v, e.astype(BF16)) * (1.0 / l))
        lse_rows = jnp.where(head_row == 2 * c + hh, m * LN_2 + jnp.log(l), lse_rows)
        rows = slice(a * sub, (a + 1) * sub)
        if hh == 1:
            o_ref[0, 0, rows, c * LANES:(c + 1) * LANES] = (
                jnp.where(even_rows, outs[-2], outs[-1]).T.astype(o_ref.dtype))
        if (c, hh) == (DIL_HEADS // 2 - 1, 1):
            padded = jnp.concatenate([lse_rows, jnp.zeros((LANES - SUBLANES, sub), F32)], axis=0)
            lse_ref[0, 0, rows, :] = padded.T


def _dilated_pattern(q, k, v, window, dil, *, tq=512, sub=256):
    B, _, n, wq = q.shape
    wk = k.shape[-1]
    wsub = window // dil
    wpad = -(-wsub // LANES) * LANES
    tq = min(tq, n)
    sub = min(sub, tq)
    return pl.pallas_call(
        functools.partial(_dil_kernel, tq=tq, sub=sub, wpad=wpad, window=wsub),
        grid=(B, dil, n // tq),
        in_specs=[
            pl.BlockSpec((1, 1, tq, wq), lambda b, j, i: (b, j, i, 0)),
            pl.BlockSpec((1, 1, n, wk), lambda b, j, i: (b, j, 0, 0)),
            pl.BlockSpec((1, 1, n, wk), lambda b, j, i: (b, j, 0, 0)),
        ],
        out_specs=[pl.BlockSpec((1, 1, tq, wk), lambda b, j, i: (b, j, i, 0)),
                   pl.BlockSpec((1, 1, tq, LANES), lambda b, j, i: (b, j, i, 0))],
        out_shape=[jax.ShapeDtypeStruct((B, dil, n, wk), BF16), jax.ShapeDtypeStruct((B, dil, n, LANES), F32)],
        scratch_shapes=[pltpu.VMEM((tq // sub, DIL_HEADS, LANES, sub), BF16)],
        compiler_params=_cparams("parallel", "parallel", "arbitrary"),
        name=f"dilated_r{dil}",
    )(q, k, v)


def _compress_kernel(xk_ref, xv_ref, pos_ref, w1_ref, w2_ref, kc_ref, vct_ref):
    st = NSA_CMP_STRIDE
    n = xk_ref.shape[1] // st
    for t, x_ref in enumerate((xk_ref, xv_ref)):
        out = jnp.zeros((n, LANES), F32)
        for g in range(NSA_KV_GROUPS):
            first = jnp.zeros((n, NSA_CMP_HIDDEN), F32)
            second = jnp.zeros((n, NSA_CMP_HIDDEN), F32)
            for l in range(st):
                x = x_ref[0, pl.ds(l, n, stride=st), :]
                first = first + _dot((x + pos_ref[t, l]).astype(BF16), w1_ref[t, g, l])
                second = second + _dot((x + pos_ref[t, st + l]).astype(BF16), w1_ref[t, g, st + l])
            hid = first + pltpu.roll(second, n - 1, 0)
            hid = (hid * jax.nn.sigmoid(hid)).astype(BF16)
            out = out + _dot(hid, w2_ref[t, g])
        if t == 0:
            kc_ref[0] = out.astype(kc_ref.dtype)
        else:
            vct_ref[0] = out.T.astype(vct_ref.dtype)


def _nsa_compress(kc, vc, cmp_pos, cmp_w1, cmp_w2):
    B, S, _ = kc.shape
    G, d = NSA_KV_GROUPS, HEAD_DIM
    n = S // NSA_CMP_STRIDE
    w1 = cmp_w1.reshape(2, NSA_CMP_BLOCK, d, NSA_CMP_HIDDEN)
    zero1 = jnp.zeros_like(w1)
    w1g = jnp.stack([jnp.concatenate([w1, zero1], axis=2), jnp.concatenate([zero1, w1], axis=2)], axis=1)
    zero2 = jnp.zeros_like(cmp_w2)
    w2g = jnp.stack([jnp.concatenate([cmp_w2, zero2], axis=2), jnp.concatenate([zero2, cmp_w2], axis=2)], axis=1)
    pos = jnp.concatenate([cmp_pos] * G, axis=-1).reshape(2, NSA_CMP_BLOCK, 1, G * d)
    w1g, w2g = w1g.astype(BF16), w2g.astype(BF16)
    xspec = pl.BlockSpec((1, S, G * d), lambda b: (b, 0, 0))
    full = lambda a: pl.BlockSpec(a.shape, lambda b: (0,) * a.ndim)
    return pl.pallas_call(
        _compress_kernel,
        grid=(B,),
        in_specs=[xspec, xspec, full(pos), full(w1g), full(w2g)],
        out_specs=[pl.BlockSpec((1, n, G * d), lambda b: (b, 0, 0)),
                   pl.BlockSpec((1, G * d, n), lambda b: (b, 0, 0))],
        out_shape=[jax.ShapeDtypeStruct((B, n, G * d), BF16), jax.ShapeDtypeStruct((B, G * d, n), BF16)],
        compiler_params=_cparams("parallel"),
        name="nsa_compress",
    )(kc, vc, pos, w1g, w2g)


def _dot_01_by_f32(o01, p):
    p1 = p.astype(BF16)
    r1 = p - p1.astype(F32)
    p2 = r1.astype(BF16)
    p3 = (r1 - p2.astype(F32)).astype(BF16)
    return _dot(o01, p1) + _dot(o01, p2) + _dot(o01, p3)


SUBLANES = 8
RANK_TARGET_GROUPS = 4


def _count_outranking(sc_ref, cnt_ref, n_sources, tq):
    groups = SEL_LANES // SUBLANES
    sub = lax.broadcasted_iota(jnp.int32, (SUBLANES, tq), 0)
    cnt_ref[...] = jnp.zeros_like(cnt_ref)
    for bi in range(groups):
        @pl.when(bi * SUBLANES < n_sources)
        def _():
            src = sc_ref[bi * SUBLANES:(bi + 1) * SUBLANES, :]
            rows = [jnp.broadcast_to(src[ii:ii + 1, :], (SUBLANES, tq)) for ii in range(SUBLANES)]

            def count_targets(first, last):
                for r in range(first, last):
                    rsl = slice(r * SUBLANES, (r + 1) * SUBLANES)
                    tgt = sc_ref[rsl, :]
                    cnt = cnt_ref[rsl, :]
                    for ii, row in enumerate(rows):
                        if r > bi:
                            ahead = row >= tgt
                        elif r < bi:
                            ahead = row > tgt
                        else:
                            ahead = (row > tgt) | ((row == tgt) & (sub > ii))
                        cnt = cnt + jnp.where(ahead, 1.0, 0.0)
                    cnt_ref[rsl, :] = cnt

            for first in range(0, groups, RANK_TARGET_GROUPS):
                @pl.when(first * SUBLANES < n_sources)
                def _(first=first):
                    count_targets(first, min(first + RANK_TARGET_GROUPS, groups))


def _nsa_head_of_chunk(c, g):
    return c + g * NSA_HEADS_PER_GROUP


def _cmp_kernel(q_ref, kc_ref, vct_ref, ovt_ref, o_ref, bias_ref, qt_ref, sc_ref, cnt_ref, *, tq):
    q0 = pl.program_id(1) * tq
    n = kc_ref.shape[1]
    top = lax.broadcasted_iota(jnp.int32, (LANES, tq), 0) < HEAD_DIM
    for c in range(_NSA_Q_CHUNKS):
        qt_ref[_nsa_head_of_chunk(c, 0)], qt_ref[_nsa_head_of_chunk(c, 1)] = _split_pair_t(
            q_ref[0, :, c * LANES:(c + 1) * LANES])
    blk = lax.broadcasted_iota(jnp.int32, (SEL_LANES, tq), 0)
    cur = lax.shift_right_logical(q0 + lax.broadcasted_iota(jnp.int32, (SEL_LANES, tq), 1), 6)
    forced = (blk == 0) | (blk == cur) | (blk == cur - 1)

    def attend(nk):
        kc = kc_ref[0, :nk, :]
        vct = vct_ref[0, :, :nk]
        qpos = q0 + lax.broadcasted_iota(jnp.int32, (nk, tq), 1)
        cmp_end = lax.broadcasted_iota(jnp.int32, (nk, tq), 0) * NSA_CMP_STRIDE + (NSA_CMP_BLOCK - 1)
        visible = cmp_end <= qpos

        def scores(h):
            return jnp.where(visible, _dot(kc, qt_ref[h]), NEG_INF)

        psum = [jnp.zeros((nk, tq), F32) for _ in range(NSA_KV_GROUPS)]
        outs = [None] * NSA_HEADS
        s_next = scores(0)
        for h in range(NSA_HEADS):
            s = s_next
            if h + 1 < NSA_HEADS:
                s_next = scores(h + 1)
            m = jnp.max(s, axis=0, keepdims=True)
            e = jnp.where(visible, jnp.exp(s - m), 0.0)
            l = jnp.maximum(jnp.sum(e, axis=0, keepdims=True), 1e-30)
            p = e * (1.0 / l)
            outs[h] = _dot(vct, p.astype(BF16))
            psum[h // NSA_HEADS_PER_GROUP] = psum[h // NSA_HEADS_PER_GROUP] + p
        for c in range(_NSA_Q_CHUNKS):
            pair_t = jnp.where(top, outs[_nsa_head_of_chunk(c, 0)], outs[_nsa_head_of_chunk(c, 1)])
            o_ref[0, :, c * LANES:(c + 1) * LANES] = pair_t.T.astype(o_ref.dtype)
        for g in range(NSA_KV_GROUPS):
            imp_t = _dot_01_by_f32(ovt_ref[:, :nk], psum[g])
            sc_ref[g] = jnp.where(forced, FORCE_SCORE, jnp.where(blk <= cur, imp_t, -FORCE_SCORE))

    step = LANES if n % LANES == 0 else n
    needed = lax.div(q0 + tq + NSA_CMP_STRIDE * step - 1, NSA_CMP_STRIDE * step)
    for v in range(1, n // step + 1):
        @pl.when(jnp.minimum(needed, n // step) == v)
        def _(nk=v * step):
            attend(nk)

    n_causal = jnp.minimum(lax.shift_right_logical(q0 + tq - 1, 6) + 1, SEL_LANES)
    for g in range(NSA_KV_GROUPS):
        _count_outranking(sc_ref.at[g], cnt_ref.at[g], n_causal, tq)
        bias_t = jnp.where(cnt_ref[g] < float(NSA_TOP_N), 0.0, UNSELECTED_BIAS)
        bias_ref[0, g] = bias_t.astype(bias_ref.dtype)


def _nsa_compressed_branch(nq, kcc, vcct, overlap, *, tq=256):
    B, S, wq = nq.shape
    G = NSA_KV_GROUPS
    n = kcc.shape[1]
    return pl.pallas_call(
        functools.partial(_cmp_kernel, tq=tq),
        grid=(B, S // tq),
        in_specs=[
            pl.BlockSpec((1, tq, wq), lambda b, i: (b, i, 0)),
            pl.BlockSpec((1, n, LANES), lambda b, i: (b, 0, 0)),
            pl.BlockSpec((1, LANES, n), lambda b, i: (b, 0, 0)),
            pl.BlockSpec((SEL_LANES, n), lambda b, i: (0, 0)),
        ],
        out_specs=[
            pl.BlockSpec((1, tq, wq), lambda b, i: (b, i, 0)),
            pl.BlockSpec((1, G, SEL_LANES, tq), lambda b, i: (b, 0, 0, i)),
        ],
        out_shape=[jax.ShapeDtypeStruct((B, S, wq), BF16),
                   jax.ShapeDtypeStruct((B, G, SEL_LANES, S), BF16)],
        scratch_shapes=[pltpu.VMEM((NSA_HEADS, LANES, tq), BF16),
                        pltpu.VMEM((G, SEL_LANES, tq), F32), pltpu.VMEM((G, SEL_LANES, tq), F32)],
        compiler_params=_cparams("parallel", "arbitrary"),
        name="nsa_cmp_topk",
    )(nq, kcc, vcct, overlap.T)


def _sel_kernel(q_ref, bias_ref, k_ref, vt_ref, oh_ref, o_ref, qa_ref, m_ref, acc_ref, *, tq, tk):
    q0 = pl.program_id(1) * tq
    for c in range(_NSA_Q_CHUNKS):
        h0, h1 = _nsa_head_of_chunk(c, 0), _nsa_head_of_chunk(c, 1)
        qa_ref[h0, :LANES, :], qa_ref[h1, :LANES, :] = _split_pair_t(q_ref[0, :, c * LANES:(c + 1) * LANES])
    for h in range(NSA_HEADS):
        qa_ref[h, LANES:, :] = bias_ref[0, h // NSA_HEADS_PER_GROUP]
    _init_softmax_state(m_ref, acc_ref)

    def tile(kt, ntiles, masked, q_lo):
        start = pl.multiple_of(kt * tk, tk)
        keys = ntiles * tk
        cols = slice(q_lo, tq)
        if masked:
            kpos = start + lax.broadcasted_iota(jnp.int32, (keys, tq - q_lo), 0)
            qpos = q0 + q_lo + lax.broadcasted_iota(jnp.int32, (keys, tq - q_lo), 1)
        ka = jnp.concatenate([k_ref[0, pl.ds(start, keys), :], oh_ref[pl.ds(start, keys), :]], axis=1)
        vt1s = [_with_ones_rows(vt_ref[0, kt + j]) for j in range(ntiles)]

        def scores(h):
            s = _dot(ka, qa_ref[h, :, cols])
            return jnp.where(kpos <= qpos, s, NEG_INF) if masked else s

        s_next = scores(0)
        for h in range(NSA_HEADS):
            s = s_next
            if h + 1 < NSA_HEADS:
                s_next = scores(h + 1)
            _flash_step_t(s, vt1s, m_ref, acc_ref, h, cols)

    _causal_key_loop(q0, tq, tk, tile, pairs=True)
    top = lax.broadcasted_iota(jnp.int32, (LANES, tq), 0) < HEAD_DIM
    for c in range(_NSA_Q_CHUNKS):
        pair_t = jnp.where(top, _normalised(acc_ref[_nsa_head_of_chunk(c, 0)], LANES),
                           _normalised(acc_ref[_nsa_head_of_chunk(c, 1)], LANES))
        o_ref[0, :, c * LANES:(c + 1) * LANES] = pair_t.T.astype(o_ref.dtype)


def _nsa_selected_branch(nqr, bias, ksl, vslt, onehot, *, tq=1024, tk=KV_TILE):
    B, S, wq = nqr.shape
    G = NSA_KV_GROUPS
    return pl.pallas_call(
        functools.partial(_sel_kernel, tq=tq, tk=tk),
        grid=(B, S // tq),
        in_specs=[
            pl.BlockSpec((1, tq, wq), lambda b, i: (b, i, 0)),
            pl.BlockSpec((1, G, SEL_LANES, tq), lambda b, i: (b, 0, 0, i)),
            pl.BlockSpec((1, S, LANES), lambda b, i: (b, 0, 0)),
            pl.BlockSpec((1, S // tk, LANES, tk), lambda b, i: (b, 0, 0, 0)),
            pl.BlockSpec((S, SEL_LANES), lambda b, i: (0, 0)),
        ],
        out_specs=pl.BlockSpec((1, tq, wq), lambda b, i: (b, i, 0)),
        out_shape=jax.ShapeDtypeStruct((B, S, wq), BF16),
        scratch_shapes=[pltpu.VMEM((NSA_HEADS, LANES + SEL_LANES, tq), BF16)]
                       + _flash_state(NSA_HEADS, LANES, tq),
        compiler_params=_cparams("parallel", "arbitrary"),
        name="nsa_selected",
    )(nqr, bias, ksl, vslt, onehot)


def _win_kernel(q_ref, k_ref, v_ref, o_ref, qt_ref, *, tq, wpad, window):
    q0 = pl.program_id(1) * tq
    span = tq + wpad
    kstart = pl.multiple_of(jnp.maximum(q0 - wpad, 0), LANES)
    mask = _band_mask_t(q0, kstart, tq, span, window)
    k = k_ref[0, pl.ds(kstart, span), :]
    v = v_ref[0, pl.ds(kstart, span), :]
    top = lax.broadcasted_iota(jnp.int32, (LANES, tq), 0) < HEAD_DIM
    for c in range(_NSA_Q_CHUNKS):
        qt_ref[_nsa_head_of_chunk(c, 0)], qt_ref[_nsa_head_of_chunk(c, 1)] = _split_pair_t(
            q_ref[0, :, c * LANES:(c + 1) * LANES])

    def scores(h):
        return jnp.where(mask, _dot(k, qt_ref[h]), NEG_INF)

    outs = [None] * NSA_HEADS
    s_next = scores(0)
    for h in range(NSA_HEADS):
        s = s_next
        if h + 1 < NSA_HEADS:
            s_next = scores(h + 1)
        m = jnp.max(s, axis=0, keepdims=True)
        e = jnp.exp2(s - m)
        l = jnp.sum(e, axis=0, keepdims=True)
        outs[h] = _dot_tn(v, e.astype(BF16)) * (1.0 / l)
    for c in range(_NSA_Q_CHUNKS):
        pair_t = jnp.where(top, outs[_nsa_head_of_chunk(c, 0)], outs[_nsa_head_of_chunk(c, 1)])
        o_ref[0, :, c * LANES:(c + 1) * LANES] = pair_t.T.astype(o_ref.dtype)


def _nsa_window_branch(nqr, kw, vw, *, tq=256):
    B, S, wq = nqr.shape
    wpad = -(-NSA_WINDOW // LANES) * LANES
    return pl.pallas_call(
        functools.partial(_win_kernel, tq=tq, wpad=wpad, window=NSA_WINDOW),
        grid=(B, S // tq),
        in_specs=[
            pl.BlockSpec((1, tq, wq), lambda b, i: (b, i, 0)),
            pl.BlockSpec((1, S, LANES), lambda b, i: (b, 0, 0)),
            pl.BlockSpec((1, S, LANES), lambda b, i: (b, 0, 0)),
        ],
        out_specs=pl.BlockSpec((1, tq, wq), lambda b, i: (b, i, 0)),
        out_shape=jax.ShapeDtypeStruct((B, S, wq), BF16),
        scratch_shapes=[pltpu.VMEM((NSA_HEADS, LANES, tq), BF16)],
        compiler_params=_cparams("parallel", "arbitrary"),
        name="nsa_window",
    )(nqr, kw, vw)


def _selection_constants(S):
    n_cmp_rows = S // NSA_CMP_STRIDE
    c = np.arange(n_cmp_rows)
    cmp_start = c * NSA_CMP_STRIDE
    cmp_end = cmp_start + NSA_CMP_BLOCK - 1
    sel_start = np.arange(SEL_LANES) * NSA_SEL_BLOCK
    overlap = ((cmp_start[:, None] < sel_start[None, :] + NSA_SEL_BLOCK) &
               (cmp_end[:, None] >= sel_start[None, :]))
    n_cmp = (S - NSA_CMP_BLOCK) // NSA_CMP_STRIDE + 1
    overlap &= (c < n_cmp)[:, None]
    onehot = (np.arange(S)[:, None] // NSA_SEL_BLOCK) == np.arange(SEL_LANES)[None, :]
    return jnp.asarray(overlap, BF16), jnp.asarray(onehot, BF16)


def _out_kernel(*refs):
    n_pat = len(DIL_PATTERNS)
    x_ref, oa_ref = refs[:2]
    dil_refs = refs[2:2 + 2 * n_pat]
    (oc_ref, os_ref, ow_ref, gl_ref, wa_ref, wb_ref, wc_ref, g_ref, b_ref, o_ref,
     nat_ref) = refs[2 + 2 * n_pat:]
    tm = x_ref.shape[0]
    mix = _dot(oa_ref[...], wa_ref[...])

    lse_chunk = _DIL_KV_CHUNKS
    for p, (_, r) in enumerate(DIL_PATTERNS):
        o_ref_p, lse_ref_p = dil_refs[2 * p], dil_refs[2 * p + 1]
        for j in range(r):
            rows = pl.ds(j, tm // r, stride=r)
            for c in range(_DIL_KV_CHUNKS):
                nat_ref[p, c, rows, :] = o_ref_p[0, j, :, c * LANES:(c + 1) * LANES].astype(F32)
            nat_ref[p, lse_chunk, rows, :] = lse_ref_p[0, j]
    low_half = _lane_half((tm, LANES)) == 0
    for c in range(_DIL_KV_CHUNKS):
        outs = [nat_ref[p, c] for p in range(n_pat)]
        lses = [jnp.where(low_half, nat_ref[p, lse_chunk, :, 2 * c:2 * c + 1],
                          nat_ref[p, lse_chunk, :, 2 * c + 1:2 * c + 2]) for p in range(n_pat)]
        mx = functools.reduce(jnp.maximum, lses)
        es = [jnp.exp(l - mx) for l in lses]
        inv = 1.0 / functools.reduce(jnp.add, es)
        ob = functools.reduce(jnp.add, [(e * inv) * o for e, o in zip(es, outs)])
        mix = mix + _dot(ob.astype(BF16), wb_ref[c * LANES:(c + 1) * LANES, :])

    gl = gl_ref[...]
    for c in range(_NSA_Q_CHUNKS):
        sl = slice(c * LANES, (c + 1) * LANES)
        h0, h1 = _nsa_head_of_chunk(c, 0), _nsa_head_of_chunk(c, 1)
        oc = jnp.zeros((tm, LANES), F32)
        for k, branch_ref in enumerate((oc_ref, os_ref, ow_ref)):
            gate = jnp.where(low_half, gl[:, 3 * h0 + k:3 * h0 + k + 1], gl[:, 3 * h1 + k:3 * h1 + k + 1])
            oc = oc + gate * branch_ref[:, sl].astype(F32)
        mix = mix + _dot(oc.astype(BF16), wc_ref[sl, :])

    y = ALPHA * x_ref[...] + mix
    o_ref[...] = _layer_norm(y, g_ref[...], b_ref[...])


def _out_weight_layout(w_out):
    na = MLA_HEADS * MLA_V
    nb = DIL_HEADS * HEAD_DIM
    src = -np.ones(NSA_HEADS * HEAD_DIM, np.int64)
    for c in range(_NSA_Q_CHUNKS):
        for g in range(NSA_KV_GROUPS):
            d0 = c * LANES + g * HEAD_DIM
            src[d0:d0 + HEAD_DIM] = na + nb + _nsa_head_of_chunk(c, g) * HEAD_DIM + np.arange(HEAD_DIM)
    return (w_out[:na].astype(BF16), w_out[na:na + nb].astype(BF16),
            _gather_columns(w_out, src, 0).astype(BF16))


def _output_projection(x, oa, dil, oc, osl, ow, gl, wa, wb, wc, gain, bias, S, *, tm=512):
    T = x.shape[0]
    n_pos = S // tm
    row = lambda a: pl.BlockSpec((tm, a.shape[1]), lambda i: (i, 0))
    full = lambda a: pl.BlockSpec(a.shape, lambda i: (0, 0))
    residue = lambda a: pl.BlockSpec((1, a.shape[1], tm // a.shape[1], a.shape[3]),
                                     lambda i: (i // n_pos, 0, i % n_pos, 0))
    g2, b2 = gain.reshape(1, -1), bias.reshape(1, -1)
    dil_flat = [a for pair in dil for a in pair]
    rows = [oc, osl, ow, gl]
    consts = [wa, wb, wc, g2, b2]
    return pl.pallas_call(
        _out_kernel,
        grid=(T // tm,),
        in_specs=[row(x), row(oa)] + [residue(a) for a in dil_flat] + [row(a) for a in rows]
                 + [full(a) for a in consts],
        out_specs=pl.BlockSpec((tm, D_MODEL), lambda i: (i, 0)),
        out_shape=jax.ShapeDtypeStruct((T, D_MODEL), F32),
        scratch_shapes=[pltpu.VMEM((len(dil), _DIL_KV_CHUNKS + 1, tm, LANES), F32)],
        compiler_params=_cparams("parallel"),
        name="out_proj_ln",
    )(x, oa, *dil_flat, *rows, *consts)


def kernel(x, ffn_w_in, ffn_w_out, ln_gain, ln_bias, w_in, w_out, mla_q_norm, mla_kv_norm,
           mla_w_uq, mla_w_ukv, nsa_cmp_pos, nsa_cmp_w1, nsa_cmp_w2):
    B, S, D = x.shape
    assert D == D_MODEL and S % 2048 == 0 and S // NSA_SEL_BLOCK <= SEL_LANES
    T = B * S
    tables = _rope_tables(S)
    overlap, onehot = _selection_constants(S)
    proj_src = _proj_source_columns()
    bs = lambda t: t.reshape(B, S, t.shape[-1])

    xf = x.reshape(T, D)
    for l in range(DEPTH):
        xf = _ffn_ln(xf, ffn_w_in, ffn_w_out, l, 0, ln_gain[l, 0], ln_bias[l, 0])

        w_big = _gather_columns(w_in[l], proj_src, 1).astype(BF16)
        wuq, wkn, wv = _mla_weight_layout(mla_w_uq[l], mla_w_ukv[l])
        outs = _input_projection(xf, w_big, mla_q_norm[l], mla_kv_norm[l], wuq, wkn, wv, tables, S)
        qa, ka, va = outs[:3]
        n_dil = 3 * len(DIL_PATTERNS)
        dqkv = outs[3:3 + n_dil]
        nq, nqr, kc, vc, ksl, vsl, kw, vw, gl = outs[3 + n_dil:]

        oa = _mla_attention(bs(qa), bs(ka), va).reshape(T, -1)
        dil = [_dilated_pattern(*dqkv[3 * p:3 * p + 3], window, r)
               for p, (window, r) in enumerate(DIL_PATTERNS)]
        kcc, vcc = _nsa_compress(bs(kc), bs(vc), nsa_cmp_pos[l], nsa_cmp_w1[l], nsa_cmp_w2[l])
        oc, sel_bias = _nsa_compressed_branch(bs(nq), kcc, vcc, overlap)
        osl = _nsa_selected_branch(bs(nqr), sel_bias, bs(ksl), vsl, onehot)
        ow = _nsa_window_branch(bs(nqr), bs(kw), bs(vw))

        wa, wb, wc = _out_weight_layout(w_out[l])
        xf = _output_projection(xf, oa, dil, oc.reshape(T, -1), osl.reshape(T, -1), ow.reshape(T, -1),
                                gl, wa, wb, wc, ln_gain[l, 1], ln_bias[l, 1], S)

        xf = _ffn_ln(xf, ffn_w_in, ffn_w_out, l, 1, ln_gain[l, 2], ln_bias[l, 2])
    return xf.reshape(B, S, D)
```

```python
import functools

import numpy as np
import jax
import jax.numpy as jnp
from jax import lax
from jax.experimental import pallas as pl
from jax.experimental.pallas import tpu as pltpu

F32 = jnp.float32
BF16 = jnp.bfloat16

D_MODEL = 1024
DEPTH = 2
HEAD_DIM = 64
MLA_HEADS = 4
MLA_Q_LORA = 256
MLA_KV_LORA = 128
MLA_NOPE = 64
MLA_ROPE = 32
MLA_V = 64
DIL_HEADS = 6
DIL_PATTERNS = ((128, 1), (512, 4), (2048, 16))
NSA_HEADS = 6
NSA_KV_GROUPS = 2
NSA_HEADS_PER_GROUP = NSA_HEADS // NSA_KV_GROUPS
NSA_CMP_BLOCK = 32
NSA_CMP_STRIDE = 16
NSA_CMP_HIDDEN = 256
NSA_SEL_BLOCK = 64
NSA_TOP_N = 16
NSA_WINDOW = 512
D_FF = ((8 * D_MODEL // 3 + 255) // 256) * 256
ROPE_THETA = 10000.0
LN_EPS = 1e-5
RMS_EPS = 1e-6
NEG_INF = -1e30
FORCE_SCORE = 1e4
ALPHA = (2 * DEPTH) ** 0.25
LOG2_E = 1.4426950408889634
LN_2 = 0.6931471805599453

LANES = 128
SEL_LANES = 128
UNSELECTED_BIAS = -1e9
VMEM_LIMIT = 56 * 1024 * 1024
KV_TILE = 512

_SRC_CQ, _SRC_CKV, _SRC_KPE = 0, 256, 384
_SRC_DQ, _SRC_DK, _SRC_DV = 416, 800, 1184
_SRC_NQ = 1568
_SRC_KC, _SRC_VC, _SRC_KSL, _SRC_VSL, _SRC_KW, _SRC_VW = 1952, 2080, 2208, 2336, 2464, 2592
_SRC_GL = 2720
_C_CQ, _C_CKV, _C_KPE, _C_DQ, _C_DK, _C_DV, _C_NQ = 0, 256, 384, 512, 896, 1280, 1664
_C_KC, _C_VC, _C_KSL, _C_VSL, _C_KW, _C_VW, _C_GL = 2048, 2176, 2304, 2432, 2560, 2688, 2816
_PROJ_WIDTH = 2944


def _cparams(*sem):
    return pltpu.CompilerParams(dimension_semantics=sem, vmem_limit_bytes=VMEM_LIMIT)


def _layer_norm(y, g, b):
    mu = jnp.mean(y, axis=-1, keepdims=True)
    d = y - mu
    var = jnp.mean(d * d, axis=-1, keepdims=True)
    return d * lax.rsqrt(var + LN_EPS) * g + b


def _dot(a, b):
    return jnp.dot(a, b, preferred_element_type=F32)


def _lane_half(shape):
    return lax.shift_right_logical(lax.broadcasted_iota(jnp.int32, shape, len(shape) - 1), 6)


FFN_LOAD_CHUNK = 256


def _ffn_load_weights(w_in_hbm, w_out_hbm, layer, which, wg_ref, wu_ref, wo_ref, stage_in, stage_out, sems):
    C = FFN_LOAD_CHUNK
    n_chunks = D_FF // C

    def copies(j, slot):
        cols = pl.ds(j * C, C)
        return (
            pltpu.make_async_copy(w_in_hbm.at[layer, which, :, cols], stage_in.at[slot, 0], sems.at[slot, 0]),
            pltpu.make_async_copy(w_in_hbm.at[layer, which, :, pl.ds(D_FF + j * C, C)], stage_in.at[slot, 1],
                                  sems.at[slot, 1]),
            pltpu.make_async_copy(w_out_hbm.at[layer, which, cols, :], stage_out.at[slot], sems.at[slot, 2]),
        )

    for c in copies(0, 0):
        c.start()
    for j in range(n_chunks):
        slot = j % 2
        if j + 1 < n_chunks:
            for c in copies(j + 1, 1 - slot):
                c.start()
        for c in copies(j, slot):
            c.wait()
        wg_ref[:, j * C:(j + 1) * C] = stage_in[slot, 0].astype(BF16)
        wu_ref[:, j * C:(j + 1) * C] = stage_in[slot, 1].astype(BF16)
        wo_ref[j * C:(j + 1) * C, :] = stage_out[slot].astype(BF16)


def _ffn_kernel(x_ref, w_in_hbm, w_out_hbm, g_ref, b_ref, o_ref, wg_ref, wu_ref, wo_ref, stage_in, stage_out,
                sems, *, layer, which):
    @pl.when(pl.program_id(0) == 0)
    def _():
        _ffn_load_weights(w_in_hbm, w_out_hbm, layer, which, wg_ref, wu_ref, wo_ref, stage_in, stage_out, sems)

    x = x_ref[...]
    xb = x.astype(BF16)
    gate = _dot(xb, wg_ref[...])
    up = _dot(xb, wu_ref[...])
    h = (gate * jax.nn.sigmoid(gate) * up).astype(BF16)
    y = ALPHA * x + 0.5 * _dot(h, wo_ref[...])
    o_ref[...] = _layer_norm(y, g_ref[...], b_ref[...])


def _ffn_ln(x, ffn_w_in, ffn_w_out, layer, which, gain, bias, *, tm=512):
    T = x.shape[0]
    C = FFN_LOAD_CHUNK
    return pl.pallas_call(
        functools.partial(_ffn_kernel, layer=layer, which=which),
        grid=(T // tm,),
        in_specs=[
            pl.BlockSpec((tm, D_MODEL), lambda i: (i, 0)),
            pl.BlockSpec(memory_space=pl.ANY),
            pl.BlockSpec(memory_space=pl.ANY),
            pl.BlockSpec((1, D_MODEL), lambda i: (0, 0)),
            pl.BlockSpec((1, D_MODEL), lambda i: (0, 0)),
        ],
        out_specs=pl.BlockSpec((tm, D_MODEL), lambda i: (i, 0)),
        out_shape=jax.ShapeDtypeStruct((T, D_MODEL), F32),
        scratch_shapes=[pltpu.VMEM((D_MODEL, D_FF), BF16), pltpu.VMEM((D_MODEL, D_FF), BF16),
                        pltpu.VMEM((D_FF, D_MODEL), BF16),
                        pltpu.VMEM((2, 2, D_MODEL, C), F32), pltpu.VMEM((2, C, D_MODEL), F32),
                        pltpu.SemaphoreType.DMA((2, 3))],
        compiler_params=_cparams("arbitrary"),
        name="ffn_ln",
    )(x, ffn_w_in, ffn_w_out, gain.reshape(1, -1), bias.reshape(1, -1))


def _rope_chunk(x, c, sa, sb, half):
    return x * c + pltpu.roll(x, LANES - half, 1) * sa + pltpu.roll(x, half, 1) * sb


def _rms_norm(x, g):
    return x * lax.rsqrt(jnp.mean(x * x, axis=-1, keepdims=True) + RMS_EPS) * g


_DIL_KV_CHUNKS = DIL_HEADS * HEAD_DIM // LANES
_NSA_Q_CHUNKS = NSA_HEADS * HEAD_DIM // LANES


def _proj_kernel(*refs):
    (x_ref, w_ref, qn_ref, kvn_ref, wuq_ref, wkn_ref, wv_ref,
     c64_ref, sa64_ref, sb64_ref, cm_ref, sam_ref, sbm_ref, qa_ref, ka_ref, va_ref) = refs[:16]
    dil_refs = refs[16:16 + 3 * len(DIL_PATTERNS)]
    (nq_ref, nqr_ref, kc_ref, vc_ref, ksl_ref, vsl_ref, kw_ref, vw_ref, gl_ref,
     stage_ref) = refs[16 + 3 * len(DIL_PATTERNS):]
    tm = x_ref.shape[0]
    xb = x_ref[...].astype(BF16)
    c64, sa64, sb64 = c64_ref[...], sa64_ref[...], sb64_ref[...]
    cm, sam, sbm = cm_ref[...], sam_ref[...], sbm_ref[...]

    y = _dot(xb, w_ref[...])

    def proj(off, width):
        return y[:, off:off + width]

    def rope64(v):
        return _rope_chunk(v, c64, sa64, sb64, HEAD_DIM // 2)

    def rope_mla(v):
        return _rope_chunk(v, cm, sam, sbm, MLA_ROPE // 2)

    cq = _rms_norm(proj(_C_CQ, MLA_Q_LORA), qn_ref[...]).astype(BF16)
    q_raw = _dot(cq, wuq_ref[...])
    ckv = _rms_norm(proj(_C_CKV, MLA_KV_LORA), kvn_ref[...]).astype(BF16)
    k_nope = _dot(ckv, wkn_ref[...])
    va_ref[0, 0] = _dot(ckv, wv_ref[...]).T.astype(BF16)
    kpe = rope_mla(proj(_C_KPE, LANES))
    q_scale = (MLA_NOPE + MLA_ROPE) ** -0.5 * LOG2_E
    for h in range(MLA_HEADS):
        sl = slice(h * LANES, (h + 1) * LANES)
        qa_ref[:, sl] = (rope_mla(q_raw[:, sl]) * q_scale).astype(BF16)
        ka_ref[:, sl] = (k_nope[:, sl] + kpe).astype(BF16)

    scale = HEAD_DIM ** -0.5
    dq = proj(_C_DQ, DIL_HEADS * HEAD_DIM)
    dk = proj(_C_DK, DIL_HEADS * HEAD_DIM)
    dv = proj(_C_DV, DIL_HEADS * HEAD_DIM)
    for c in range(_DIL_KV_CHUNKS):
        stage_ref[c] = rope64(dq[:, c * LANES:(c + 1) * LANES]) * (scale * LOG2_E)
        stage_ref[_DIL_KV_CHUNKS + c] = rope64(dk[:, c * LANES:(c + 1) * LANES])
        stage_ref[2 * _DIL_KV_CHUNKS + c] = dv[:, c * LANES:(c + 1) * LANES]
    for p, (_, r) in enumerate(DIL_PATTERNS):
        for t in range(3):
            o_ref = dil_refs[3 * p + t]
            for j in range(r):
                for c in range(_DIL_KV_CHUNKS):
                    rows = stage_ref[t * _DIL_KV_CHUNKS + c, pl.ds(j, tm // r, stride=r), :]
                    o_ref[0, j, :, c * LANES:(c + 1) * LANES] = rows.astype(BF16)

    nq = proj(_C_NQ, NSA_HEADS * HEAD_DIM)
    nq_ref[...] = (nq * scale).astype(BF16)
    for c in range(NSA_HEADS_PER_GROUP):
        sl = slice(c * LANES, (c + 1) * LANES)
        nqr_ref[:, sl] = (rope64(nq[:, sl]) * (scale * LOG2_E)).astype(BF16)
    kc_ref[...] = proj(_C_KC, LANES)
    vc_ref[...] = proj(_C_VC, LANES)
    ksl_ref[...] = rope64(proj(_C_KSL, LANES)).astype(BF16)
    vsl_ref[0, 0] = proj(_C_VSL, LANES).T.astype(BF16)
    kw_ref[...] = rope64(proj(_C_KW, LANES)).astype(BF16)
    vw_ref[...] = proj(_C_VW, LANES).astype(BF16)
    gl_ref[...] = jax.nn.sigmoid(proj(_C_GL, LANES))


def _proj_source_columns():
    src = -np.ones(_PROJ_WIDTH, np.int64)

    def put(dst, s0, n):
        src[dst:dst + n] = s0 + np.arange(n)

    put(_C_CQ, _SRC_CQ, MLA_Q_LORA)
    put(_C_CKV, _SRC_CKV, MLA_KV_LORA)
    put(_C_KPE + MLA_NOPE, _SRC_KPE, MLA_ROPE)
    put(_C_DQ, _SRC_DQ, DIL_HEADS * HEAD_DIM)
    put(_C_DK, _SRC_DK, DIL_HEADS * HEAD_DIM)
    put(_C_DV, _SRC_DV, DIL_HEADS * HEAD_DIM)
    for c in range(NSA_HEADS_PER_GROUP):
        for half, h in enumerate((c, c + NSA_HEADS_PER_GROUP)):
            put(_C_NQ + c * LANES + half * HEAD_DIM, _SRC_NQ + h * HEAD_DIM, HEAD_DIM)
    for dst, s0 in ((_C_KC, _SRC_KC), (_C_VC, _SRC_VC), (_C_KSL, _SRC_KSL), (_C_VSL, _SRC_VSL),
                    (_C_KW, _SRC_KW), (_C_VW, _SRC_VW)):
        put(dst, s0, NSA_KV_GROUPS * HEAD_DIM)
    put(_C_GL, _SRC_GL, NSA_HEADS * 3)
    return src


def _gather_columns(w, src, axis):
    pieces, i, n = [], 0, len(src)
    while i < n:
        j = i + 1
        if src[i] < 0:
            while j < n and src[j] < 0:
                j += 1
            shape = list(w.shape)
            shape[axis] = j - i
            pieces.append(jnp.zeros(shape, w.dtype))
        else:
            while j < n and src[j] == src[j - 1] + 1:
                j += 1
            pieces.append(lax.slice_in_dim(w, int(src[i]), int(src[i]) + j - i, axis=axis))
        i = j
    return jnp.concatenate(pieces, axis=axis)


def _mla_weight_layout(w_uq, w_ukv):
    dq = MLA_NOPE + MLA_ROPE
    src_q = -np.ones(MLA_HEADS * LANES, np.int64)
    src_kn = -np.ones(MLA_HEADS * LANES, np.int64)
    src_v = np.zeros(MLA_HEADS * MLA_V, np.int64)
    for h in range(MLA_HEADS):
        src_q[h * LANES:h * LANES + dq] = h * dq + np.arange(dq)
        src_kn[h * LANES:h * LANES + MLA_NOPE] = h * (MLA_NOPE + MLA_V) + np.arange(MLA_NOPE)
        src_v[h * MLA_V:(h + 1) * MLA_V] = h * (MLA_NOPE + MLA_V) + MLA_NOPE + np.arange(MLA_V)
    return (_gather_columns(w_uq, src_q, 1).astype(BF16),
            _gather_columns(w_ukv, src_kn, 1).astype(BF16),
            _gather_columns(w_ukv, src_v, 1).astype(BF16))


def _rope_tables(S):
    def cos_sin(dim):
        inv_freq = ROPE_THETA ** (-jnp.arange(0, dim, 2, dtype=F32) / dim)
        ang = jnp.arange(S, dtype=F32)[:, None] * inv_freq[None, :]
        return jnp.cos(ang), jnp.sin(ang)

    cos, sin = cos_sin(HEAD_DIM)
    zero = jnp.zeros_like(sin)
    c64 = jnp.concatenate([cos, cos] * 2, axis=1)
    sa64 = jnp.concatenate([-sin, zero] * 2, axis=1)
    sb64 = jnp.concatenate([zero, sin] * 2, axis=1)
    cos_m, sin_m = cos_sin(MLA_ROPE)
    ones = jnp.ones((S, MLA_NOPE), F32)
    z64 = jnp.zeros((S, MLA_NOPE), F32)
    z16 = jnp.zeros_like(sin_m)
    z32 = jnp.zeros((S, LANES - MLA_NOPE - MLA_ROPE), F32)
    cm = jnp.concatenate([ones, cos_m, cos_m, z32], axis=1)
    sam = jnp.concatenate([z64, -sin_m, z16, z32], axis=1)
    sbm = jnp.concatenate([z64, z16, sin_m, z32], axis=1)
    return c64, sa64, sb64, cm, sam, sbm


def _input_projection(h, w_big, q_norm, kv_norm, wuq, wkn, wv, tables, S, *, tm=KV_TILE):
    T = h.shape[0]
    n_pos = S // tm
    row = lambda w: pl.BlockSpec((tm, w), lambda i: (i, 0))
    full = lambda a: pl.BlockSpec(a.shape, lambda i: (0, 0))
    tab = pl.BlockSpec((tm, LANES), lambda i: (i % n_pos, 0))
    B = T // S
    specs, shapes = [], []

    def rows_out(w, dt):
        specs.append(row(w))
        shapes.append(jax.ShapeDtypeStruct((T, w), dt))

    def tile_t_out(c):
        specs.append(pl.BlockSpec((1, 1, c, tm), lambda i: (i // n_pos, i % n_pos, 0, 0)))
        shapes.append(jax.ShapeDtypeStruct((B, n_pos, c, tm), BF16))

    def residue_out(r, w):
        specs.append(pl.BlockSpec((1, r, tm // r, w), lambda i: (i // n_pos, 0, i % n_pos, 0)))
        shapes.append(jax.ShapeDtypeStruct((B, r, S // r, w), BF16))

    rows_out(MLA_HEADS * LANES, BF16)
    rows_out(MLA_HEADS * LANES, BF16)
    tile_t_out(MLA_HEADS * MLA_V)
    for _, r in DIL_PATTERNS:
        residue_out(r, DIL_HEADS * HEAD_DIM)
        residue_out(r, DIL_HEADS * HEAD_DIM)
        residue_out(r, DIL_HEADS * HEAD_DIM)
    rows_out(NSA_HEADS * HEAD_DIM, BF16)
    rows_out(NSA_HEADS * HEAD_DIM, BF16)
    rows_out(LANES, F32)
    rows_out(LANES, F32)
    rows_out(LANES, BF16)
    tile_t_out(LANES)
    rows_out(LANES, BF16)
    rows_out(LANES, BF16)
    rows_out(LANES, F32)
    qn = q_norm.reshape(1, -1)
    kvn = kv_norm.reshape(1, -1)
    return pl.pallas_call(
        _proj_kernel,
        grid=(T // tm,),
        in_specs=[row(D_MODEL), full(w_big), full(qn), full(kvn), full(wuq), full(wkn), full(wv)]
                 + [tab] * 6,
        out_specs=specs,
        out_shape=shapes,
        scratch_shapes=[pltpu.VMEM((3 * _DIL_KV_CHUNKS, tm, LANES), F32)],
        compiler_params=_cparams("parallel"),
        name="in_proj",
    )(h, w_big, qn, kvn, wuq, wkn, wv, *tables)


ONES_ROWS = 16


def _flash_step_t(s, vt1s, m_ref, acc_ref, idx, cols=slice(None)):
    m_prev = m_ref[idx, :, cols]
    m_new = jnp.maximum(m_prev, jnp.max(s, axis=0, keepdims=True))
    alpha = jnp.exp2(m_prev - m_new)
    p = jnp.exp2(s - m_new).astype(BF16)
    pv = functools.reduce(jnp.add, [_dot(vt1, p[j * KV_TILE:(j + 1) * KV_TILE]) for j, vt1 in enumerate(vt1s)])
    acc_ref[idx, :, cols] = alpha * acc_ref[idx, :, cols] + pv
    m_ref[idx, :, cols] = m_new


def _causal_key_loop(q0, tq, tk, tile, *, pairs):
    n_full = q0 // tk
    if pairs:
        def pair(i, carry):
            tile(2 * i, 2, False, 0)
            return carry

        lax.fori_loop(0, lax.shift_right_logical(n_full, 1), pair, 0)

        @pl.when(lax.rem(n_full, 2) == 1)
        def _():
            tile(n_full - 1, 1, False, 0)
    else:
        def single(i, carry):
            tile(i, 1, False, 0)
            return carry

        lax.fori_loop(0, n_full, single, 0)

    for d in range(tq // tk):
        tile(n_full + d, 1, True, d * tk)


def _init_softmax_state(m_ref, acc_ref):
    m_ref[...] = jnp.full(m_ref.shape, NEG_INF, F32)
    acc_ref[...] = jnp.zeros_like(acc_ref)


def _flash_state(slots, channels, queries):
    return [pltpu.VMEM((slots, 1, queries), F32), pltpu.VMEM((slots, channels + ONES_ROWS, queries), F32)]


def _split_pair_t(pair):
    t = pair.T
    top = lax.broadcasted_iota(jnp.int32, t.shape, 0) < HEAD_DIM
    zero = jnp.zeros_like(t)
    return jnp.where(top, t, zero), jnp.where(top, zero, t)


def _with_ones_rows(vt):
    return jnp.concatenate([vt, jnp.ones((ONES_ROWS, vt.shape[1]), vt.dtype)], axis=0)


def _normalised(acc, channels):
    return acc[:channels] / acc[channels:channels + 1]


def _mla_kernel(q_ref, k_ref, vt_ref, o_ref, qt_ref, m_ref, acc_ref, *, tq, tk):
    q0 = pl.program_id(1) * tq
    _init_softmax_state(m_ref, acc_ref)
    for h in range(MLA_HEADS):
        qt_ref[h] = q_ref[0, :, h * LANES:(h + 1) * LANES].T

    def tile(kt, ntiles, masked, q_lo):
        start = pl.multiple_of(kt * tk, tk)
        keys = ntiles * tk
        cols = slice(q_lo, tq)
        if masked:
            kpos = start + lax.broadcasted_iota(jnp.int32, (keys, tq - q_lo), 0)
            qpos = q0 + q_lo + lax.broadcasted_iota(jnp.int32, (keys, tq - q_lo), 1)

        def scores(h):
            s = _dot(k_ref[0, pl.ds(start, keys), h * LANES:(h + 1) * LANES], qt_ref[h, :, cols])
            return jnp.where(kpos <= qpos, s, NEG_INF) if masked else s

        s_next = scores(0)
        for h in range(MLA_HEADS):
            s = s_next
            if h + 1 < MLA_HEADS:
                s_next = scores(h + 1)
            vt1s = [_with_ones_rows(vt_ref[0, kt + j, (h // 2) * LANES:(h // 2 + 1) * LANES, :])
                    for j in range(ntiles)]
            _flash_step_t(s, vt1s, m_ref, acc_ref, h, cols)

    _causal_key_loop(q0, tq, tk, tile, pairs=True)
    even_rows = lax.broadcasted_iota(jnp.int32, (LANES, tq), 0) < MLA_V
    for pair in range(MLA_HEADS // 2):
        h0, h1 = 2 * pair, 2 * pair + 1
        out_t = jnp.where(even_rows, _normalised(acc_ref[h0], LANES), _normalised(acc_ref[h1], LANES))
        o_ref[0, :, pair * LANES:(pair + 1) * LANES] = out_t.T.astype(o_ref.dtype)


def _mla_attention(q, k, vt, *, tq=1024, tk=KV_TILE):
    B, S, _ = q.shape
    return pl.pallas_call(
        functools.partial(_mla_kernel, tq=tq, tk=tk),
        grid=(B, S // tq),
        in_specs=[
            pl.BlockSpec((1, tq, MLA_HEADS * LANES), lambda b, i: (b, i, 0)),
            pl.BlockSpec((1, S, MLA_HEADS * LANES), lambda b, i: (b, 0, 0)),
            pl.BlockSpec((1, S // tk, MLA_HEADS * MLA_V, tk), lambda b, i: (b, 0, 0, 0)),
        ],
        out_specs=pl.BlockSpec((1, tq, MLA_HEADS * MLA_V), lambda b, i: (b, i, 0)),
        out_shape=jax.ShapeDtypeStruct((B, S, MLA_HEADS * MLA_V), BF16),
        scratch_shapes=[pltpu.VMEM((MLA_HEADS, LANES, tq), BF16)] + _flash_state(MLA_HEADS, LANES, tq),
        compiler_params=_cparams("parallel", "arbitrary"),
        name="mla_attn",
    )(q, k, vt)


def _dot_tn(a, b):
    return lax.dot_general(a, b, (((0,), (0,)), ((), ())), preferred_element_type=F32)


def _band_mask_t(q0, kstart, tq, span, window):
    kpos = kstart + lax.broadcasted_iota(jnp.int32, (span, tq), 0)
    qpos = q0 + lax.broadcasted_iota(jnp.int32, (span, tq), 1)
    dist = qpos - kpos
    return (dist >= 0) & (dist <= window)


def _dil_kernel(q_ref, k_ref, v_ref, o_ref, lse_ref, qt_ref, *, tq, sub, wpad, window):
    q0 = pl.program_id(2) * tq
    n = k_ref.shape[2]
    span = min(sub + wpad, n)
    even_rows = lax.broadcasted_iota(jnp.int32, (LANES, sub), 0) < HEAD_DIM
    windows = []
    for a in range(tq // sub):
        qs = q0 + a * sub
        kstart = pl.multiple_of(jnp.clip(qs - wpad, 0, n - span), LANES)
        windows.append((kstart, _band_mask_t(qs, kstart, sub, span, window)))
    chains = [(a, c, hh) for a in range(tq // sub) for c in range(DIL_HEADS // 2) for hh in range(2)]
    for a in range(tq // sub):
        for c in range(DIL_HEADS // 2):
            qt_ref[a, 2 * c], qt_ref[a, 2 * c + 1] = _split_pair_t(
                q_ref[0, 0, a * sub:(a + 1) * sub, c * LANES:(c + 1) * LANES])

    def scores(a, c, hh):
        kstart, mask = windows[a]
        k = k_ref[0, 0, pl.ds(kstart, span), c * LANES:(c + 1) * LANES]
        return jnp.where(mask, _dot(k, qt_ref[a, 2 * c + hh]), NEG_INF)

    s_next = scores(*chains[0])
    outs = []
    head_row = lax.broadcasted_iota(jnp.int32, (SUBLANES, sub), 0)
    lse_rows = jnp.zeros((SUBLANES, sub), F32)
    for i, (a, c, hh) in enumerate(chains):
        s = s_next
        if i + 1 < len(chains):
            s_next = scores(*chains[i + 1])
        m = jnp.max(s, axis=0, keepdims=True)
        e = jnp.exp2(s - m)
        l = jnp.sum(e, axis=0, keepdims=True)
        v = v_ref[0, 0, pl.ds(windows[a][0], span), c * LANES:(c + 1) * LANES]
        outs.append(_dot_tn(v, e.astype(BF16)) * (1.0 / l))
        lse_rows = jnp.where(head_row == 2 * c + hh, m * LN_2 + jnp.log(l), lse_rows)
        rows = slice(a * sub, (a + 1) * sub)
        if hh == 1:
            o_ref[0, 0, rows, c * LANES:(c + 1) * LANES] = (
                jnp.where(even_rows, outs[-2], outs[-1]).T.astype(o_ref.dtype))
        if (c, hh) == (DIL_HEADS // 2 - 1, 1):
            padded = jnp.concatenate([lse_rows, jnp.zeros((LANES - SUBLANES, sub), F32)], axis=0)
            lse_ref[0, 0, rows, :] = padded.T


def _dilated_pattern(q, k, v, window, dil, *, tq=1024, sub=256):
    B, _, n, wq = q.shape
    wk = k.shape[-1]
    wsub = window // dil
    wpad = -(-wsub // LANES) * LANES
    tq = min(tq, n)
    sub = min(sub, tq)
    return pl.pallas_call(
        functools.partial(_dil_kernel, tq=tq, sub=sub, wpad=wpad, window=wsub),
        grid=(B, dil, n // tq),
        in_specs=[
            pl.BlockSpec((1, 1, tq, wq), lambda b, j, i: (b, j, i, 0)),
            pl.BlockSpec((1, 1, n, wk), lambda b, j, i: (b, j, 0, 0)),
            pl.BlockSpec((1, 1, n, wk), lambda b, j, i: (b, j, 0, 0)),
        ],
        out_specs=[pl.BlockSpec((1, 1, tq, wk), lambda b, j, i: (b, j, i, 0)),
                   pl.BlockSpec((1, 1, tq, LANES), lambda b, j, i: (b, j, i, 0))],
        out_shape=[jax.ShapeDtypeStruct((B, dil, n, wk), BF16), jax.ShapeDtypeStruct((B, dil, n, LANES), F32)],
        scratch_shapes=[pltpu.VMEM((tq // sub, DIL_HEADS, LANES, sub), BF16)],
        compiler_params=_cparams("parallel", "parallel", "arbitrary"),
        name=f"dilated_r{dil}",
    )(q, k, v)


def _compress_kernel(xk_ref, xv_ref, pos_ref, w1_ref, w2_ref, kc_ref, vct_ref):
    st = NSA_CMP_STRIDE
    n = xk_ref.shape[1] // st
    for t, x_ref in enumerate((xk_ref, xv_ref)):
        out = jnp.zeros((n, LANES), F32)
        for g in range(NSA_KV_GROUPS):
            first = jnp.zeros((n, NSA_CMP_HIDDEN), F32)
            second = jnp.zeros((n, NSA_CMP_HIDDEN), F32)
            for l in range(st):
                x = x_ref[0, pl.ds(l, n, stride=st), :]
                first = first + _dot((x + pos_ref[t, l]).astype(BF16), w1_ref[t, g, l])
                second = second + _dot((x + pos_ref[t, st + l]).astype(BF16), w1_ref[t, g, st + l])
            hid = first + pltpu.roll(second, n - 1, 0)
            hid = (hid * jax.nn.sigmoid(hid)).astype(BF16)
            out = out + _dot(hid, w2_ref[t, g])
        if t == 0:
            kc_ref[0] = out.astype(kc_ref.dtype)
        else:
            vct_ref[0] = out.T.astype(vct_ref.dtype)


def _nsa_compress(kc, vc, cmp_pos, cmp_w1, cmp_w2):
    B, S, _ = kc.shape
    G, d = NSA_KV_GROUPS, HEAD_DIM
    n = S // NSA_CMP_STRIDE
    w1 = cmp_w1.reshape(2, NSA_CMP_BLOCK, d, NSA_CMP_HIDDEN)
    zero1 = jnp.zeros_like(w1)
    w1g = jnp.stack([jnp.concatenate([w1, zero1], axis=2), jnp.concatenate([zero1, w1], axis=2)], axis=1)
    zero2 = jnp.zeros_like(cmp_w2)
    w2g = jnp.stack([jnp.concatenate([cmp_w2, zero2], axis=2), jnp.concatenate([zero2, cmp_w2], axis=2)], axis=1)
    pos = jnp.concatenate([cmp_pos] * G, axis=-1).reshape(2, NSA_CMP_BLOCK, 1, G * d)
    w1g, w2g = w1g.astype(BF16), w2g.astype(BF16)
    xspec = pl.BlockSpec((1, S, G * d), lambda b: (b, 0, 0))
    full = lambda a: pl.BlockSpec(a.shape, lambda b: (0,) * a.ndim)
    return pl.pallas_call(
        _compress_kernel,
        grid=(B,),
        in_specs=[xspec, xspec, full(pos), full(w1g), full(w2g)],
        out_specs=[pl.BlockSpec((1, n, G * d), lambda b: (b, 0, 0)),
                   pl.BlockSpec((1, G * d, n), lambda b: (b, 0, 0))],
        out_shape=[jax.ShapeDtypeStruct((B, n, G * d), BF16), jax.ShapeDtypeStruct((B, G * d, n), BF16)],
        compiler_params=_cparams("parallel"),
        name="nsa_compress",
    )(kc, vc, pos, w1g, w2g)


def _dot_01_by_f32(o01, p):
    p1 = p.astype(BF16)
    r1 = p - p1.astype(F32)
    p2 = r1.astype(BF16)
    p3 = (r1 - p2.astype(F32)).astype(BF16)
    return _dot(o01, p1) + _dot(o01, p2) + _dot(o01, p3)


SUBLANES = 8
RANK_TARGET_GROUPS = 4


def _count_outranking(sc_ref, cnt_ref, n_sources, tq):
    groups = SEL_LANES // SUBLANES
    sub = lax.broadcasted_iota(jnp.int32, (SUBLANES, tq), 0)
    cnt_ref[...] = jnp.zeros_like(cnt_ref)
    for bi in range(groups):
        @pl.when(bi * SUBLANES < n_sources)
        def _():
            src = sc_ref[bi * SUBLANES:(bi + 1) * SUBLANES, :]
            rows = [jnp.broadcast_to(src[ii:ii + 1, :], (SUBLANES, tq)) for ii in range(SUBLANES)]

            def count_targets(first, last):
                for r in range(first, last):
                    rsl = slice(r * SUBLANES, (r + 1) * SUBLANES)
                    tgt = sc_ref[rsl, :]
                    cnt = cnt_ref[rsl, :]
                    for ii, row in enumerate(rows):
                        if r > bi:
                            ahead = row >= tgt
                        elif r < bi:
                            ahead = row > tgt
                        else:
                            ahead = (row > tgt) | ((row == tgt) & (sub > ii))
                        cnt = cnt + jnp.where(ahead, 1.0, 0.0)
                    cnt_ref[rsl, :] = cnt

            for first in range(0, groups, RANK_TARGET_GROUPS):
                @pl.when(first * SUBLANES < n_sources)
                def _(first=first):
                    count_targets(first, min(first + RANK_TARGET_GROUPS, groups))


def _nsa_head_of_chunk(c, g):
    return c + g * NSA_HEADS_PER_GROUP


def _cmp_kernel(q_ref, kc_ref, vct_ref, ovt_ref, o_ref, bias_ref, qt_ref, sc_ref, cnt_ref, *, tq):
    q0 = pl.program_id(1) * tq
    n = kc_ref.shape[1]
    top = lax.broadcasted_iota(jnp.int32, (LANES, tq), 0) < HEAD_DIM
    for c in range(_NSA_Q_CHUNKS):
        qt_ref[_nsa_head_of_chunk(c, 0)], qt_ref[_nsa_head_of_chunk(c, 1)] = _split_pair_t(
            q_ref[0, :, c * LANES:(c + 1) * LANES])
    blk = lax.broadcasted_iota(jnp.int32, (SEL_LANES, tq), 0)
    cur = lax.shift_right_logical(q0 + lax.broadcasted_iota(jnp.int32, (SEL_LANES, tq), 1), 6)
    forced = (blk == 0) | (blk == cur) | (blk == cur - 1)

    def attend(nk):
        kc = kc_ref[0, :nk, :]
        vct = vct_ref[0, :, :nk]
        qpos = q0 + lax.broadcasted_iota(jnp.int32, (nk, tq), 1)
        cmp_end = lax.broadcasted_iota(jnp.int32, (nk, tq), 0) * NSA_CMP_STRIDE + (NSA_CMP_BLOCK - 1)
        visible = cmp_end <= qpos

        def scores(h):
            return jnp.where(visible, _dot(kc, qt_ref[h]), NEG_INF)

        psum = [jnp.zeros((nk, tq), F32) for _ in range(NSA_KV_GROUPS)]
        outs = [None] * NSA_HEADS
        s_next = scores(0)
        for h in range(NSA_HEADS):
            s = s_next
            if h + 1 < NSA_HEADS:
                s_next = scores(h + 1)
            m = jnp.max(s, axis=0, keepdims=True)
            e = jnp.where(visible, jnp.exp(s - m), 0.0)
            l = jnp.maximum(jnp.sum(e, axis=0, keepdims=True), 1e-30)
            p = e * (1.0 / l)
            outs[h] = _dot(vct, p.astype(BF16))
            psum[h // NSA_HEADS_PER_GROUP] = psum[h // NSA_HEADS_PER_GROUP] + p
        for c in range(_NSA_Q_CHUNKS):
            pair_t = jnp.where(top, outs[_nsa_head_of_chunk(c, 0)], outs[_nsa_head_of_chunk(c, 1)])
            o_ref[0, :, c * LANES:(c + 1) * LANES] = pair_t.T.astype(o_ref.dtype)
        for g in range(NSA_KV_GROUPS):
            imp_t = _dot_01_by_f32(ovt_ref[:, :nk], psum[g])
            sc_ref[g] = jnp.where(forced, FORCE_SCORE, jnp.where(blk <= cur, imp_t, -FORCE_SCORE))

    step = LANES if n % LANES == 0 else n
    needed = lax.div(q0 + tq + NSA_CMP_STRIDE * step - 1, NSA_CMP_STRIDE * step)
    for v in range(1, n // step + 1):
        @pl.when(jnp.minimum(needed, n // step) == v)
        def _(nk=v * step):
            attend(nk)

    n_causal = jnp.minimum(lax.shift_right_logical(q0 + tq - 1, 6) + 1, SEL_LANES)
    for g in range(NSA_KV_GROUPS):
        _count_outranking(sc_ref.at[g], cnt_ref.at[g], n_causal, tq)
        bias_t = jnp.where(cnt_ref[g] < float(NSA_TOP_N), 0.0, UNSELECTED_BIAS)
        bias_ref[0, g] = bias_t.astype(bias_ref.dtype)


def _nsa_compressed_branch(nq, kcc, vcct, overlap, *, tq=256):
    B, S, wq = nq.shape
    G = NSA_KV_GROUPS
    n = kcc.shape[1]
    return pl.pallas_call(
        functools.partial(_cmp_kernel, tq=tq),
        grid=(B, S // tq),
        in_specs=[
            pl.BlockSpec((1, tq, wq), lambda b, i: (b, i, 0)),
            pl.BlockSpec((1, n, LANES), lambda b, i: (b, 0, 0)),
            pl.BlockSpec((1, LANES, n), lambda b, i: (b, 0, 0)),
            pl.BlockSpec((SEL_LANES, n), lambda b, i: (0, 0)),
        ],
        out_specs=[
            pl.BlockSpec((1, tq, wq), lambda b, i: (b, i, 0)),
            pl.BlockSpec((1, G, SEL_LANES, tq), lambda b, i: (b, 0, 0, i)),
        ],
        out_shape=[jax.ShapeDtypeStruct((B, S, wq), BF16),
                   jax.ShapeDtypeStruct((B, G, SEL_LANES, S), BF16)],
        scratch_shapes=[pltpu.VMEM((NSA_HEADS, LANES, tq), BF16),
                        pltpu.VMEM((G, SEL_LANES, tq), F32), pltpu.VMEM((G, SEL_LANES, tq), F32)],
        compiler_params=_cparams("parallel", "arbitrary"),
        name="nsa_cmp_topk",
    )(nq, kcc, vcct, overlap.T)


def _sel_kernel(q_ref, bias_ref, k_ref, vt_ref, oh_ref, o_ref, qa_ref, m_ref, acc_ref, *, tq, tk):
    q0 = pl.program_id(1) * tq
    for c in range(_NSA_Q_CHUNKS):
        h0, h1 = _nsa_head_of_chunk(c, 0), _nsa_head_of_chunk(c, 1)
        qa_ref[h0, :LANES, :], qa_ref[h1, :LANES, :] = _split_pair_t(q_ref[0, :, c * LANES:(c + 1) * LANES])
    for h in range(NSA_HEADS):
        qa_ref[h, LANES:, :] = bias_ref[0, h // NSA_HEADS_PER_GROUP]
    _init_softmax_state(m_ref, acc_ref)

    def tile(kt, ntiles, masked, q_lo):
        start = pl.multiple_of(kt * tk, tk)
        keys = ntiles * tk
        cols = slice(q_lo, tq)
        if masked:
            kpos = start + lax.broadcasted_iota(jnp.int32, (keys, tq - q_lo), 0)
            qpos = q0 + q_lo + lax.broadcasted_iota(jnp.int32, (keys, tq - q_lo), 1)
        ka = jnp.concatenate([k_ref[0, pl.ds(start, keys), :], oh_ref[pl.ds(start, keys), :]], axis=1)
        vt1s = [_with_ones_rows(vt_ref[0, kt + j]) for j in range(ntiles)]

        def scores(h):
            s = _dot(ka, qa_ref[h, :, cols])
            return jnp.where(kpos <= qpos, s, NEG_INF) if masked else s

        s_next = scores(0)
        for h in range(NSA_HEADS):
            s = s_next
            if h + 1 < NSA_HEADS:
                s_next = scores(h + 1)
            _flash_step_t(s, vt1s, m_ref, acc_ref, h, cols)

    _causal_key_loop(q0, tq, tk, tile, pairs=True)
    top = lax.broadcasted_iota(jnp.int32, (LANES, tq), 0) < HEAD_DIM
    for c in range(_NSA_Q_CHUNKS):
        pair_t = jnp.where(top, _normalised(acc_ref[_nsa_head_of_chunk(c, 0)], LANES),
                           _normalised(acc_ref[_nsa_head_of_chunk(c, 1)], LANES))
        o_ref[0, :, c * LANES:(c + 1) * LANES] = pair_t.T.astype(o_ref.dtype)


def _nsa_selected_branch(nqr, bias, ksl, vslt, onehot, *, tq=1024, tk=KV_TILE):
    B, S, wq = nqr.shape
    G = NSA_KV_GROUPS
    return pl.pallas_call(
        functools.partial(_sel_kernel, tq=tq, tk=tk),
        grid=(B, S // tq),
        in_specs=[
            pl.BlockSpec((1, tq, wq), lambda b, i: (b, i, 0)),
            pl.BlockSpec((1, G, SEL_LANES, tq), lambda b, i: (b, 0, 0, i)),
            pl.BlockSpec((1, S, LANES), lambda b, i: (b, 0, 0)),
            pl.BlockSpec((1, S // tk, LANES, tk), lambda b, i: (b, 0, 0, 0)),
            pl.BlockSpec((S, SEL_LANES), lambda b, i: (0, 0)),
        ],
        out_specs=pl.BlockSpec((1, tq, wq), lambda b, i: (b, i, 0)),
        out_shape=jax.ShapeDtypeStruct((B, S, wq), BF16),
        scratch_shapes=[pltpu.VMEM((NSA_HEADS, LANES + SEL_LANES, tq), BF16)]
                       + _flash_state(NSA_HEADS, LANES, tq),
        compiler_params=_cparams("parallel", "arbitrary"),
        name="nsa_selected",
    )(nqr, bias, ksl, vslt, onehot)


def _win_kernel(q_ref, k_ref, v_ref, o_ref, qt_ref, *, tq, wpad, window):
    q0 = pl.program_id(1) * tq
    span = tq + wpad
    kstart = pl.multiple_of(jnp.maximum(q0 - wpad, 0), LANES)
    mask = _band_mask_t(q0, kstart, tq, span, window)
    k = k_ref[0, pl.ds(kstart, span), :]
    v = v_ref[0, pl.ds(kstart, span), :]
    top = lax.broadcasted_iota(jnp.int32, (LANES, tq), 0) < HEAD_DIM
    for c in range(_NSA_Q_CHUNKS):
        qt_ref[_nsa_head_of_chunk(c, 0)], qt_ref[_nsa_head_of_chunk(c, 1)] = _split_pair_t(
            q_ref[0, :, c * LANES:(c + 1) * LANES])

    def scores(h):
        return jnp.where(mask, _dot(k, qt_ref[h]), NEG_INF)

    outs = [None] * NSA_HEADS
    s_next = scores(0)
    for h in range(NSA_HEADS):
        s = s_next
        if h + 1 < NSA_HEADS:
            s_next = scores(h + 1)
        m = jnp.max(s, axis=0, keepdims=True)
        e = jnp.exp2(s - m)
        l = jnp.sum(e, axis=0, keepdims=True)
        outs[h] = _dot_tn(v, e.astype(BF16)) * (1.0 / l)
    for c in range(_NSA_Q_CHUNKS):
        pair_t = jnp.where(top, outs[_nsa_head_of_chunk(c, 0)], outs[_nsa_head_of_chunk(c, 1)])
        o_ref[0, :, c * LANES:(c + 1) * LANES] = pair_t.T.astype(o_ref.dtype)


def _nsa_window_branch(nqr, kw, vw, *, tq=256):
    B, S, wq = nqr.shape
    wpad = -(-NSA_WINDOW // LANES) * LANES
    return pl.pallas_call(
        functools.partial(_win_kernel, tq=tq, wpad=wpad, window=NSA_WINDOW),
        grid=(B, S // tq),
        in_specs=[
            pl.BlockSpec((1, tq, wq), lambda b, i: (b, i, 0)),
            pl.BlockSpec((1, S, LANES), lambda b, i: (b, 0, 0)),
            pl.BlockSpec((1, S, LANES), lambda b, i: (b, 0, 0)),
        ],
        out_specs=pl.BlockSpec((1, tq, wq), lambda b, i: (b, i, 0)),
        out_shape=jax.ShapeDtypeStruct((B, S, wq), BF16),
        scratch_shapes=[pltpu.VMEM((NSA_HEADS, LANES, tq), BF16)],
        compiler_params=_cparams("parallel", "arbitrary"),
        name="nsa_window",
    )(nqr, kw, vw)


def _selection_constants(S):
    n_cmp_rows = S // NSA_CMP_STRIDE
    c = np.arange(n_cmp_rows)
    cmp_start = c * NSA_CMP_STRIDE
    cmp_end = cmp_start + NSA_CMP_BLOCK - 1
    sel_start = np.arange(SEL_LANES) * NSA_SEL_BLOCK
    overlap = ((cmp_start[:, None] < sel_start[None, :] + NSA_SEL_BLOCK) &
               (cmp_end[:, None] >= sel_start[None, :]))
    n_cmp = (S - NSA_CMP_BLOCK) // NSA_CMP_STRIDE + 1
    overlap &= (c < n_cmp)[:, None]
    onehot = (np.arange(S)[:, None] // NSA_SEL_BLOCK) == np.arange(SEL_LANES)[None, :]
    return jnp.asarray(overlap, BF16), jnp.asarray(onehot, BF16)


def _out_kernel(*refs):
    n_pat = len(DIL_PATTERNS)
    x_ref, oa_ref = refs[:2]
    dil_refs = refs[2:2 + 2 * n_pat]
    (oc_ref, os_ref, ow_ref, gl_ref, wa_ref, wb_ref, wc_ref, g_ref, b_ref, o_ref,
     nat_ref) = refs[2 + 2 * n_pat:]
    tm = x_ref.shape[0]
    mix = _dot(oa_ref[...], wa_ref[...])

    lse_chunk = _DIL_KV_CHUNKS
    for p, (_, r) in enumerate(DIL_PATTERNS):
        o_ref_p, lse_ref_p = dil_refs[2 * p], dil_refs[2 * p + 1]
        for j in range(r):
            rows = pl.ds(j, tm // r, stride=r)
            for c in range(_DIL_KV_CHUNKS):
                nat_ref[p, c, rows, :] = o_ref_p[0, j, :, c * LANES:(c + 1) * LANES].astype(F32)
            nat_ref[p, lse_chunk, rows, :] = lse_ref_p[0, j]
    low_half = _lane_half((tm, LANES)) == 0
    for c in range(_DIL_KV_CHUNKS):
        outs = [nat_ref[p, c] for p in range(n_pat)]
        lses = [jnp.where(low_half, nat_ref[p, lse_chunk, :, 2 * c:2 * c + 1],
                          nat_ref[p, lse_chunk, :, 2 * c + 1:2 * c + 2]) for p in range(n_pat)]
        mx = functools.reduce(jnp.maximum, lses)
        es = [jnp.exp(l - mx) for l in lses]
        inv = 1.0 / functools.reduce(jnp.add, es)
        ob = functools.reduce(jnp.add, [(e * inv) * o for e, o in zip(es, outs)])
        mix = mix + _dot(ob.astype(BF16), wb_ref[c * LANES:(c + 1) * LANES, :])

    gl = gl_ref[...]
    for c in range(_NSA_Q_CHUNKS):
        sl = slice(c * LANES, (c + 1) * LANES)
        h0, h1 = _nsa_head_of_chunk(c, 0), _nsa_head_of_chunk(c, 1)
        oc = jnp.zeros((tm, LANES), F32)
        for k, branch_ref in enumerate((oc_ref, os_ref, ow_ref)):
            gate = jnp.where(low_half, gl[:, 3 * h0 + k:3 * h0 + k + 1], gl[:, 3 * h1 + k:3 * h1 + k + 1])
            oc = oc + gate * branch_ref[:, sl].astype(F32)
        mix = mix + _dot(oc.astype(BF16), wc_ref[sl, :])

    y = ALPHA * x_ref[...] + mix
    o_ref[...] = _layer_norm(y, g_ref[...], b_ref[...])


def _out_weight_layout(w_out):
    na = MLA_HEADS * MLA_V
    nb = DIL_HEADS * HEAD_DIM
    src = -np.ones(NSA_HEADS * HEAD_DIM, np.int64)
    for c in range(_NSA_Q_CHUNKS):
        for g in range(NSA_KV_GROUPS):
            d0 = c * LANES + g * HEAD_DIM
            src[d0:d0 + HEAD_DIM] = na + nb + _nsa_head_of_chunk(c, g) * HEAD_DIM + np.arange(HEAD_DIM)
    return (w_out[:na].astype(BF16), w_out[na:na + nb].astype(BF16),
            _gather_columns(w_out, src, 0).astype(BF16))


def _output_projection(x, oa, dil, oc, osl, ow, gl, wa, wb, wc, gain, bias, S, *, tm=512):
    T = x.shape[0]
    n_pos = S // tm
    row = lambda a: pl.BlockSpec((tm, a.shape[1]), lambda i: (i, 0))
    full = lambda a: pl.BlockSpec(a.shape, lambda i: (0, 0))
    residue = lambda a: pl.BlockSpec((1, a.shape[1], tm // a.shape[1], a.shape[3]),
                                     lambda i: (i // n_pos, 0, i % n_pos, 0))
    g2, b2 = gain.reshape(1, -1), bias.reshape(1, -1)
    dil_flat = [a for pair in dil for a in pair]
    rows = [oc, osl, ow, gl]
    consts = [wa, wb, wc, g2, b2]
    return pl.pallas_call(
        _out_kernel,
        grid=(T // tm,),
        in_specs=[row(x), row(oa)] + [residue(a) for a in dil_flat] + [row(a) for a in rows]
                 + [full(a) for a in consts],
        out_specs=pl.BlockSpec((tm, D_MODEL), lambda i: (i, 0)),
        out_shape=jax.ShapeDtypeStruct((T, D_MODEL), F32),
        scratch_shapes=[pltpu.VMEM((len(dil), _DIL_KV_CHUNKS + 1, tm, LANES), F32)],
        compiler_params=_cparams("parallel"),
        name="out_proj_ln",
    )(x, oa, *dil_flat, *rows, *consts)


def kernel(x, ffn_w_in, ffn_w_out, ln_gain, ln_bias, w_in, w_out, mla_q_norm, mla_kv_norm,
           mla_w_uq, mla_w_ukv, nsa_cmp_pos, nsa_cmp_w1, nsa_cmp_w2):
    B, S, D = x.shape
    assert D == D_MODEL and S % 2048 == 0 and S // NSA_SEL_BLOCK <= SEL_LANES
    T = B * S
    tables = _rope_tables(S)
    overlap, onehot = _selection_constants(S)
    proj_src = _proj_source_columns()
    bs = lambda t: t.reshape(B, S, t.shape[-1])

    xf = x.reshape(T, D)
    for l in range(DEPTH):
        xf = _ffn_ln(xf, ffn_w_in, ffn_w_out, l, 0, ln_gain[l, 0], ln_bias[l, 0])

        w_big = _gather_columns(w_in[l], proj_src, 1).astype(BF16)
        wuq, wkn, wv = _mla_weight_layout(mla_w_uq[l], mla_w_ukv[l])
        outs = _input_projection(xf, w_big, mla_q_norm[l], mla_kv_norm[l], wuq, wkn, wv, tables, S)
        qa, ka, va = outs[:3]
        n_dil = 3 * len(DIL_PATTERNS)
        dqkv = outs[3:3 + n_dil]
        nq, nqr, kc, vc, ksl, vsl, kw, vw, gl = outs[3 + n_dil:]

        oa = _mla_attention(bs(qa), bs(ka), va).reshape(T, -1)
        dil = [_dilated_pattern(*dqkv[3 * p:3 * p + 3], window, r)
               for p, (window, r) in enumerate(DIL_PATTERNS)]
        kcc, vcc = _nsa_compress(bs(kc), bs(vc), nsa_cmp_pos[l], nsa_cmp_w1[l], nsa_cmp_w2[l])
        oc, sel_bias = _nsa_compressed_branch(bs(nq), kcc, vcc, overlap)
        osl = _nsa_selected_branch(bs(nqr), sel_bias, bs(ksl), vsl, onehot)
        ow = _nsa_window_branch(bs(nqr), bs(kw), bs(vw))

        wa, wb, wc = _out_weight_layout(w_out[l])
        xf = _output_projection(xf, oa, dil, oc.reshape(T, -1), osl.reshape(T, -1), ow.reshape(T, -1),
                                gl, wa, wb, wc, ln_gain[l, 1], ln_bias[l, 1], S)

        xf = _ffn_ln(xf, ffn_w_in, ffn_w_out, l, 1, ln_gain[l, 2], ln_bias[l, 2])
    return xf.reshape(B, S, D)
```
